```python
import jax, jax.numpy as jnp
from jax import lax
import numpy as np

D_MODEL = 1024
BATCH = 1
SEQ = 16384
DEPTH = 4
DEC_BATCH = 16
DEC_SEQ = 32
PAST_LEN = 4096

CHUNK = 64
D_MIX = 2 * D_MODEL
D_A = D_MIX // 2
A_HEADS = 8
A_DH = D_A // A_HEADS
A_CHUNK = 128
D_B = D_MIX - D_A
B_HEADS = 4
B_DK = D_B // B_HEADS
B_DV = D_B // B_HEADS
D_PROJ = 3 * D_A + 4 * D_B
ROPE_BASE = 10000.0
EPS = 1e-6

kernel_name = "hymba_gmlp_retention_stream"


def _rmsnorm(x, g):
    xf = x.astype(jnp.float32)
    y = xf * lax.rsqrt(jnp.mean(xf * xf, axis=-1, keepdims=True) + EPS)
    return (y * g.astype(jnp.float32)).astype(x.dtype)


def _head_rms(x):
    return x * lax.rsqrt(jnp.mean(x * x, axis=-1, keepdims=True) + EPS)


def _log_decay():
    return jnp.log(1.0 - 2.0 ** (-5.0 - jnp.arange(B_HEADS, dtype=jnp.float32)))


def _rotary(x, pos):
    half = x.shape[-1] // 2
    inv = ROPE_BASE ** (-jnp.arange(half, dtype=jnp.float32) / half)
    ang = pos.astype(jnp.float32)[:, None] * inv[None, :]
    cos = jnp.cos(ang)[None, :, None, :]
    sin = jnp.sin(ang)[None, :, None, :]
    x1, x2 = x[..., :half], x[..., half:]
    return jnp.concatenate([x1 * cos - x2 * sin, x1 * sin + x2 * cos], axis=-1)


def _chunk_mask(c):
    i = jnp.arange(c)
    return (i[None, :] // CHUNK) <= (i[:, None] // CHUNK)


def _spatial_gate(v, w_s, b_s):
    B, L, H, Dh = v.shape
    c = min(L, A_CHUNK)
    n = L // c
    w = jnp.where(_chunk_mask(c)[None], w_s[:, :c, :c], 0.0).astype(v.dtype)
    vb = v.reshape(B, n, c, H, Dh)
    mixed = jnp.einsum("hij,bnjhd->bnihd", w, vb) + b_s[:, :c].T[None, None, :, :, None]
    return mixed.reshape(B, L, H, Dh)


def _retention_block(q, k, v, s, lg):
    L = q.shape[1]
    idx = jnp.arange(L, dtype=jnp.float32)
    diff = idx[:, None] - idx[None, :]
    causal = diff >= 0
    dec = jnp.where(causal[None], jnp.exp(lg[:, None, None] * jnp.where(causal, diff, 0.0)[None]), 0.0)
    scores = jnp.einsum("bihd,bjhd->bhij", q, k) * dec[None]
    intra = jnp.einsum("bhij,bjhe->bihe", scores, v)
    cross = jnp.einsum("bihd,bhde->bihe", q, s) * jnp.exp(lg[None, :] * (idx[:, None] + 1.0))[None, :, :, None]
    kw = k * jnp.exp(lg[None, :] * (L - 1.0 - idx[:, None]))[None, :, :, None]
    s_new = jnp.exp(lg * L)[None, :, None, None] * s + jnp.einsum("bjhd,bjhe->bhde", kw, v)
    return intra + cross, s_new


def _retention(q, k, v, s0):
    B, L = q.shape[0], q.shape[1]
    lg = _log_decay()
    if L <= CHUNK:
        return _retention_block(q, k, v, s0, lg)
    n = L // CHUNK

    def to_blocks(t):
        return jnp.moveaxis(t.reshape(B, n, CHUNK, t.shape[2], t.shape[3]), 1, 0)

    def step(s, blk):
        qb, kb, vb = blk
        o, s = _retention_block(qb, kb, vb, s, lg)
        return s, o

    s_fin, o = lax.scan(step, s0, (to_blocks(q), to_blocks(k), to_blocks(v)))
    o = jnp.moveaxis(o, 0, 1).reshape(B, L, B_HEADS, B_DV)
    return o, s_fin


def _layer(x, pos, s_ret, g_norm, w_in, g_sgu, w_s, b_s, w_out):
    B, L, _ = x.shape
    h = _rmsnorm(x, g_norm)
    p = jnp.einsum("bld,dp->blp", h, w_in)
    splits = [D_A, 2 * D_A, 3 * D_A, 3 * D_A + D_B, 3 * D_A + 2 * D_B, 3 * D_A + 3 * D_B]
    u_a, v_a, z_a, q_b, k_b, v_b, z_b = jnp.split(p, splits, axis=-1)
    v_n = _rmsnorm(v_a.reshape(B, L, A_HEADS, A_DH), g_sgu)
    mixed = _spatial_gate(v_n, w_s, b_s)
    y_a = u_a * mixed.reshape(B, L, D_A) * jax.nn.silu(z_a)
    qf = _rotary(q_b.reshape(B, L, B_HEADS, B_DK).astype(jnp.float32), pos)
    kf = _rotary(k_b.reshape(B, L, B_HEADS, B_DK).astype(jnp.float32), pos) * (B_DK ** -0.5)
    vf = v_b.reshape(B, L, B_HEADS, B_DV).astype(jnp.float32)
    o, s_new = _retention(qf, kf, vf, s_ret.astype(jnp.float32))
    y_b = _head_rms(o).reshape(B, L, D_B).astype(x.dtype) * jax.nn.silu(z_b)
    y = jnp.einsum("blm,md->bld", jnp.concatenate([y_a, y_b], axis=-1), w_out)
    return x + y, s_new, v_n


def setup_inputs(seed: int = 0) -> dict:
    key = jax.random.key(seed)
    ks = jax.random.split(key, 10)
    nrm = jax.random.normal
    return {
        "x_prompt": nrm(ks[0], (BATCH, SEQ, D_MODEL), jnp.float32),
        "x_sample": nrm(ks[1], (DEC_BATCH, DEC_SEQ, D_MODEL), jnp.float32),
        "state_ret": 0.1 * nrm(ks[2], (DEPTH, DEC_BATCH, B_HEADS, B_DK, B_DV), jnp.float32),
        "g_norm": 1.0 + 0.05 * nrm(ks[3], (DEPTH, D_MODEL), jnp.float32),
        "w_in": nrm(ks[4], (DEPTH, D_MODEL, D_PROJ), jnp.float32) * (D_MODEL ** -0.5),
        "g_sgu": 1.0 + 0.05 * nrm(ks[5], (DEPTH, A_HEADS, A_DH), jnp.float32),
        "w_s": nrm(ks[6], (DEPTH, A_HEADS, A_CHUNK, A_CHUNK), jnp.float32) * (0.5 * A_CHUNK ** -0.5),
        "b_s": 1.0 + 0.05 * nrm(ks[7], (DEPTH, A_HEADS, A_CHUNK), jnp.float32),
        "w_out": nrm(ks[8], (DEPTH, D_MIX, D_MODEL), jnp.float32) * (D_MIX ** -0.5),
        "g_final": 1.0 + 0.05 * nrm(ks[9], (D_MODEL,), jnp.float32),
    }


def reference(x_prompt, x_sample, state_ret, g_norm, w_in, g_sgu, w_s, b_s, w_out, g_final):
    pos_p = jnp.arange(x_prompt.shape[1])
    pos_s = PAST_LEN + jnp.arange(x_sample.shape[1])
    hp, hs = x_prompt, x_sample
    ret_p, ret_s, chunk_v = [], [], []
    for l in range(DEPTH):
        s0 = jnp.zeros((hp.shape[0], B_HEADS, B_DK, B_DV), jnp.float32)
        hp, sp, _ = _layer(hp, pos_p, s0, g_norm[l], w_in[l], g_sgu[l], w_s[l], b_s[l], w_out[l])
        hs, ss, vn = _layer(hs, pos_s, state_ret[l], g_norm[l], w_in[l], g_sgu[l], w_s[l], b_s[l], w_out[l])
        ret_p.append(sp.astype(state_ret.dtype))
        ret_s.append(ss.astype(state_ret.dtype))
        chunk_v.append(vn)
    y_prompt = _rmsnorm(hp, g_final)
    y_sample = _rmsnorm(hs, g_final)
    new_ret_prompt = jnp.stack(ret_p, axis=0)
    new_ret_sample = jnp.stack(ret_s, axis=0)
    new_chunk_v = jnp.stack(chunk_v, axis=0)
    return (y_prompt, y_sample, new_ret_prompt, new_ret_sample, new_chunk_v)
```

```python
import functools

import jax
import jax.numpy as jnp
from jax import lax
from jax.experimental import pallas as pl
from jax.experimental.pallas import tpu as pltpu

D_MODEL = 1024
DEPTH = 4
PAST_LEN = 4096
CHUNK = 64
D_MIX = 2 * D_MODEL
D_A = D_MIX // 2
A_HEADS = 8
A_DH = D_A // A_HEADS
A_CHUNK = 128
D_B = D_MIX - D_A
B_HEADS = 4
B_DK = D_B // B_HEADS
B_DV = D_B // B_HEADS
D_PROJ = 3 * D_A + 4 * D_B
ROPE_BASE = 10000.0
EPS = 1e-6
HALF = B_DK // 2

OFF_U, OFF_V, OFF_ZA = 0, D_A, 2 * D_A
OFF_Q, OFF_K, OFF_VB, OFF_ZB = 3 * D_A, 3 * D_A + D_B, 3 * D_A + 2 * D_B, 3 * D_A + 3 * D_B

PROMPT_TILE = 256
VMEM_LIMIT_BYTES = 56 * 1024 * 1024


def _bf(x):
    return x.astype(jnp.bfloat16)


def _dot(a, b):
    return jnp.dot(a, b, preferred_element_type=jnp.float32)


def _dot_nt(a, b):
    return lax.dot_general(a, b, (((1,), (1,)), ((), ())), preferred_element_type=jnp.float32)


def _dot_tn(a, b):
    return lax.dot_general(a, b, (((0,), (0,)), ((), ())), preferred_element_type=jnp.float32)


def _rms(x):
    return x * lax.rsqrt(jnp.mean(x * x, axis=-1, keepdims=True) + EPS)


def _silu(z):
    return z * (1.0 / (1.0 + jnp.exp(-z)))


def _rotary(x, cos, sin):
    x1, x2 = x[:, :HALF], x[:, HALF:]
    return jnp.concatenate([x1 * cos - x2 * sin, x1 * sin + x2 * cos], axis=-1)


def _wide(t):
    return jnp.concatenate([t, t], axis=-1)


def _head_norm_v(v, gsgu):
    parts = [_rms(v[:, h * A_DH:(h + 1) * A_DH]) for h in range(A_HEADS)]
    return jnp.concatenate(parts, axis=-1) * gsgu


def _retention_head(q, k, v, s, dec, rdec, cdec, sdec):
    qb, kb, vb = _bf(q), _bf(k), _bf(v)
    scores = _dot_nt(qb, kb) * dec
    intra = _dot(_bf(scores), vb)
    cross = _dot(qb, _bf(s)) * _wide(rdec)
    kw = _bf(k * _wide(cdec))
    s_new = sdec * s + _dot_tn(kw, vb)
    return intra + cross, s_new


def _prompt_layer_kernel(x_ref, cos_ref, sin_ref, gn_ref, win_ref, gsgu_ref, ws_ref, bs_ref,
                         dec_ref, rdec_ref, cdec_ref, sdec_ref, wout_ref, gfin_ref,
                         y_ref, s_ref, mix_ref, *, final_norm):
    tile = x_ref.shape[0]

    @pl.when(pl.program_id(0) == 0)
    def _():
        s_ref[...] = jnp.zeros_like(s_ref)

    x = x_ref[...]
    h = _bf(_rms(x) * gn_ref[...])

    u = _dot(h, win_ref[:, OFF_U:OFF_U + D_A])
    v = _dot(h, win_ref[:, OFF_V:OFF_V + D_A])
    za = _dot(h, win_ref[:, OFF_ZA:OFF_ZA + D_A])
    vn = _bf(_head_norm_v(v, gsgu_ref[...]))
    rows = []
    for c in range(tile // A_CHUNK):
        cols = [_dot(ws_ref[hd], vn[c * A_CHUNK:(c + 1) * A_CHUNK, hd * A_DH:(hd + 1) * A_DH])
                for hd in range(A_HEADS)]
        rows.append(jnp.concatenate(cols, axis=-1) + bs_ref[...])
    mixed = jnp.concatenate(rows, axis=0)
    mix_ref[:, 0:D_A] = _bf(u * mixed * _silu(za))

    cos = cos_ref[...]
    sin = sin_ref[...]
    for hb in range(B_HEADS):
        c0 = hb * B_DK
        q = _rotary(_dot(h, win_ref[:, OFF_Q + c0:OFF_Q + c0 + B_DK]), cos, sin)
        k = _rotary(_dot(h, win_ref[:, OFF_K + c0:OFF_K + c0 + B_DK]), cos, sin) * (B_DK ** -0.5)
        vb = _dot(h, win_ref[:, OFF_VB + c0:OFF_VB + c0 + B_DV])
        zb = _dot(h, win_ref[:, OFF_ZB + c0:OFF_ZB + c0 + B_DV])
        o, s_new = _retention_head(q, k, vb, s_ref[hb], dec_ref[hb], rdec_ref[hb], cdec_ref[hb],
                                   sdec_ref[hb])
        s_ref[hb] = s_new
        mix_ref[:, D_A + c0:D_A + c0 + B_DV] = _bf(_rms(o) * _silu(zb))

    y = x + _dot(mix_ref[...], wout_ref[...])
    if final_norm:
        y = _rms(y) * gfin_ref[...]
    y_ref[...] = y


def _sample_layer_kernel(x_ref, s_in_ref, cos_ref, sin_ref, gn_ref, win_ref, gsgu_ref, ws_ref, bs_ref,
                         dec_ref, rdec_ref, cdec_ref, sdec_ref, wout_ref, gfin_ref,
                         y_ref, s_out_ref, vn_ref, qkvz_ref, mix_ref, *, final_norm, seq):
    b = pl.program_id(0)
    nb = pl.num_programs(0)

    @pl.when(b == 0)
    def _():
        x = x_ref[...]
        h = _bf(_rms(x) * gn_ref[...])
        u = _dot(h, win_ref[:, OFF_U:OFF_U + D_A])
        v = _dot(h, win_ref[:, OFF_V:OFF_V + D_A])
        za = _dot(h, win_ref[:, OFF_ZA:OFF_ZA + D_A])
        vn = _head_norm_v(v, gsgu_ref[...])
        vn_ref[...] = vn
        vnb = _bf(vn)
        rows = []
        for bb in range(x.shape[0] // seq):
            cols = [_dot(ws_ref[hd], vnb[bb * seq:(bb + 1) * seq, hd * A_DH:(hd + 1) * A_DH])
                    for hd in range(A_HEADS)]
            rows.append(jnp.concatenate(cols, axis=-1) + bs_ref[...])
        mixed = jnp.concatenate(rows, axis=0)
        mix_ref[:, 0:D_A] = _bf(u * mixed * _silu(za))
        qkvz_ref[...] = _dot(h, win_ref[:, OFF_Q:OFF_Q + 4 * D_B])

    r0 = pl.multiple_of(b * seq, seq)
    rsl = pl.ds(r0, seq)
    cos = cos_ref[...]
    sin = sin_ref[...]
    for hb in range(B_HEADS):
        c0 = hb * B_DK
        q = _rotary(qkvz_ref[rsl, c0:c0 + B_DK], cos, sin)
        k = _rotary(qkvz_ref[rsl, D_B + c0:D_B + c0 + B_DK], cos, sin) * (B_DK ** -0.5)
        vb = qkvz_ref[rsl, 2 * D_B + c0:2 * D_B + c0 + B_DV]
        zb = qkvz_ref[rsl, 3 * D_B + c0:3 * D_B + c0 + B_DV]
        o, s_new = _retention_head(q, k, vb, s_in_ref[0, hb], dec_ref[hb], rdec_ref[hb], cdec_ref[hb],
                                   sdec_ref[hb])
        s_out_ref[0, hb] = s_new
        mix_ref[rsl, D_A + c0:D_A + c0 + B_DV] = _bf(_rms(o) * _silu(zb))

    @pl.when(b == nb - 1)
    def _():
        y = x_ref[...] + _dot(mix_ref[...], wout_ref[...])
        if final_norm:
            y = _rms(y) * gfin_ref[...]
        y_ref[...] = y


def _const_spec(shape):
    zeros = (0,) * len(shape)
    return pl.BlockSpec(shape, lambda i: zeros, pipeline_mode=pl.Buffered(1))


def _decay_tables(length):
    lg = jnp.log(1.0 - 2.0 ** (-5.0 - jnp.arange(B_HEADS, dtype=jnp.float32)))
    idx = jnp.arange(length, dtype=jnp.float32)
    diff = idx[:, None] - idx[None, :]
    causal = diff >= 0
    dec = jnp.where(causal[None], jnp.exp(lg[:, None, None] * jnp.where(causal, diff, 0.0)[None]), 0.0)
    rdec = jnp.exp(lg[:, None] * (idx[None, :] + 1.0))
    cdec = jnp.exp(lg[:, None] * (length - 1.0 - idx[None, :]))
    sdec = jnp.exp(lg * length)
    rdec = jnp.broadcast_to(rdec[:, :, None], (B_HEADS, length, 128))
    cdec = jnp.broadcast_to(cdec[:, :, None], (B_HEADS, length, 128))
    sdec = jnp.broadcast_to(sdec[:, None, None], (B_HEADS, 1, B_DV))
    return dec, rdec, cdec, sdec


def _rope_tables(pos):
    inv = ROPE_BASE ** (-jnp.arange(HALF, dtype=jnp.float32) / HALF)
    ang = pos.astype(jnp.float32)[:, None] * inv[None, :]
    return jnp.cos(ang), jnp.sin(ang)


def _gate_weights(w_s, b_s, c):
    i = jnp.arange(c)
    mask = (i[None, :] // CHUNK) <= (i[:, None] // CHUNK)
    w = _bf(jnp.where(mask[None], w_s[:, :c, :c], 0.0))
    bias = jnp.broadcast_to(b_s[:, :c].T[:, :, None], (c, A_HEADS, A_DH)).reshape(c, D_A)
    return w, bias


def _prompt_layer(x, cos, sin, gn, win, gsgu, ws, bs, decs, wout, gfin, final_norm):
    seq = x.shape[0]
    tile = PROMPT_TILE
    dec, rdec, cdec, sdec = decs
    row = lambda i: (i, 0)
    kern = functools.partial(_prompt_layer_kernel, final_norm=final_norm)
    return pl.pallas_call(
        kern,
        grid=(seq // tile,),
        in_specs=[
            pl.BlockSpec((tile, D_MODEL), row),
            pl.BlockSpec((tile, HALF), row),
            pl.BlockSpec((tile, HALF), row),
            _const_spec((1, D_MODEL)),
            _const_spec((D_MODEL, D_PROJ)),
            _const_spec((1, D_A)),
            _const_spec((A_HEADS, A_CHUNK, A_CHUNK)),
            _const_spec((A_CHUNK, D_A)),
            _const_spec((B_HEADS, tile, tile)),
            _const_spec((B_HEADS, tile, 128)),
            _const_spec((B_HEADS, tile, 128)),
            _const_spec((B_HEADS, 1, B_DV)),
            _const_spec((D_MIX, D_MODEL)),
            _const_spec((1, D_MODEL)),
        ],
        out_specs=[
            pl.BlockSpec((tile, D_MODEL), row),
            pl.BlockSpec((B_HEADS, B_DK, B_DV), lambda i: (0, 0, 0)),
        ],
        out_shape=[
            jax.ShapeDtypeStruct((seq, D_MODEL), jnp.float32),
            jax.ShapeDtypeStruct((B_HEADS, B_DK, B_DV), jnp.float32),
        ],
        scratch_shapes=[pltpu.VMEM((tile, D_MIX), jnp.bfloat16)],
        compiler_params=pltpu.CompilerParams(
            dimension_semantics=("arbitrary",), vmem_limit_bytes=VMEM_LIMIT_BYTES),
        name="prompt_layer",
    )(x, cos, sin, gn, win, gsgu, ws, bs, dec, rdec, cdec, sdec, wout, gfin)


def _sample_layer(x, state, cos, sin, gn, win, gsgu, ws, bs, decs, wout, gfin, final_norm, seq):
    rows = x.shape[0]
    nb = rows // seq
    dec, rdec, cdec, sdec = decs
    kern = functools.partial(_sample_layer_kernel, final_norm=final_norm, seq=seq)
    state_spec = pl.BlockSpec((1, B_HEADS, B_DK, B_DV), lambda i: (i, 0, 0, 0))
    return pl.pallas_call(
        kern,
        grid=(nb,),
        in_specs=[
            _const_spec((rows, D_MODEL)),
            state_spec,
            _const_spec((seq, HALF)),
            _const_spec((seq, HALF)),
            _const_spec((1, D_MODEL)),
            _const_spec((D_MODEL, D_PROJ)),
            _const_spec((1, D_A)),
            _const_spec((A_HEADS, seq, seq)),
            _const_spec((seq, D_A)),
            _const_spec((B_HEADS, seq, seq)),
            _const_spec((B_HEADS, seq, 128)),
            _const_spec((B_HEADS, seq, 128)),
            _const_spec((B_HEADS, 1, B_DV)),
            _const_spec((D_MIX, D_MODEL)),
            _const_spec((1, D_MODEL)),
        ],
        out_specs=[
            pl.BlockSpec((rows, D_MODEL), lambda i: (0, 0)),
            state_spec,
            pl.BlockSpec((rows, D_A), lambda i: (0, 0)),
        ],
        out_shape=[
            jax.ShapeDtypeStruct((rows, D_MODEL), jnp.float32),
            jax.ShapeDtypeStruct(state.shape, jnp.float32),
            jax.ShapeDtypeStruct((rows, D_A), jnp.float32),
        ],
        scratch_shapes=[
            pltpu.VMEM((rows, 4 * D_B), jnp.float32),
            pltpu.VMEM((rows, D_MIX), jnp.bfloat16),
        ],
        compiler_params=pltpu.CompilerParams(
            dimension_semantics=("arbitrary",), vmem_limit_bytes=VMEM_LIMIT_BYTES),
        name="sample_layer",
    )(x, state, cos, sin, gn, win, gsgu, ws, bs, dec, rdec, cdec, sdec, wout, gfin)


def kernel(x_prompt, x_sample, state_ret, g_norm, w_in, g_sgu, w_s, b_s, w_out, g_final):
    batch, seq, _ = x_prompt.shape
    dec_batch, dec_seq, _ = x_sample.shape
    assert batch == 1 and seq % PROMPT_TILE == 0 and PROMPT_TILE % A_CHUNK == 0
    assert dec_seq <= CHUNK

    cos_p, sin_p = _rope_tables(jnp.arange(seq))
    cos_s, sin_s = _rope_tables(PAST_LEN + jnp.arange(dec_seq))
    decs_p = _decay_tables(PROMPT_TILE)
    decs_s = _decay_tables(dec_seq)
    gfin = g_final.reshape(1, D_MODEL)

    hp = x_prompt.reshape(seq, D_MODEL)
    hs = x_sample.reshape(dec_batch * dec_seq, D_MODEL)
    ret_p, ret_s, chunk_v = [], [], []
    for l in range(DEPTH):
        last = l == DEPTH - 1
        gn = g_norm[l].reshape(1, D_MODEL)
        gsgu = g_sgu[l].reshape(1, D_A)
        win = _bf(w_in[l])
        wout = _bf(w_out[l])
        ws_p, bs_p = _gate_weights(w_s[l], b_s[l], A_CHUNK)
        ws_s, bs_s = _gate_weights(w_s[l], b_s[l], dec_seq)
        hp, sp = _prompt_layer(hp, cos_p, sin_p, gn, win, gsgu, ws_p, bs_p, decs_p, wout, gfin, last)
        hs, ss, vn = _sample_layer(hs, state_ret[l], cos_s, sin_s, gn, win, gsgu, ws_s, bs_s, decs_s,
                                   wout, gfin, last, dec_seq)
        ret_p.append(sp.reshape(batch, B_HEADS, B_DK, B_DV))
        ret_s.append(ss)
        chunk_v.append(vn.reshape(dec_batch, dec_seq, A_HEADS, A_DH))
    y_prompt = hp.reshape(batch, seq, D_MODEL)
    y_sample = hs.reshape(dec_batch, dec_seq, D_MODEL)
    return (y_prompt, y_sample, jnp.stack(ret_p, axis=0), jnp.stack(ret_s, axis=0),
            jnp.stack(chunk_v, axis=0))
```

```python
import functools

import jax
import jax.numpy as jnp
from jax import lax
from jax.experimental import pallas as pl
from jax.experimental.pallas import tpu as pltpu

D_MODEL = 1024
DEPTH = 4
PAST_LEN = 4096
CHUNK = 64
D_MIX = 2 * D_MODEL
D_A = D_MIX // 2
A_HEADS = 8
A_DH = D_A // A_HEADS
A_CHUNK = 128
D_B = D_MIX - D_A
B_HEADS = 4
B_DK = D_B // B_HEADS
B_DV = D_B // B_HEADS
D_PROJ = 3 * D_A + 4 * D_B
ROPE_BASE = 10000.0
EPS = 1e-6
HALF = B_DK // 2
ROPE_SPLIT = 128

OFF_U, OFF_V, OFF_ZA = 0, D_A, 2 * D_A
OFF_Q, OFF_K, OFF_VB, OFF_ZB = 3 * D_A, 3 * D_A + D_B, 3 * D_A + 2 * D_B, 3 * D_A + 3 * D_B

PROMPT_TILE = 256
VMEM_LIMIT_BYTES = 56 * 1024 * 1024


def _bf(x):
    return x.astype(jnp.bfloat16)


def _dot(a, b):
    return jnp.dot(a, b, preferred_element_type=jnp.float32)


def _dot_nt(a, b):
    return lax.dot_general(a, b, (((1,), (1,)), ((), ())), preferred_element_type=jnp.float32)


def _dot_tn(a, b):
    return lax.dot_general(a, b, (((0,), (0,)), ((), ())), preferred_element_type=jnp.float32)


def _rms(x):
    return x * lax.rsqrt(jnp.mean(x * x, axis=-1, keepdims=True) + EPS)


def _silu(z):
    return z * (1.0 / (1.0 + jnp.exp(-z)))


def _rotary(x, cos, sin):
    x1, x2 = x[:, :HALF], x[:, HALF:]
    return jnp.concatenate([x1 * cos - x2 * sin, x1 * sin + x2 * cos], axis=-1)


def _wide(t):
    return jnp.concatenate([t, t], axis=-1)


def _head_norm_v(v, gsgu):
    parts = [_rms(v[:, h * A_DH:(h + 1) * A_DH]) for h in range(A_HEADS)]
    return jnp.concatenate(parts, axis=-1) * gsgu


def _retention_head(q, k, v, s, dec, rdec, cdec, sdec):
    qb, kb, vb = _bf(q), _bf(k), _bf(v)
    scores = _dot_nt(qb, kb) * dec
    intra = _dot(_bf(scores), vb)
    cross = _dot(qb, _bf(s)) * _wide(rdec)
    kw = _bf(k * _wide(cdec))
    s_new = sdec * s + _dot_tn(kw, vb)
    return intra + cross, s_new


def _prompt_layer_kernel(x_ref, cos_ref, sin_ref, gn_ref, win_ref, gsgu_ref, ws_ref, bs_ref,
                         dec_ref, rdec_ref, cdec_ref, sdec_ref, wout_ref, gfin_ref,
                         y_ref, s_ref, mix_ref, *, final_norm):
    tile = x_ref.shape[0]

    @pl.when(pl.program_id(0) == 0)
    def _():
        s_ref[...] = jnp.zeros_like(s_ref)

    x = x_ref[...]
    h = _bf(_rms(x) * gn_ref[...])

    u = _dot(h, win_ref[:, OFF_U:OFF_U + D_A])
    v = _dot(h, win_ref[:, OFF_V:OFF_V + D_A])
    za = _dot(h, win_ref[:, OFF_ZA:OFF_ZA + D_A])
    vn = _bf(_head_norm_v(v, gsgu_ref[...]))
    rows = []
    for c in range(tile // A_CHUNK):
        cols = [_dot(ws_ref[hd], vn[c * A_CHUNK:(c + 1) * A_CHUNK, hd * A_DH:(hd + 1) * A_DH])
                for hd in range(A_HEADS)]
        rows.append(jnp.concatenate(cols, axis=-1) + bs_ref[...])
    mixed = jnp.concatenate(rows, axis=0)
    mix_ref[:, 0:D_A] = _bf(u * mixed * _silu(za))

    cos = cos_ref[...]
    sin = sin_ref[...]
    for hb in range(B_HEADS):
        c0 = hb * B_DK
        q = _rotary(_dot(h, win_ref[:, OFF_Q + c0:OFF_Q + c0 + B_DK]), cos, sin)
        k = _rotary(_dot(h, win_ref[:, OFF_K + c0:OFF_K + c0 + B_DK]), cos, sin) * (B_DK ** -0.5)
        vb = _dot(h, win_ref[:, OFF_VB + c0:OFF_VB + c0 + B_DV])
        zb = _dot(h, win_ref[:, OFF_ZB + c0:OFF_ZB + c0 + B_DV])
        o, s_new = _retention_head(q, k, vb, s_ref[hb], dec_ref[hb], rdec_ref[hb], cdec_ref[hb],
                                   sdec_ref[hb])
        s_ref[hb] = s_new
        mix_ref[:, D_A + c0:D_A + c0 + B_DV] = _bf(_rms(o) * _silu(zb))

    y = x + _dot(mix_ref[...], wout_ref[...])
    if final_norm:
        y = _rms(y) * gfin_ref[...]
    y_ref[...] = y


def _sample_layer_kernel(x_ref, s_in_ref, cos_ref, sin_ref, gn_ref, win_ref, gsgu_ref, ws_ref, bs_ref,
                         dec_ref, rdec_ref, cdec_ref, sdec_ref, wout_ref, gfin_ref,
                         y_ref, s_out_ref, vn_ref, qkvz_ref, mix_ref, *, final_norm, seq):
    b = pl.program_id(0)
    nb = pl.num_programs(0)

    @pl.when(b == 0)
    def _():
        x = x_ref[...]
        h = _bf(_rms(x) * gn_ref[...])
        u = _dot(h, win_ref[:, OFF_U:OFF_U + D_A])
        v = _dot(h, win_ref[:, OFF_V:OFF_V + D_A])
        za = _dot(h, win_ref[:, OFF_ZA:OFF_ZA + D_A])
        vn = _head_norm_v(v, gsgu_ref[...])
        vn_ref[...] = vn
        vnb = _bf(vn)
        rows = []
        for bb in range(x.shape[0] // seq):
            cols = [_dot(ws_ref[hd], vnb[bb * seq:(bb + 1) * seq, hd * A_DH:(hd + 1) * A_DH])
                    for hd in range(A_HEADS)]
            rows.append(jnp.concatenate(cols, axis=-1) + bs_ref[...])
        mixed = jnp.concatenate(rows, axis=0)
        mix_ref[:, 0:D_A] = _bf(u * mixed * _silu(za))
        qkvz_ref[...] = _dot(h, win_ref[:, OFF_Q:OFF_Q + 4 * D_B])

    r0 = pl.multiple_of(b * seq, seq)
    rsl = pl.ds(r0, seq)
    cos = cos_ref[...]
    sin = sin_ref[...]
    for hb in range(B_HEADS):
        c0 = hb * B_DK
        q = _rotary(qkvz_ref[rsl, c0:c0 + B_DK], cos, sin)
        k = _rotary(qkvz_ref[rsl, D_B + c0:D_B + c0 + B_DK], cos, sin) * (B_DK ** -0.5)
        vb = qkvz_ref[rsl, 2 * D_B + c0:2 * D_B + c0 + B_DV]
        zb = qkvz_ref[rsl, 3 * D_B + c0:3 * D_B + c0 + B_DV]
        o, s_new = _retention_head(q, k, vb, s_in_ref[0, hb], dec_ref[hb], rdec_ref[hb], cdec_ref[hb],
                                   sdec_ref[hb])
        s_out_ref[0, hb] = s_new
        mix_ref[rsl, D_A + c0:D_A + c0 + B_DV] = _bf(_rms(o) * _silu(zb))

    @pl.when(b == nb - 1)
    def _():
        y = x_ref[...] + _dot(mix_ref[...], wout_ref[...])
        if final_norm:
            y = _rms(y) * gfin_ref[...]
        y_ref[...] = y


def _const_spec(shape):
    zeros = (0,) * len(shape)
    return pl.BlockSpec(shape, lambda i: zeros, pipeline_mode=pl.Buffered(1))


def _layer_spec(shape, layer):
    zeros = (0,) * len(shape)
    return pl.BlockSpec((None,) + shape, lambda i: (layer,) + zeros, pipeline_mode=pl.Buffered(1))


def _decay_tables(length):
    lg = jnp.log(1.0 - 2.0 ** (-5.0 - jnp.arange(B_HEADS, dtype=jnp.float32)))
    idx = jnp.arange(length, dtype=jnp.float32)
    diff = idx[:, None] - idx[None, :]
    causal = diff >= 0
    dec = jnp.where(causal[None], jnp.exp(lg[:, None, None] * jnp.where(causal, diff, 0.0)[None]), 0.0)
    rdec = jnp.exp(lg[:, None] * (idx[None, :] + 1.0))
    cdec = jnp.exp(lg[:, None] * (length - 1.0 - idx[None, :]))
    sdec = jnp.exp(lg * length)
    rdec = jnp.broadcast_to(rdec[:, :, None], (B_HEADS, length, 128))
    cdec = jnp.broadcast_to(cdec[:, :, None], (B_HEADS, length, 128))
    sdec = jnp.broadcast_to(sdec[:, None, None], (B_HEADS, 1, B_DV))
    return dec, rdec, cdec, sdec


def _rope_tables(start, length):
    inv = ROPE_BASE ** (-jnp.arange(HALF, dtype=jnp.float32) / HALF)
    if length <= ROPE_SPLIT:
        ang = (start + jnp.arange(length)).astype(jnp.float32)[:, None] * inv[None, :]
        return jnp.cos(ang), jnp.sin(ang)
    assert length % ROPE_SPLIT == 0
    hi = (start + ROPE_SPLIT * jnp.arange(length // ROPE_SPLIT)).astype(jnp.float32)[:, None] * inv[None, :]
    lo = jnp.arange(ROPE_SPLIT).astype(jnp.float32)[:, None] * inv[None, :]
    ch, sh = jnp.cos(hi)[:, None, :], jnp.sin(hi)[:, None, :]
    cl, sl = jnp.cos(lo)[None, :, :], jnp.sin(lo)[None, :, :]
    cos = (ch * cl - sh * sl).reshape(length, HALF)
    sin = (sh * cl + ch * sl).reshape(length, HALF)
    return cos, sin


def _gate_weights(w_s, b_s, c):
    i = jnp.arange(c)
    mask = (i[None, :] // CHUNK) <= (i[:, None] // CHUNK)
    w = _bf(jnp.where(mask[None], w_s[:, :c, :c], 0.0))
    bias = jnp.broadcast_to(b_s[:, :c].T[:, :, None], (c, A_HEADS, A_DH)).reshape(c, D_A)
    return w, bias


def _prompt_layer(layer, x, cos, sin, gn, win, gsgu, ws, bs, decs, wout, gfin, final_norm):
    seq = x.shape[0]
    tile = PROMPT_TILE
    dec, rdec, cdec, sdec = decs
    row = lambda i: (i, 0)
    kern = functools.partial(_prompt_layer_kernel, final_norm=final_norm)
    return pl.pallas_call(
        kern,
        grid=(seq // tile,),
        in_specs=[
            pl.BlockSpec((tile, D_MODEL), row),
            pl.BlockSpec((tile, HALF), row),
            pl.BlockSpec((tile, HALF), row),
            _layer_spec((1, D_MODEL), layer),
            _layer_spec((D_MODEL, D_PROJ), layer),
            _layer_spec((1, D_A), layer),
            _const_spec((A_HEADS, A_CHUNK, A_CHUNK)),
            _const_spec((A_CHUNK, D_A)),
            _const_spec((B_HEADS, tile, tile)),
            _const_spec((B_HEADS, tile, 128)),
            _const_spec((B_HEADS, tile, 128)),
            _const_spec((B_HEADS, 1, B_DV)),
            _layer_spec((D_MIX, D_MODEL), layer),
            _const_spec((1, D_MODEL)),
        ],
        out_specs=[
            pl.BlockSpec((tile, D_MODEL), row),
            pl.BlockSpec((B_HEADS, B_DK, B_DV), lambda i: (0, 0, 0)),
        ],
        out_shape=[
            jax.ShapeDtypeStruct((seq, D_MODEL), jnp.float32),
            jax.ShapeDtypeStruct((B_HEADS, B_DK, B_DV), jnp.float32),
        ],
        scratch_shapes=[pltpu.VMEM((tile, D_MIX), jnp.bfloat16)],
        compiler_params=pltpu.CompilerParams(
            dimension_semantics=("arbitrary",), vmem_limit_bytes=VMEM_LIMIT_BYTES),
        name="prompt_layer",
    )(x, cos, sin, gn, win, gsgu, ws, bs, dec, rdec, cdec, sdec, wout, gfin)


def _sample_layer(layer, x, state, cos, sin, gn, win, gsgu, ws, bs, decs, wout, gfin, final_norm, seq):
    rows = x.shape[0]
    nb = rows // seq
    dec, rdec, cdec, sdec = decs
    kern = functools.partial(_sample_layer_kernel, final_norm=final_norm, seq=seq)
    state_in_spec = pl.BlockSpec((None, 1, B_HEADS, B_DK, B_DV), lambda i: (layer, i, 0, 0, 0))
    state_out_spec = pl.BlockSpec((1, B_HEADS, B_DK, B_DV), lambda i: (i, 0, 0, 0))
    return pl.pallas_call(
        kern,
        grid=(nb,),
        in_specs=[
            _const_spec((rows, D_MODEL)),
            state_in_spec,
            _const_spec((seq, HALF)),
            _const_spec((seq, HALF)),
            _layer_spec((1, D_MODEL), layer),
            _layer_spec((D_MODEL, D_PROJ), layer),
            _layer_spec((1, D_A), layer),
            _const_spec((A_HEADS, seq, seq)),
            _const_spec((seq, D_A)),
            _const_spec((B_HEADS, seq, seq)),
            _const_spec((B_HEADS, seq, 128)),
            _const_spec((B_HEADS, seq, 128)),
            _const_spec((B_HEADS, 1, B_DV)),
            _layer_spec((D_MIX, D_MODEL), layer),
            _const_spec((1, D_MODEL)),
        ],
        out_specs=[
            pl.BlockSpec((rows, D_MODEL), lambda i: (0, 0)),
            state_out_spec,
            pl.BlockSpec((rows, D_A), lambda i: (0, 0)),
        ],
        out_shape=[
            jax.ShapeDtypeStruct((rows, D_MODEL), jnp.float32),
            jax.ShapeDtypeStruct(state.shape[1:], jnp.float32),
            jax.ShapeDtypeStruct((rows, D_A), jnp.float32),
        ],
        scratch_shapes=[
            pltpu.VMEM((rows, 4 * D_B), jnp.float32),
            pltpu.VMEM((rows, D_MIX), jnp.bfloat16),
        ],
        compiler_params=pltpu.CompilerParams(
            dimension_semantics=("arbitrary",), vmem_limit_bytes=VMEM_LIMIT_BYTES),
        name="sample_layer",
    )(x, state, cos, sin, gn, win, gsgu, ws, bs, dec, rdec, cdec, sdec, wout, gfin)


def kernel(x_prompt, x_sample, state_ret, g_norm, w_in, g_sgu, w_s, b_s, w_out, g_final):
    batch, seq, _ = x_prompt.shape
    dec_batch, dec_seq, _ = x_sample.shape
    assert batch == 1 and seq % PROMPT_TILE == 0 and PROMPT_TILE % A_CHUNK == 0
    assert dec_seq <= CHUNK

    cos_p, sin_p = _rope_tables(0, seq)
    cos_s, sin_s = _rope_tables(PAST_LEN, dec_seq)
    decs_p = _decay_tables(PROMPT_TILE)
    decs_s = _decay_tables(dec_seq)
    gfin = g_final.reshape(1, D_MODEL)
    gn = g_norm.reshape(DEPTH, 1, D_MODEL)
    gsgu = g_sgu.reshape(DEPTH, 1, D_A)
    win = _bf(w_in)
    wout = _bf(w_out)

    hp = x_prompt.reshape(seq, D_MODEL)
    hs = x_sample.reshape(dec_batch * dec_seq, D_MODEL)
    ret_p, ret_s, chunk_v = [], [], []
    for l in range(DEPTH):
        last = l == DEPTH - 1
        ws_p, bs_p = _gate_weights(w_s[l], b_s[l], A_CHUNK)
        ws_s, bs_s = _gate_weights(w_s[l], b_s[l], dec_seq)
        hp, sp = _prompt_layer(l, hp, cos_p, sin_p, gn, win, gsgu, ws_p, bs_p, decs_p, wout, gfin, last)
        hs, ss, vn = _sample_layer(l, hs, state_ret, cos_s, sin_s, gn, win, gsgu, ws_s, bs_s, decs_s,
                                   wout, gfin, last, dec_seq)
        ret_p.append(sp.reshape(batch, B_HEADS, B_DK, B_DV))
        ret_s.append(ss)
        chunk_v.append(vn.reshape(dec_batch, dec_seq, A_HEADS, A_DH))
    y_prompt = hp.reshape(batch, seq, D_MODEL)
    y_sample = hs.reshape(dec_batch, dec_seq, D_MODEL)
    return (y_prompt, y_sample, jnp.stack(ret_p, axis=0), jnp.stack(ret_s, axis=0),
            jnp.stack(chunk_v, axis=0))
```

```python
import functools

import jax
import jax.numpy as jnp
from jax import lax
from jax.experimental import pallas as pl
from jax.experimental.pallas import tpu as pltpu

D_MODEL = 1024
DEPTH = 4
PAST_LEN = 4096
CHUNK = 64
D_MIX = 2 * D_MODEL
D_A = D_MIX // 2
A_HEADS = 8
A_DH = D_A // A_HEADS
A_CHUNK = 128
D_B = D_MIX - D_A
B_HEADS = 4
B_DK = D_B // B_HEADS
B_DV = D_B // B_HEADS
D_PROJ = 3 * D_A + 4 * D_B
ROPE_BASE = 10000.0
EPS = 1e-6
HALF = B_DK // 2
ROPE_SPLIT = 128

OFF_U, OFF_V, OFF_ZA = 0, D_A, 2 * D_A
OFF_Q, OFF_K, OFF_VB, OFF_ZB = 3 * D_A, 3 * D_A + D_B, 3 * D_A + 2 * D_B, 3 * D_A + 3 * D_B

RET_BLOCK = 256
PROMPT_TILE = 512
VMEM_LIMIT_BYTES = 56 * 1024 * 1024


def _bf(x):
    return x.astype(jnp.bfloat16)


def _dot(a, b):
    return jnp.dot(a, b, preferred_element_type=jnp.float32)


def _dot_nt(a, b):
    return lax.dot_general(a, b, (((1,), (1,)), ((), ())), preferred_element_type=jnp.float32)


def _dot_tn(a, b):
    return lax.dot_general(a, b, (((0,), (0,)), ((), ())), preferred_element_type=jnp.float32)


def _rms(x):
    return x * lax.rsqrt(jnp.mean(x * x, axis=-1, keepdims=True) + EPS)


def _silu(z):
    return z * (1.0 / (1.0 + jnp.exp(-z)))


def _rotary(x, cos, sin):
    x1, x2 = x[:, :HALF], x[:, HALF:]
    return jnp.concatenate([x1 * cos - x2 * sin, x1 * sin + x2 * cos], axis=-1)


def _wide(t):
    return jnp.concatenate([t, t], axis=-1)


def _head_norm_v(v, gsgu):
    parts = [_rms(v[:, h * A_DH:(h + 1) * A_DH]) for h in range(A_HEADS)]
    return jnp.concatenate(parts, axis=-1) * gsgu


def _retention_head(q, k, v, s, dec, rdec, cdec, sdec):
    qb, kb, vb = _bf(q), _bf(k), _bf(v)
    scores = _dot_nt(qb, kb) * dec
    intra = _dot(_bf(scores), vb)
    cross = _dot(qb, _bf(s)) * _wide(rdec)
    kw = _bf(k * _wide(cdec))
    s_new = sdec * s + _dot_tn(kw, vb)
    return intra + cross, s_new


def _prompt_layer_kernel(x_ref, cos_ref, sin_ref, gn_ref, win_ref, gsgu_ref, ws_ref, bs_ref,
                         dec_ref, rdec_ref, cdec_ref, sdec_ref, wout_ref, gfin_ref,
                         y_ref, s_ref, mix_ref, *, final_norm):
    tile = x_ref.shape[0]

    @pl.when(pl.program_id(0) == 0)
    def _():
        s_ref[...] = jnp.zeros_like(s_ref)

    for blk in range(tile // RET_BLOCK):
        r = slice(blk * RET_BLOCK, (blk + 1) * RET_BLOCK)
        x = x_ref[r, :]
        h = _bf(_rms(x) * gn_ref[...])

        u = _dot(h, win_ref[:, OFF_U:OFF_U + D_A])
        v = _dot(h, win_ref[:, OFF_V:OFF_V + D_A])
        za = _dot(h, win_ref[:, OFF_ZA:OFF_ZA + D_A])
        vn = _bf(_head_norm_v(v, gsgu_ref[...]))
        rows = []
        for c in range(RET_BLOCK // A_CHUNK):
            cols = [_dot(ws_ref[hd], vn[c * A_CHUNK:(c + 1) * A_CHUNK, hd * A_DH:(hd + 1) * A_DH])
                    for hd in range(A_HEADS)]
            rows.append(jnp.concatenate(cols, axis=-1) + bs_ref[...])
        mixed = jnp.concatenate(rows, axis=0)
        mix_ref[r, 0:D_A] = _bf(u * mixed * _silu(za))

        cos = cos_ref[r, :]
        sin = sin_ref[r, :]
        for hb in range(B_HEADS):
            c0 = hb * B_DK
            q = _rotary(_dot(h, win_ref[:, OFF_Q + c0:OFF_Q + c0 + B_DK]), cos, sin)
            k = _rotary(_dot(h, win_ref[:, OFF_K + c0:OFF_K + c0 + B_DK]), cos, sin) * (B_DK ** -0.5)
            vb = _dot(h, win_ref[:, OFF_VB + c0:OFF_VB + c0 + B_DV])
            zb = _dot(h, win_ref[:, OFF_ZB + c0:OFF_ZB + c0 + B_DV])
            o, s_new = _retention_head(q, k, vb, s_ref[hb], dec_ref[hb], rdec_ref[hb], cdec_ref[hb],
                                       sdec_ref[hb])
            s_ref[hb] = s_new
            mix_ref[r, D_A + c0:D_A + c0 + B_DV] = _bf(_rms(o) * _silu(zb))

        y = x + _dot(mix_ref[r, :], wout_ref[...])
        if final_norm:
            y = _rms(y) * gfin_ref[...]
        y_ref[r, :] = y


def _sample_layer_kernel(x_ref, s_in_ref, cos_ref, sin_ref, gn_ref, win_ref, gsgu_ref, ws_ref, bs_ref,
                         dec_ref, rdec_ref, cdec_ref, sdec_ref, wout_ref, gfin_ref,
                         y_ref, s_out_ref, vn_ref, qkvz_ref, mix_ref, *, final_norm, seq):
    b = pl.program_id(0)
    nb = pl.num_programs(0)

    @pl.when(b == 0)
    def _():
        x = x_ref[...]
        h = _bf(_rms(x) * gn_ref[...])
        u = _dot(h, win_ref[:, OFF_U:OFF_U + D_A])
        v = _dot(h, win_ref[:, OFF_V:OFF_V + D_A])
        za = _dot(h, win_ref[:, OFF_ZA:OFF_ZA + D_A])
        vn = _head_norm_v(v, gsgu_ref[...])
        vn_ref[...] = vn
        vnb = _bf(vn)
        rows = []
        for bb in range(x.shape[0] // seq):
            cols = [_dot(ws_ref[hd], vnb[bb * seq:(bb + 1) * seq, hd * A_DH:(hd + 1) * A_DH])
                    for hd in range(A_HEADS)]
            rows.append(jnp.concatenate(cols, axis=-1) + bs_ref[...])
        mixed = jnp.concatenate(rows, axis=0)
        mix_ref[:, 0:D_A] = _bf(u * mixed * _silu(za))
        qkvz_ref[...] = _dot(h, win_ref[:, OFF_Q:OFF_Q + 4 * D_B])

    r0 = pl.multiple_of(b * seq, seq)
    rsl = pl.ds(r0, seq)
    cos = cos_ref[...]
    sin = sin_ref[...]
    for hb in range(B_HEADS):
        c0 = hb * B_DK
        q = _rotary(qkvz_ref[rsl, c0:c0 + B_DK], cos, sin)
        k = _rotary(qkvz_ref[rsl, D_B + c0:D_B + c0 + B_DK], cos, sin) * (B_DK ** -0.5)
        vb = qkvz_ref[rsl, 2 * D_B + c0:2 * D_B + c0 + B_DV]
        zb = qkvz_ref[rsl, 3 * D_B + c0:3 * D_B + c0 + B_DV]
        o, s_new = _retention_head(q, k, vb, s_in_ref[0, hb], dec_ref[hb], rdec_ref[hb], cdec_ref[hb],
                                   sdec_ref[hb])
        s_out_ref[0, hb] = s_new
        mix_ref[rsl, D_A + c0:D_A + c0 + B_DV] = _bf(_rms(o) * _silu(zb))

    @pl.when(b == nb - 1)
    def _():
        y = x_ref[...] + _dot(mix_ref[...], wout_ref[...])
        if final_norm:
            y = _rms(y) * gfin_ref[...]
        y_ref[...] = y


def _const_spec(shape):
    zeros = (0,) * len(shape)
    return pl.BlockSpec(shape, lambda i: zeros, pipeline_mode=pl.Buffered(1))


def _layer_spec(shape, layer):
    zeros = (0,) * len(shape)
    return pl.BlockSpec((None,) + shape, lambda i: (layer,) + zeros, pipeline_mode=pl.Buffered(1))


def _decay_tables(length):
    lg = jnp.log(1.0 - 2.0 ** (-5.0 - jnp.arange(B_HEADS, dtype=jnp.float32)))
    idx = jnp.arange(length, dtype=jnp.float32)
    diff = idx[:, None] - idx[None, :]
    causal = diff >= 0
    dec = jnp.where(causal[None], jnp.exp(lg[:, None, None] * jnp.where(causal, diff, 0.0)[None]), 0.0)
    rdec = jnp.exp(lg[:, None] * (idx[None, :] + 1.0))
    cdec = jnp.exp(lg[:, None] * (length - 1.0 - idx[None, :]))
    sdec = jnp.exp(lg * length)
    rdec = jnp.broadcast_to(rdec[:, :, None], (B_HEADS, length, 128))
    cdec = jnp.broadcast_to(cdec[:, :, None], (B_HEADS, length, 128))
    sdec = jnp.broadcast_to(sdec[:, None, None], (B_HEADS, 1, B_DV))
    return dec, rdec, cdec, sdec


def _rope_tables(start, length):
    inv = ROPE_BASE ** (-jnp.arange(HALF, dtype=jnp.float32) / HALF)
    if length <= ROPE_SPLIT:
        ang = (start + jnp.arange(length)).astype(jnp.float32)[:, None] * inv[None, :]
        return jnp.cos(ang), jnp.sin(ang)
    assert length % ROPE_SPLIT == 0
    hi = (start + ROPE_SPLIT * jnp.arange(length // ROPE_SPLIT)).astype(jnp.float32)[:, None] * inv[None, :]
    lo = jnp.arange(ROPE_SPLIT).astype(jnp.float32)[:, None] * inv[None, :]
    ch, sh = jnp.cos(hi)[:, None, :], jnp.sin(hi)[:, None, :]
    cl, sl = jnp.cos(lo)[None, :, :], jnp.sin(lo)[None, :, :]
    cos = (ch * cl - sh * sl).reshape(length, HALF)
    sin = (sh * cl + ch * sl).reshape(length, HALF)
    return cos, sin


def _gate_weights(w_s, b_s, c):
    i = jnp.arange(c)
    mask = (i[None, :] // CHUNK) <= (i[:, None] // CHUNK)
    w = _bf(jnp.where(mask[None], w_s[:, :c, :c], 0.0))
    bias = jnp.broadcast_to(b_s[:, :c].T[:, :, None], (c, A_HEADS, A_DH)).reshape(c, D_A)
    return w, bias


def _prompt_layer(layer, x, cos, sin, gn, win, gsgu, ws, bs, decs, wout, gfin, final_norm):
    seq = x.shape[0]
    tile = PROMPT_TILE
    dec, rdec, cdec, sdec = decs
    row = lambda i: (i, 0)
    kern = functools.partial(_prompt_layer_kernel, final_norm=final_norm)
    return pl.pallas_call(
        kern,
        grid=(seq // tile,),
        in_specs=[
            pl.BlockSpec((tile, D_MODEL), row),
            pl.BlockSpec((tile, HALF), row),
            pl.BlockSpec((tile, HALF), row),
            _layer_spec((1, D_MODEL), layer),
            _layer_spec((D_MODEL, D_PROJ), layer),
            _layer_spec((1, D_A), layer),
            _const_spec((A_HEADS, A_CHUNK, A_CHUNK)),
            _const_spec((A_CHUNK, D_A)),
            _const_spec((B_HEADS, RET_BLOCK, RET_BLOCK)),
            _const_spec((B_HEADS, RET_BLOCK, 128)),
            _const_spec((B_HEADS, RET_BLOCK, 128)),
            _const_spec((B_HEADS, 1, B_DV)),
            _layer_spec((D_MIX, D_MODEL), layer),
            _const_spec((1, D_MODEL)),
        ],
        out_specs=[
            pl.BlockSpec((tile, D_MODEL), row),
            pl.BlockSpec((B_HEADS, B_DK, B_DV), lambda i: (0, 0, 0)),
        ],
        out_shape=[
            jax.ShapeDtypeStruct((seq, D_MODEL), jnp.float32),
            jax.ShapeDtypeStruct((B_HEADS, B_DK, B_DV), jnp.float32),
        ],
        scratch_shapes=[pltpu.VMEM((tile, D_MIX), jnp.bfloat16)],
        compiler_params=pltpu.CompilerParams(
            dimension_semantics=("arbitrary",), vmem_limit_bytes=VMEM_LIMIT_BYTES),
        name="prompt_layer",
    )(x, cos, sin, gn, win, gsgu, ws, bs, dec, rdec, cdec, sdec, wout, gfin)


def _sample_layer(layer, x, state, cos, sin, gn, win, gsgu, ws, bs, decs, wout, gfin, final_norm, seq):
    rows = x.shape[0]
    nb = rows // seq
    dec, rdec, cdec, sdec = decs
    kern = functools.partial(_sample_layer_kernel, final_norm=final_norm, seq=seq)
    state_in_spec = pl.BlockSpec((None, 1, B_HEADS, B_DK, B_DV), lambda i: (layer, i, 0, 0, 0))
    state_out_spec = pl.BlockSpec((1, B_HEADS, B_DK, B_DV), lambda i: (i, 0, 0, 0))
    return pl.pallas_call(
        kern,
        grid=(nb,),
        in_specs=[
            _const_spec((rows, D_MODEL)),
            state_in_spec,
            _const_spec((seq, HALF)),
            _const_spec((seq, HALF)),
            _layer_spec((1, D_MODEL), layer),
            _layer_spec((D_MODEL, D_PROJ), layer),
            _layer_spec((1, D_A), layer),
            _const_spec((A_HEADS, seq, seq)),
            _const_spec((seq, D_A)),
            _const_spec((B_HEADS, seq, seq)),
            _const_spec((B_HEADS, seq, 128)),
            _const_spec((B_HEADS, seq, 128)),
            _const_spec((B_HEADS, 1, B_DV)),
            _layer_spec((D_MIX, D_MODEL), layer),
            _const_spec((1, D_MODEL)),
        ],
        out_specs=[
            pl.BlockSpec((rows, D_MODEL), lambda i: (0, 0)),
            state_out_spec,
            pl.BlockSpec((rows, D_A), lambda i: (0, 0)),
        ],
        out_shape=[
            jax.ShapeDtypeStruct((rows, D_MODEL), jnp.float32),
            jax.ShapeDtypeStruct(state.shape[1:], jnp.float32),
            jax.ShapeDtypeStruct((rows, D_A), jnp.float32),
        ],
        scratch_shapes=[
            pltpu.VMEM((rows, 4 * D_B), jnp.float32),
            pltpu.VMEM((rows, D_MIX), jnp.bfloat16),
        ],
        compiler_params=pltpu.CompilerParams(
            dimension_semantics=("arbitrary",), vmem_limit_bytes=VMEM_LIMIT_BYTES),
        name="sample_layer",
    )(x, state, cos, sin, gn, win, gsgu, ws, bs, dec, rdec, cdec, sdec, wout, gfin)


def kernel(x_prompt, x_sample, state_ret, g_norm, w_in, g_sgu, w_s, b_s, w_out, g_final):
    batch, seq, _ = x_prompt.shape
    dec_batch, dec_seq, _ = x_sample.shape
    assert batch == 1 and seq % PROMPT_TILE == 0
    assert PROMPT_TILE % RET_BLOCK == 0 and RET_BLOCK % A_CHUNK == 0
    assert dec_seq <= CHUNK

    cos_p, sin_p = _rope_tables(0, seq)
    cos_s, sin_s = _rope_tables(PAST_LEN, dec_seq)
    decs_p = _decay_tables(RET_BLOCK)
    decs_s = _decay_tables(dec_seq)
    gfin = g_final.reshape(1, D_MODEL)
    gn = g_norm.reshape(DEPTH, 1, D_MODEL)
    gsgu = g_sgu.reshape(DEPTH, 1, D_A)
    win = _bf(w_in)
    wout = _bf(w_out)

    hp = x_prompt.reshape(seq, D_MODEL)
    hs = x_sample.reshape(dec_batch * dec_seq, D_MODEL)
    ret_p, ret_s, chunk_v = [], [], []
    for l in range(DEPTH):
        last = l == DEPTH - 1
        ws_p, bs_p = _gate_weights(w_s[l], b_s[l], A_CHUNK)
        ws_s, bs_s = _gate_weights(w_s[l], b_s[l], dec_seq)
        hp, sp = _prompt_layer(l, hp, cos_p, sin_p, gn, win, gsgu, ws_p, bs_p, decs_p, wout, gfin, last)
        hs, ss, vn = _sample_layer(l, hs, state_ret, cos_s, sin_s, gn, win, gsgu, ws_s, bs_s, decs_s,
                                   wout, gfin, last, dec_seq)
        ret_p.append(sp.reshape(batch, B_HEADS, B_DK, B_DV))
        ret_s.append(ss)
        chunk_v.append(vn.reshape(dec_batch, dec_seq, A_HEADS, A_DH))
    y_prompt = hp.reshape(batch, seq, D_MODEL)
    y_sample = hs.reshape(dec_batch, dec_seq, D_MODEL)
    return (y_prompt, y_sample, jnp.stack(ret_p, axis=0), jnp.stack(ret_s, axis=0),
            jnp.stack(chunk_v, axis=0))
```

```python
import functools

import jax
import jax.numpy as jnp
from jax import lax
from jax.experimental import pallas as pl
from jax.experimental.pallas import tpu as pltpu

D_MODEL = 1024
DEPTH = 4
PAST_LEN = 4096
CHUNK = 64
D_MIX = 2 * D_MODEL
D_A = D_MIX // 2
A_HEADS = 8
A_DH = D_A // A_HEADS
A_CHUNK = 128
D_B = D_MIX - D_A
B_HEADS = 4
B_DK = D_B // B_HEADS
B_DV = D_B // B_HEADS
D_PROJ = 3 * D_A + 4 * D_B
ROPE_BASE = 10000.0
EPS = 1e-6
HALF = B_DK // 2
ROPE_SPLIT = 128

OFF_U, OFF_V, OFF_ZA = 0, D_A, 2 * D_A
OFF_Q, OFF_K, OFF_VB, OFF_ZB = 3 * D_A, 3 * D_A + D_B, 3 * D_A + 2 * D_B, 3 * D_A + 3 * D_B

RET_BLOCK = 256
PROMPT_TILE = 512
W_CHUNK = 256

SAMPLE_GROUP = 2
SAMPLE_GATE_GROUP = 4
W_TILE = 512
N_IN_TILES = D_PROJ // W_TILE
N_OUT_TILES = D_MIX // W_TILE

VMEM_LIMIT_BYTES = 56 * 1024 * 1024


def _bf(x):
    return x.astype(jnp.bfloat16)


def _dot(a, b):
    return jnp.dot(a, b, preferred_element_type=jnp.float32)


def _dot_nt(a, b):
    return lax.dot_general(a, b, (((1,), (1,)), ((), ())), preferred_element_type=jnp.float32)


def _dot_tn(a, b):
    return lax.dot_general(a, b, (((0,), (0,)), ((), ())), preferred_element_type=jnp.float32)


def _rms(x):
    return x * lax.rsqrt(jnp.mean(x * x, axis=-1, keepdims=True) + EPS)


def _silu(z):
    return z * (1.0 / (1.0 + jnp.exp(-z)))


def _rotary(x, cos, sin):
    x1, x2 = x[:, :HALF], x[:, HALF:]
    return jnp.concatenate([x1 * cos - x2 * sin, x1 * sin + x2 * cos], axis=-1)


def _wide(t):
    return jnp.concatenate([t, t], axis=-1)


def _head_norm(v, gsgu):
    parts = [_rms(v[:, h * A_DH:(h + 1) * A_DH]) for h in range(v.shape[1] // A_DH)]
    return jnp.concatenate(parts, axis=-1) * gsgu


def _spatial_gate(vn, ws_ref, head0, bias):
    c = ws_ref.shape[-1]
    rows = []
    for r in range(vn.shape[0] // c):
        cols = [_dot(ws_ref[head0 + h], vn[r * c:(r + 1) * c, h * A_DH:(h + 1) * A_DH])
                for h in range(vn.shape[1] // A_DH)]
        rows.append(jnp.concatenate(cols, axis=-1) + bias)
    return jnp.concatenate(rows, axis=0)


def _retention(q, k, v, states, dec, rdec, cdec, sdec):
    n = len(states)
    seq = q.shape[0] // n
    qb, kb, vb = _bf(q), _bf(k), _bf(v)
    scores = _dot_nt(qb, kb) * dec
    intra = _dot(_bf(scores), vb)
    cross = [_dot(qb[i * seq:(i + 1) * seq], _bf(states[i])) for i in range(n)]
    cross = (cross[0] if n == 1 else jnp.concatenate(cross, axis=0)) * _wide(rdec)
    kw = _bf(k * _wide(cdec))
    new_states = [sdec * states[i] + _dot_tn(kw[i * seq:(i + 1) * seq], vb[i * seq:(i + 1) * seq])
                  for i in range(n)]
    return intra + cross, new_states


def _load_layer_weights(layer, win_hbm, wout_hbm, win_bf, wout_bf, stage_in, stage_out, sems):
    jobs = []
    for c in range(D_PROJ // W_CHUNK):
        cols = slice(c * W_CHUNK, (c + 1) * W_CHUNK)
        copy = pltpu.make_async_copy(win_hbm.at[layer, :, cols], stage_in.at[c % 2], sems.at[0, c % 2])
        jobs.append((copy, stage_in.at[c % 2], win_bf.at[:, cols]))
    for c in range(D_MIX // W_CHUNK):
        rows = slice(c * W_CHUNK, (c + 1) * W_CHUNK)
        copy = pltpu.make_async_copy(wout_hbm.at[layer, rows, :], stage_out.at[c % 2], sems.at[1, c % 2])
        jobs.append((copy, stage_out.at[c % 2], wout_bf.at[rows, :]))
    jobs[0][0].start()
    jobs[1][0].start()
    for i, (copy, stage, dst) in enumerate(jobs):
        copy.wait()
        dst[...] = _bf(stage[...])
        if i + 2 < len(jobs):
            jobs[i + 2][0].start()


def _prompt_layer_kernel(x_ref, cos_ref, sin_ref, gn_ref, win_hbm, gsgu_ref, ws_ref, bs_ref,
                         dec_ref, rdec_ref, cdec_ref, sdec_ref, wout_hbm, gfin_ref,
                         y_ref, s_ref,
                         win_ref, wout_ref, stage_in, stage_out, sems, mix_ref, *, layer, final_norm):
    tile = x_ref.shape[0]

    @pl.when(pl.program_id(0) == 0)
    def _():
        s_ref[...] = jnp.zeros_like(s_ref)
        _load_layer_weights(layer, win_hbm, wout_hbm, win_ref, wout_ref, stage_in, stage_out, sems)

    for blk in range(tile // RET_BLOCK):
        r = slice(blk * RET_BLOCK, (blk + 1) * RET_BLOCK)
        x = x_ref[r, :]
        h = _bf(_rms(x) * gn_ref[...])

        u = _dot(h, win_ref[:, OFF_U:OFF_U + D_A])
        v = _dot(h, win_ref[:, OFF_V:OFF_V + D_A])
        za = _dot(h, win_ref[:, OFF_ZA:OFF_ZA + D_A])
        vn = _bf(_head_norm(v, gsgu_ref[...]))
        mixed = _spatial_gate(vn, ws_ref, 0, bs_ref[...])
        mix_ref[r, 0:D_A] = _bf(u * mixed * _silu(za))

        cos = cos_ref[r, :]
        sin = sin_ref[r, :]
        for hb in range(B_HEADS):
            c0 = hb * B_DK
            q = _rotary(_dot(h, win_ref[:, OFF_Q + c0:OFF_Q + c0 + B_DK]), cos, sin)
            k = _rotary(_dot(h, win_ref[:, OFF_K + c0:OFF_K + c0 + B_DK]), cos, sin) * (B_DK ** -0.5)
            vb = _dot(h, win_ref[:, OFF_VB + c0:OFF_VB + c0 + B_DV])
            zb = _dot(h, win_ref[:, OFF_ZB + c0:OFF_ZB + c0 + B_DV])
            o, (s_new,) = _retention(q, k, vb, [s_ref[hb]], dec_ref[hb], rdec_ref[hb], cdec_ref[hb],
                                     sdec_ref[hb])
            s_ref[hb] = s_new
            mix_ref[r, D_A + c0:D_A + c0 + B_DV] = _bf(_rms(o) * _silu(zb))

        y = x + _dot(mix_ref[r, :], wout_ref[...])
        if final_norm:
            y = _rms(y) * gfin_ref[...]
        y_ref[r, :] = y


def _sample_kernel(x_ref, s_in_ref, cos_ref, sin_ref, gn_ref, win_ref, gsgu_ref, ws_ref, bs_ref,
                   dec_ref, rdec_ref, cdec_ref, sdec_ref, wout_ref, gfin_ref,
                   y_ref, s_out_ref, vn_ref,
                   h_ref, p_ref, mix_ref, *, seq):
    layer = pl.program_id(0)
    step = pl.program_id(1)
    group = s_in_ref.shape[0]
    n_groups = x_ref.shape[0] // (group * seq)
    tiles_per_block = D_A // W_TILE

    @pl.when((layer == 0) & (step == 0))
    def _():
        y_ref[...] = x_ref[...]

    @pl.when(step == 0)
    def _():
        h_ref[...] = _bf(_rms(y_ref[...]) * gn_ref[...])

    @pl.when(step < N_IN_TILES)
    def _():
        p_ref[step] = _dot(h_ref[...], _bf(win_ref[...]))

    for t in range(tiles_per_block):
        @pl.when(step == 2 * tiles_per_block + t)
        def _(t=t):
            cols = slice(t * W_TILE, (t + 1) * W_TILE)
            u = p_ref[t]
            v = p_ref[tiles_per_block + t]
            za = p_ref[2 * tiles_per_block + t]
            vn = _head_norm(v, gsgu_ref[:, cols])
            vn_ref[...] = vn
            mixed = _spatial_gate(_bf(vn), ws_ref, t * (W_TILE // A_DH), bs_ref[:, cols])
            mix_ref[:, cols] = _bf(u * mixed * _silu(za))

    @pl.when((step >= N_IN_TILES) & (step < N_IN_TILES + n_groups))
    def _():
        r0 = pl.multiple_of((step - N_IN_TILES) * (group * seq), group * seq)
        rows = pl.ds(r0, group * seq)
        cos = cos_ref[...]
        sin = sin_ref[...]
        heads_per_tile = W_TILE // B_DK
        for hb in range(B_HEADS):
            t, c0 = hb // heads_per_tile, (hb % heads_per_tile) * B_DK
            blocks = [p_ref[(3 + i) * tiles_per_block + t, rows, c0:c0 + B_DK] for i in range(4)]
            q = _rotary(blocks[0], cos, sin)
            k = _rotary(blocks[1], cos, sin) * (B_DK ** -0.5)
            states = [s_in_ref[i, hb] for i in range(group)]
            o, new_states = _retention(q, k, blocks[2], states, dec_ref[hb], rdec_ref[hb], cdec_ref[hb],
                                       sdec_ref[hb])
            for i in range(group):
                s_out_ref[i, hb] = new_states[i]
            mix_ref[rows, D_A + hb * B_DV:D_A + (hb + 1) * B_DV] = _bf(_rms(o) * _silu(blocks[3]))

    @pl.when(step >= N_IN_TILES + n_groups)
    def _():
        for t in range(N_OUT_TILES):
            @pl.when(step == N_IN_TILES + n_groups + t)
            def _(t=t):
                y_ref[...] += _dot(mix_ref[:, t * W_TILE:(t + 1) * W_TILE], _bf(wout_ref[...]))

    @pl.when((layer == pl.num_programs(0) - 1) & (step == pl.num_programs(1) - 1))
    def _():
        y_ref[...] = _rms(y_ref[...]) * gfin_ref[...]


def _const_spec(shape, grid_rank=1):
    zeros = (0,) * len(shape)
    return pl.BlockSpec(shape, lambda *_: zeros, pipeline_mode=pl.Buffered(1))


def _decay_tables(length, n_seq=1):
    lg = jnp.log(1.0 - 2.0 ** (-5.0 - jnp.arange(B_HEADS, dtype=jnp.float32)))
    idx = jnp.arange(length, dtype=jnp.float32)
    diff = idx[:, None] - idx[None, :]
    causal = diff >= 0
    dec = jnp.where(causal[None], jnp.exp(lg[:, None, None] * jnp.where(causal, diff, 0.0)[None]), 0.0)
    rdec = jnp.exp(lg[:, None] * (idx[None, :] + 1.0))
    cdec = jnp.exp(lg[:, None] * (length - 1.0 - idx[None, :]))
    sdec = jnp.exp(lg * length)
    dec = jnp.einsum("ab,hij->haibj", jnp.eye(n_seq, dtype=dec.dtype), dec)
    dec = dec.reshape(B_HEADS, n_seq * length, n_seq * length)
    rdec = jnp.broadcast_to(jnp.tile(rdec, (1, n_seq))[:, :, None], (B_HEADS, n_seq * length, 128))
    cdec = jnp.broadcast_to(jnp.tile(cdec, (1, n_seq))[:, :, None], (B_HEADS, n_seq * length, 128))
    sdec = jnp.broadcast_to(sdec[:, None, None], (B_HEADS, 1, B_DV))
    return dec, rdec, cdec, sdec


def _rope_tables(start, length):
    inv = ROPE_BASE ** (-jnp.arange(HALF, dtype=jnp.float32) / HALF)
    if length <= ROPE_SPLIT:
        ang = (start + jnp.arange(length)).astype(jnp.float32)[:, None] * inv[None, :]
        return jnp.cos(ang), jnp.sin(ang)
    assert length % ROPE_SPLIT == 0
    hi = (start + ROPE_SPLIT * jnp.arange(length // ROPE_SPLIT)).astype(jnp.float32)[:, None] * inv[None, :]
    lo = jnp.arange(ROPE_SPLIT).astype(jnp.float32)[:, None] * inv[None, :]
    ch, sh = jnp.cos(hi)[:, None, :], jnp.sin(hi)[:, None, :]
    cl, sl = jnp.cos(lo)[None, :, :], jnp.sin(lo)[None, :, :]
    cos = (ch * cl - sh * sl).reshape(length, HALF)
    sin = (sh * cl + ch * sl).reshape(length, HALF)
    return cos, sin


def _gate_weights(w_s, b_s, c, n_seq=1):
    i = jnp.arange(c)
    mask = (i[None, :] // CHUNK) <= (i[:, None] // CHUNK)
    w = jnp.where(mask, w_s[..., :c, :c], 0.0)
    w = jnp.einsum("ab,...ij->...aibj", jnp.eye(n_seq, dtype=w.dtype), w)
    w = _bf(w.reshape(w.shape[:-4] + (n_seq * c, n_seq * c)))
    bias = jnp.swapaxes(b_s[..., :c], -1, -2)
    bias = jnp.broadcast_to(bias[..., None], bias.shape + (A_DH,)).reshape(bias.shape[:-1] + (D_A,))
    bias = jnp.tile(bias, (1,) * (bias.ndim - 2) + (n_seq, 1))
    return w, bias


def _prompt_layer(layer, x, cos, sin, gn, w_in, gsgu, ws, bs, decs, w_out, gfin, final_norm):
    seq = x.shape[0]
    tile = PROMPT_TILE
    dec, rdec, cdec, sdec = decs
    row = lambda i: (i, 0)
    kern = functools.partial(_prompt_layer_kernel, layer=layer, final_norm=final_norm)
    return pl.pallas_call(
        kern,
        grid=(seq // tile,),
        in_specs=[
            pl.BlockSpec((tile, D_MODEL), row),
            pl.BlockSpec((tile, HALF), row),
            pl.BlockSpec((tile, HALF), row),
            _const_spec((1, D_MODEL)),
            pl.BlockSpec(memory_space=pl.ANY),
            _const_spec((1, D_A)),
            _const_spec((A_HEADS, A_CHUNK, A_CHUNK)),
            _const_spec((A_CHUNK, D_A)),
            _const_spec((B_HEADS, RET_BLOCK, RET_BLOCK)),
            _const_spec((B_HEADS, RET_BLOCK, 128)),
            _const_spec((B_HEADS, RET_BLOCK, 128)),
            _const_spec((B_HEADS, 1, B_DV)),
            pl.BlockSpec(memory_space=pl.ANY),
            _const_spec((1, D_MODEL)),
        ],
        out_specs=[
            pl.BlockSpec((tile, D_MODEL), row),
            pl.BlockSpec((B_HEADS, B_DK, B_DV), lambda i: (0, 0, 0)),
        ],
        out_shape=[
            jax.ShapeDtypeStruct((seq, D_MODEL), jnp.float32),
            jax.ShapeDtypeStruct((B_HEADS, B_DK, B_DV), jnp.float32),
        ],
        scratch_shapes=[
            pltpu.VMEM((D_MODEL, D_PROJ), jnp.bfloat16),
            pltpu.VMEM((D_MIX, D_MODEL), jnp.bfloat16),
            pltpu.VMEM((2, D_MODEL, W_CHUNK), jnp.float32),
            pltpu.VMEM((2, W_CHUNK, D_MODEL), jnp.float32),
            pltpu.SemaphoreType.DMA((2, 2)),
            pltpu.VMEM((tile, D_MIX), jnp.bfloat16),
        ],
        compiler_params=pltpu.CompilerParams(
            dimension_semantics=("arbitrary",), vmem_limit_bytes=VMEM_LIMIT_BYTES),
        name="prompt_layer",
    )(x, cos, sin, gn, w_in, gsgu, ws, bs, dec, rdec, cdec, sdec, w_out, gfin)


def _sample_trunk(x, state, cos, sin, gn, w_in, gsgu, ws, bs, decs, w_out, gfin, seq):
    rows = x.shape[0]
    nb = rows // seq
    group = SAMPLE_GROUP
    n_groups = nb // group
    n_steps = N_IN_TILES + n_groups + N_OUT_TILES
    dec, rdec, cdec, sdec = decs
    gate_rows = ws.shape[-1]
    tiles_per_block = D_A // W_TILE

    def per_layer(shape):
        zeros = (0,) * len(shape)
        return pl.BlockSpec((None,) + shape, lambda l, s: (l,) + zeros)

    def state_map(l, s):
        return (l, jnp.clip(s - N_IN_TILES, 0, n_groups - 1), 0, 0, 0)

    state_spec = pl.BlockSpec((None, group, B_HEADS, B_DK, B_DV), state_map)
    kern = functools.partial(_sample_kernel, seq=seq)
    return pl.pallas_call(
        kern,
        grid=(DEPTH, n_steps),
        in_specs=[
            _const_spec((rows, D_MODEL)),
            state_spec,
            _const_spec((group * seq, HALF)),
            _const_spec((group * seq, HALF)),
            per_layer((1, D_MODEL)),
            pl.BlockSpec((None, D_MODEL, W_TILE), lambda l, s: (l, 0, jnp.minimum(s, N_IN_TILES - 1))),
            per_layer((1, D_A)),
            per_layer((A_HEADS, gate_rows, gate_rows)),
            per_layer((gate_rows, D_A)),
            _const_spec((B_HEADS, group * seq, group * seq)),
            _const_spec((B_HEADS, group * seq, 128)),
            _const_spec((B_HEADS, group * seq, 128)),
            _const_spec((B_HEADS, 1, B_DV)),
            pl.BlockSpec((None, W_TILE, D_MODEL),
                         lambda l, s: (l, jnp.clip(s - N_IN_TILES - n_groups, 0, N_OUT_TILES - 1), 0)),
            _const_spec((1, D_MODEL)),
        ],
        out_specs=[
            pl.BlockSpec((rows, D_MODEL), lambda l, s: (0, 0)),
            state_spec,
            pl.BlockSpec((None, rows, W_TILE),
                         lambda l, s: (l, 0, jnp.clip(s - 2 * tiles_per_block, 0, tiles_per_block - 1))),
        ],
        out_shape=[
            jax.ShapeDtypeStruct((rows, D_MODEL), jnp.float32),
            jax.ShapeDtypeStruct(state.shape, jnp.float32),
            jax.ShapeDtypeStruct((DEPTH, rows, D_A), jnp.float32),
        ],
        scratch_shapes=[
            pltpu.VMEM((rows, D_MODEL), jnp.bfloat16),
            pltpu.VMEM((N_IN_TILES, rows, W_TILE), jnp.float32),
            pltpu.VMEM((rows, D_MIX), jnp.bfloat16),
        ],
        compiler_params=pltpu.CompilerParams(
            dimension_semantics=("arbitrary", "arbitrary"), vmem_limit_bytes=VMEM_LIMIT_BYTES),
        name="sample_trunk",
    )(x, state, cos, sin, gn, w_in, gsgu, ws, bs, dec, rdec, cdec, sdec, w_out, gfin)


def kernel(x_prompt, x_sample, state_ret, g_norm, w_in, g_sgu, w_s, b_s, w_out, g_final):
    batch, seq, _ = x_prompt.shape
    dec_batch, dec_seq, _ = x_sample.shape
    assert batch == 1 and seq % PROMPT_TILE == 0
    assert PROMPT_TILE % RET_BLOCK == 0 and RET_BLOCK % A_CHUNK == 0
    assert dec_seq <= CHUNK and dec_batch % SAMPLE_GROUP == 0
    assert (dec_batch * dec_seq) % (SAMPLE_GATE_GROUP * dec_seq) == 0
    assert D_A % W_TILE == 0 and W_TILE % B_DK == 0 and W_TILE % A_DH == 0

    gfin = g_final.reshape(1, D_MODEL)
    gn = g_norm.reshape(DEPTH, 1, D_MODEL)
    gsgu = g_sgu.reshape(DEPTH, 1, D_A)

    cos_s, sin_s = _rope_tables(PAST_LEN, dec_seq)
    cos_s, sin_s = jnp.tile(cos_s, (SAMPLE_GROUP, 1)), jnp.tile(sin_s, (SAMPLE_GROUP, 1))
    ws_s, bs_s = _gate_weights(w_s, b_s, dec_seq, SAMPLE_GATE_GROUP)
    hs, new_ret_sample, vn = _sample_trunk(
        x_sample.reshape(dec_batch * dec_seq, D_MODEL), state_ret, cos_s, sin_s, gn, w_in, gsgu, ws_s, bs_s,
        _decay_tables(dec_seq, SAMPLE_GROUP), w_out, gfin, dec_seq)

    cos_p, sin_p = _rope_tables(0, seq)
    decs_p = _decay_tables(RET_BLOCK)
    ws_p, bs_p = _gate_weights(w_s, b_s, A_CHUNK)
    hp = x_prompt.reshape(seq, D_MODEL)
    ret_p = []
    for l in range(DEPTH):
        hp, sp = _prompt_layer(l, hp, cos_p, sin_p, gn[l], w_in, gsgu[l], ws_p[l], bs_p[l], decs_p, w_out,
                               gfin, l == DEPTH - 1)
        ret_p.append(sp.reshape(batch, B_HEADS, B_DK, B_DV))

    y_prompt = hp.reshape(batch, seq, D_MODEL)
    y_sample = hs.reshape(dec_batch, dec_seq, D_MODEL)
    new_chunk_v = vn.reshape(DEPTH, dec_batch, dec_seq, A_HEADS, A_DH)
    return (y_prompt, y_sample, jnp.stack(ret_p, axis=0), new_ret_sample, new_chunk_v)
```

```python
import functools

import jax
import jax.numpy as jnp
from jax import lax
from jax.experimental import pallas as pl
from jax.experimental.pallas import tpu as pltpu

D_MODEL = 1024
DEPTH = 4
PAST_LEN = 4096
CHUNK = 64
D_MIX = 2 * D_MODEL
D_A = D_MIX // 2
A_HEADS = 8
A_DH = D_A // A_HEADS
A_CHUNK = 128
D_B = D_MIX - D_A
B_HEADS = 4
B_DK = D_B // B_HEADS
B_DV = D_B // B_HEADS
D_PROJ = 3 * D_A + 4 * D_B
ROPE_BASE = 10000.0
EPS = 1e-6
HALF = B_DK // 2
ROPE_SPLIT = 128

OFF_U, OFF_V, OFF_ZA = 0, D_A, 2 * D_A
OFF_Q, OFF_K, OFF_VB, OFF_ZB = 3 * D_A, 3 * D_A + D_B, 3 * D_A + 2 * D_B, 3 * D_A + 3 * D_B

RET_BLOCK = 256
PROMPT_TILE = 512
W_IN_CHUNK_ROWS = 32
W_OUT_CHUNK_ROWS = 128
W_STAGE_SLOTS = 6

SAMPLE_GROUP = 2
SAMPLE_GATE_GROUP = 4
W_TILE = 512
N_IN_TILES = D_PROJ // W_TILE
N_OUT_TILES = D_MIX // W_TILE
STREAM_LOOKAHEAD = 3
STREAM_SLOTS = STREAM_LOOKAHEAD + 1

VMEM_LIMIT_BYTES = 56 * 1024 * 1024


def _bf(x):
    return x.astype(jnp.bfloat16)


def _dot(a, b):
    return jnp.dot(a, b, preferred_element_type=jnp.float32)


def _dot_nt(a, b):
    return lax.dot_general(a, b, (((1,), (1,)), ((), ())), preferred_element_type=jnp.float32)


def _dot_tn(a, b):
    return lax.dot_general(a, b, (((0,), (0,)), ((), ())), preferred_element_type=jnp.float32)


def _rms(x):
    return x * lax.rsqrt(jnp.mean(x * x, axis=-1, keepdims=True) + EPS)


def _silu(z):
    return z * (1.0 / (1.0 + jnp.exp(-z)))


def _rotary(x, cos, sin):
    x1, x2 = x[:, :HALF], x[:, HALF:]
    return jnp.concatenate([x1 * cos - x2 * sin, x1 * sin + x2 * cos], axis=-1)


def _wide(t):
    return jnp.concatenate([t, t], axis=-1)


def _head_norm(v, gsgu):
    parts = [_rms(v[:, h * A_DH:(h + 1) * A_DH]) for h in range(v.shape[1] // A_DH)]
    return jnp.concatenate(parts, axis=-1) * gsgu


def _spatial_gate(vn, ws_ref, head0, bias):
    c = ws_ref.shape[-1]
    rows = []
    for r in range(vn.shape[0] // c):
        cols = [_dot(ws_ref[head0 + h], vn[r * c:(r + 1) * c, h * A_DH:(h + 1) * A_DH])
                for h in range(vn.shape[1] // A_DH)]
        rows.append(jnp.concatenate(cols, axis=-1) + bias)
    return jnp.concatenate(rows, axis=0)


def _retention(q, k, v, states, dec, rdec, cdec, sdec):
    n = len(states)
    seq = q.shape[0] // n
    qb, kb, vb = _bf(q), _bf(k), _bf(v)
    scores = _dot_nt(qb, kb) * dec
    intra = _dot(_bf(scores), vb)
    cross = [_dot(qb[i * seq:(i + 1) * seq], _bf(states[i])) for i in range(n)]
    cross = (cross[0] if n == 1 else jnp.concatenate(cross, axis=0)) * _wide(rdec)
    kw = _bf(k * _wide(cdec))
    new_states = [sdec * states[i] + _dot_tn(kw[i * seq:(i + 1) * seq], vb[i * seq:(i + 1) * seq])
                  for i in range(n)]
    return intra + cross, new_states


def _load_layer_weights(layer, win_hbm, wout_hbm, win_bf, wout_bf, stage_in, stage_out, sems):
    def chunk_jobs(src, dst, stage, sem_row, rows_total):
        n_slots, rows = stage.shape[0], stage.shape[1]
        jobs = []
        for c in range(rows_total // rows):
            r = slice(c * rows, (c + 1) * rows)
            slot = c % n_slots
            copy = pltpu.make_async_copy(src.at[layer, r, :], stage.at[slot], sems.at[sem_row, slot])
            jobs.append((copy, stage.at[slot], dst.at[r, :]))
        return jobs, n_slots

    for jobs, n_slots in (chunk_jobs(win_hbm, win_bf, stage_in, 0, D_MODEL),
                          chunk_jobs(wout_hbm, wout_bf, stage_out, 1, D_MIX)):
        for copy, _, _ in jobs[:n_slots]:
            copy.start()
        for i, (copy, stage, dst) in enumerate(jobs):
            copy.wait()
            dst[...] = _bf(stage[...])
            if i + n_slots < len(jobs):
                jobs[i + n_slots][0].start()


def _prompt_layer_kernel(x_ref, cos_ref, sin_ref, gn_ref, win_hbm, gsgu_ref, ws_ref, bs_ref,
                         dec_ref, rdec_ref, cdec_ref, sdec_ref, wout_hbm, gfin_ref,
                         y_ref, s_ref,
                         win_ref, wout_ref, stage_in, stage_out, sems, mix_ref, *, layer, final_norm):
    tile = x_ref.shape[0]

    @pl.when(pl.program_id(0) == 0)
    def _():
        s_ref[...] = jnp.zeros_like(s_ref)
        _load_layer_weights(layer, win_hbm, wout_hbm, win_ref, wout_ref, stage_in, stage_out, sems)

    for blk in range(tile // RET_BLOCK):
        r = slice(blk * RET_BLOCK, (blk + 1) * RET_BLOCK)
        x = x_ref[r, :]
        h = _bf(_rms(x) * gn_ref[...])

        u = _dot(h, win_ref[:, OFF_U:OFF_U + D_A])
        v = _dot(h, win_ref[:, OFF_V:OFF_V + D_A])
        za = _dot(h, win_ref[:, OFF_ZA:OFF_ZA + D_A])
        vn = _bf(_head_norm(v, gsgu_ref[...]))
        mixed = _spatial_gate(vn, ws_ref, 0, bs_ref[...])
        mix_ref[r, 0:D_A] = _bf(u * mixed * _silu(za))

        cos = cos_ref[r, :]
        sin = sin_ref[r, :]
        for hb in range(B_HEADS):
            c0 = hb * B_DK
            q = _rotary(_dot(h, win_ref[:, OFF_Q + c0:OFF_Q + c0 + B_DK]), cos, sin)
            k = _rotary(_dot(h, win_ref[:, OFF_K + c0:OFF_K + c0 + B_DK]), cos, sin) * (B_DK ** -0.5)
            vb = _dot(h, win_ref[:, OFF_VB + c0:OFF_VB + c0 + B_DV])
            zb = _dot(h, win_ref[:, OFF_ZB + c0:OFF_ZB + c0 + B_DV])
            o, (s_new,) = _retention(q, k, vb, [s_ref[hb]], dec_ref[hb], rdec_ref[hb], cdec_ref[hb],
                                     sdec_ref[hb])
            s_ref[hb] = s_new
            mix_ref[r, D_A + c0:D_A + c0 + B_DV] = _bf(_rms(o) * _silu(zb))

        y = x + _dot(mix_ref[r, :], wout_ref[...])
        if final_norm:
            y = _rms(y) * gfin_ref[...]
        y_ref[r, :] = y


class _TileStream:
    def __init__(self, hbm, stage, sems, tiles_per_layer, tile_view):
        self.hbm, self.stage, self.sems = hbm, stage, sems
        self.tiles_per_layer, self.tile_view = tiles_per_layer, tile_view
        self.n_tiles = hbm.shape[0] * tiles_per_layer

    def _copy(self, n):
        slot = n % STREAM_SLOTS
        src = self.tile_view(self.hbm, n // self.tiles_per_layer, n % self.tiles_per_layer)
        return pltpu.make_async_copy(src, self.stage.at[slot], self.sems.at[slot])

    def prime(self):
        for n in range(STREAM_LOOKAHEAD):
            self._copy(n).start()

    def take(self, n):
        @pl.when(n + STREAM_LOOKAHEAD < self.n_tiles)
        def _():
            self._copy(n + STREAM_LOOKAHEAD).start()

        self._copy(n).wait()
        return self.stage.at[n % STREAM_SLOTS]


def _tile_start(t):
    return t * W_TILE if isinstance(t, int) else pl.multiple_of(t * W_TILE, W_TILE)


def _win_tile(hbm, layer, t):
    return hbm.at[layer, :, pl.ds(_tile_start(t), W_TILE)]


def _wout_tile(hbm, layer, t):
    return hbm.at[layer, pl.ds(_tile_start(t), W_TILE), :]


def _sample_kernel(x_ref, s_in_ref, cos_ref, sin_ref, gn_ref, win_hbm, gsgu_ref, ws_ref, bs_ref,
                   dec_ref, rdec_ref, cdec_ref, sdec_ref, wout_hbm, gfin_ref,
                   y_ref, s_out_ref, vn_ref,
                   h_ref, p_ref, mix_ref, win_stage, wout_stage, sems, *, seq):
    layer = pl.program_id(0)
    step = pl.program_id(1)
    group = s_in_ref.shape[0]
    n_groups = x_ref.shape[0] // (group * seq)
    tiles_per_block = D_A // W_TILE
    win_stream = _TileStream(win_hbm, win_stage, sems.at[0], N_IN_TILES, _win_tile)
    wout_stream = _TileStream(wout_hbm, wout_stage, sems.at[1], N_OUT_TILES, _wout_tile)

    @pl.when((layer == 0) & (step == 0))
    def _():
        win_stream.prime()
        wout_stream.prime()
        y_ref[...] = x_ref[...]

    @pl.when(step == 0)
    def _():
        h_ref[...] = _bf(_rms(y_ref[...]) * gn_ref[...])

    @pl.when(step < N_IN_TILES)
    def _():
        w = win_stream.take(layer * N_IN_TILES + step)
        p_ref[step] = _dot(h_ref[...], _bf(w[...]))

    for t in range(tiles_per_block):
        @pl.when(step == 2 * tiles_per_block + t)
        def _(t=t):
            cols = slice(t * W_TILE, (t + 1) * W_TILE)
            u = p_ref[t]
            v = p_ref[tiles_per_block + t]
            za = p_ref[2 * tiles_per_block + t]
            vn = _head_norm(v, gsgu_ref[:, cols])
            vn_ref[...] = vn
            mixed = _spatial_gate(_bf(vn), ws_ref, t * (W_TILE // A_DH), bs_ref[:, cols])
            mix_ref[:, cols] = _bf(u * mixed * _silu(za))

    @pl.when((step >= N_IN_TILES) & (step < N_IN_TILES + n_groups))
    def _():
        r0 = pl.multiple_of((step - N_IN_TILES) * (group * seq), group * seq)
        rows = pl.ds(r0, group * seq)
        cos = cos_ref[...]
        sin = sin_ref[...]
        heads_per_tile = W_TILE // B_DK
        for hb in range(B_HEADS):
            t, c0 = hb // heads_per_tile, (hb % heads_per_tile) * B_DK
            blocks = [p_ref[(3 + i) * tiles_per_block + t, rows, c0:c0 + B_DK] for i in range(4)]
            q = _rotary(blocks[0], cos, sin)
            k = _rotary(blocks[1], cos, sin) * (B_DK ** -0.5)
            states = [s_in_ref[i, hb] for i in range(group)]
            o, new_states = _retention(q, k, blocks[2], states, dec_ref[hb], rdec_ref[hb], cdec_ref[hb],
                                       sdec_ref[hb])
            for i in range(group):
                s_out_ref[i, hb] = new_states[i]
            mix_ref[rows, D_A + hb * B_DV:D_A + (hb + 1) * B_DV] = _bf(_rms(o) * _silu(blocks[3]))

    @pl.when(step >= N_IN_TILES + n_groups)
    def _():
        w = wout_stream.take(layer * N_OUT_TILES + step - (N_IN_TILES + n_groups))
        for t in range(N_OUT_TILES):
            @pl.when(step == N_IN_TILES + n_groups + t)
            def _(t=t):
                y_ref[...] += _dot(mix_ref[:, t * W_TILE:(t + 1) * W_TILE], _bf(w[...]))

    @pl.when((layer == pl.num_programs(0) - 1) & (step == pl.num_programs(1) - 1))
    def _():
        y_ref[...] = _rms(y_ref[...]) * gfin_ref[...]


def _const_spec(shape, grid_rank=1):
    zeros = (0,) * len(shape)
    return pl.BlockSpec(shape, lambda *_: zeros, pipeline_mode=pl.Buffered(1))


def _decay_tables(length, n_seq=1):
    lg = jnp.log(1.0 - 2.0 ** (-5.0 - jnp.arange(B_HEADS, dtype=jnp.float32)))
    idx = jnp.arange(length, dtype=jnp.float32)
    diff = idx[:, None] - idx[None, :]
    causal = diff >= 0
    dec = jnp.where(causal[None], jnp.exp(lg[:, None, None] * jnp.where(causal, diff, 0.0)[None]), 0.0)
    rdec = jnp.exp(lg[:, None] * (idx[None, :] + 1.0))
    cdec = jnp.exp(lg[:, None] * (length - 1.0 - idx[None, :]))
    sdec = jnp.exp(lg * length)
    dec = jnp.einsum("ab,hij->haibj", jnp.eye(n_seq, dtype=dec.dtype), dec)
    dec = dec.reshape(B_HEADS, n_seq * length, n_seq * length)
    rdec = jnp.broadcast_to(jnp.tile(rdec, (1, n_seq))[:, :, None], (B_HEADS, n_seq * length, 128))
    cdec = jnp.broadcast_to(jnp.tile(cdec, (1, n_seq))[:, :, None], (B_HEADS, n_seq * length, 128))
    sdec = jnp.broadcast_to(sdec[:, None, None], (B_HEADS, 1, B_DV))
    return dec, rdec, cdec, sdec


def _rope_tables(start, length):
    inv = ROPE_BASE ** (-jnp.arange(HALF, dtype=jnp.float32) / HALF)
    if length <= ROPE_SPLIT:
        ang = (start + jnp.arange(length)).astype(jnp.float32)[:, None] * inv[None, :]
        return jnp.cos(ang), jnp.sin(ang)
    assert length % ROPE_SPLIT == 0
    hi = (start + ROPE_SPLIT * jnp.arange(length // ROPE_SPLIT)).astype(jnp.float32)[:, None] * inv[None, :]
    lo = jnp.arange(ROPE_SPLIT).astype(jnp.float32)[:, None] * inv[None, :]
    ch, sh = jnp.cos(hi)[:, None, :], jnp.sin(hi)[:, None, :]
    cl, sl = jnp.cos(lo)[None, :, :], jnp.sin(lo)[None, :, :]
    cos = (ch * cl - sh * sl).reshape(length, HALF)
    sin = (sh * cl + ch * sl).reshape(length, HALF)
    return cos, sin


def _gate_weights(w_s, b_s, c, n_seq=1):
    i = jnp.arange(c)
    mask = (i[None, :] // CHUNK) <= (i[:, None] // CHUNK)
    w = jnp.where(mask, w_s[..., :c, :c], 0.0)
    w = jnp.einsum("ab,...ij->...aibj", jnp.eye(n_seq, dtype=w.dtype), w)
    w = _bf(w.reshape(w.shape[:-4] + (n_seq * c, n_seq * c)))
    bias = jnp.swapaxes(b_s[..., :c], -1, -2)
    bias = jnp.broadcast_to(bias[..., None], bias.shape + (A_DH,)).reshape(bias.shape[:-1] + (D_A,))
    bias = jnp.tile(bias, (1,) * (bias.ndim - 2) + (n_seq, 1))
    return w, bias


def _prompt_layer(layer, x, cos, sin, gn, w_in, gsgu, ws, bs, decs, w_out, gfin, final_norm):
    seq = x.shape[0]
    tile = PROMPT_TILE
    dec, rdec, cdec, sdec = decs
    row = lambda i: (i, 0)
    kern = functools.partial(_prompt_layer_kernel, layer=layer, final_norm=final_norm)
    return pl.pallas_call(
        kern,
        grid=(seq // tile,),
        in_specs=[
            pl.BlockSpec((tile, D_MODEL), row),
            pl.BlockSpec((tile, HALF), row),
            pl.BlockSpec((tile, HALF), row),
            _const_spec((1, D_MODEL)),
            pl.BlockSpec(memory_space=pl.ANY),
            _const_spec((1, D_A)),
            _const_spec((A_HEADS, A_CHUNK, A_CHUNK)),
            _const_spec((A_CHUNK, D_A)),
            _const_spec((B_HEADS, RET_BLOCK, RET_BLOCK)),
            _const_spec((B_HEADS, RET_BLOCK, 128)),
            _const_spec((B_HEADS, RET_BLOCK, 128)),
            _const_spec((B_HEADS, 1, B_DV)),
            pl.BlockSpec(memory_space=pl.ANY),
            _const_spec((1, D_MODEL)),
        ],
        out_specs=[
            pl.BlockSpec((tile, D_MODEL), row),
            pl.BlockSpec((B_HEADS, B_DK, B_DV), lambda i: (0, 0, 0)),
        ],
        out_shape=[
            jax.ShapeDtypeStruct((seq, D_MODEL), jnp.float32),
            jax.ShapeDtypeStruct((B_HEADS, B_DK, B_DV), jnp.float32),
        ],
        scratch_shapes=[
            pltpu.VMEM((D_MODEL, D_PROJ), jnp.bfloat16),
            pltpu.VMEM((D_MIX, D_MODEL), jnp.bfloat16),
            pltpu.VMEM((W_STAGE_SLOTS, W_IN_CHUNK_ROWS, D_PROJ), jnp.float32),
            pltpu.VMEM((W_STAGE_SLOTS, W_OUT_CHUNK_ROWS, D_MODEL), jnp.float32),
            pltpu.SemaphoreType.DMA((2, W_STAGE_SLOTS)),
            pltpu.VMEM((tile, D_MIX), jnp.bfloat16),
        ],
        compiler_params=pltpu.CompilerParams(
            dimension_semantics=("arbitrary",), vmem_limit_bytes=VMEM_LIMIT_BYTES),
        name="prompt_layer",
    )(x, cos, sin, gn, w_in, gsgu, ws, bs, dec, rdec, cdec, sdec, w_out, gfin)


def _sample_trunk(x, state, cos, sin, gn, w_in, gsgu, ws, bs, decs, w_out, gfin, seq):
    rows = x.shape[0]
    nb = rows // seq
    group = SAMPLE_GROUP
    n_groups = nb // group
    n_steps = N_IN_TILES + n_groups + N_OUT_TILES
    dec, rdec, cdec, sdec = decs
    gate_rows = ws.shape[-1]
    tiles_per_block = D_A // W_TILE

    def per_layer(shape):
        zeros = (0,) * len(shape)
        return pl.BlockSpec((None,) + shape, lambda l, s: (l,) + zeros)

    def state_map(l, s):
        return (l, jnp.clip(s - N_IN_TILES, 0, n_groups - 1), 0, 0, 0)

    state_spec = pl.BlockSpec((None, group, B_HEADS, B_DK, B_DV), state_map)
    kern = functools.partial(_sample_kernel, seq=seq)
    return pl.pallas_call(
        kern,
        grid=(DEPTH, n_steps),
        in_specs=[
            _const_spec((rows, D_MODEL)),
            state_spec,
            _const_spec((group * seq, HALF)),
            _const_spec((group * seq, HALF)),
            per_layer((1, D_MODEL)),
            pl.BlockSpec(memory_space=pl.ANY),
            per_layer((1, D_A)),
            per_layer((A_HEADS, gate_rows, gate_rows)),
            per_layer((gate_rows, D_A)),
            _const_spec((B_HEADS, group * seq, group * seq)),
            _const_spec((B_HEADS, group * seq, 128)),
            _const_spec((B_HEADS, group * seq, 128)),
            _const_spec((B_HEADS, 1, B_DV)),
            pl.BlockSpec(memory_space=pl.ANY),
            _const_spec((1, D_MODEL)),
        ],
        out_specs=[
            pl.BlockSpec((rows, D_MODEL), lambda l, s: (0, 0)),
            state_spec,
            pl.BlockSpec((None, rows, W_TILE),
                         lambda l, s: (l, 0, jnp.clip(s - 2 * tiles_per_block, 0, tiles_per_block - 1))),
        ],
        out_shape=[
            jax.ShapeDtypeStruct((rows, D_MODEL), jnp.float32),
            jax.ShapeDtypeStruct(state.shape, jnp.float32),
            jax.ShapeDtypeStruct((DEPTH, rows, D_A), jnp.float32),
        ],
        scratch_shapes=[
            pltpu.VMEM((rows, D_MODEL), jnp.bfloat16),
            pltpu.VMEM((N_IN_TILES, rows, W_TILE), jnp.float32),
            pltpu.VMEM((rows, D_MIX), jnp.bfloat16),
            pltpu.VMEM((STREAM_SLOTS, D_MODEL, W_TILE), jnp.float32),
            pltpu.VMEM((STREAM_SLOTS, W_TILE, D_MODEL), jnp.float32),
            pltpu.SemaphoreType.DMA((2, STREAM_SLOTS)),
        ],
        compiler_params=pltpu.CompilerParams(
            dimension_semantics=("arbitrary", "arbitrary"), vmem_limit_bytes=VMEM_LIMIT_BYTES),
        name="sample_trunk",
    )(x, state, cos, sin, gn, w_in, gsgu, ws, bs, dec, rdec, cdec, sdec, w_out, gfin)


def kernel(x_prompt, x_sample, state_ret, g_norm, w_in, g_sgu, w_s, b_s, w_out, g_final):
    batch, seq, _ = x_prompt.shape
    dec_batch, dec_seq, _ = x_sample.shape
    assert batch == 1 and seq % PROMPT_TILE == 0
    assert PROMPT_TILE % RET_BLOCK == 0 and RET_BLOCK % A_CHUNK == 0
    assert dec_seq <= CHUNK and dec_batch % SAMPLE_GROUP == 0
    assert (dec_batch * dec_seq) % (SAMPLE_GATE_GROUP * dec_seq) == 0
    assert D_A % W_TILE == 0 and W_TILE % B_DK == 0 and W_TILE % A_DH == 0

    gfin = g_final.reshape(1, D_MODEL)
    gn = g_norm.reshape(DEPTH, 1, D_MODEL)
    gsgu = g_sgu.reshape(DEPTH, 1, D_A)

    cos_s, sin_s = _rope_tables(PAST_LEN, dec_seq)
    cos_s, sin_s = jnp.tile(cos_s, (SAMPLE_GROUP, 1)), jnp.tile(sin_s, (SAMPLE_GROUP, 1))
    ws_s, bs_s = _gate_weights(w_s, b_s, dec_seq, SAMPLE_GATE_GROUP)
    hs, new_ret_sample, vn = _sample_trunk(
        x_sample.reshape(dec_batch * dec_seq, D_MODEL), state_ret, cos_s, sin_s, gn, w_in, gsgu, ws_s, bs_s,
        _decay_tables(dec_seq, SAMPLE_GROUP), w_out, gfin, dec_seq)

    cos_p, sin_p = _rope_tables(0, seq)
    decs_p = _decay_tables(RET_BLOCK)
    ws_p, bs_p = _gate_weights(w_s, b_s, A_CHUNK)
    hp = x_prompt.reshape(seq, D_MODEL)
    ret_p = []
    for l in range(DEPTH):
        hp, sp = _prompt_layer(l, hp, cos_p, sin_p, gn[l], w_in, gsgu[l], ws_p[l], bs_p[l], decs_p, w_out,
                               gfin, l == DEPTH - 1)
        ret_p.append(sp.reshape(batch, B_HEADS, B_DK, B_DV))

    y_prompt = hp.reshape(batch, seq, D_MODEL)
    y_sample = hs.reshape(dec_batch, dec_seq, D_MODEL)
    new_chunk_v = vn.reshape(DEPTH, dec_batch, dec_seq, A_HEADS, A_DH)
    return (y_prompt, y_sample, jnp.stack(ret_p, axis=0), new_ret_sample, new_chunk_v)
```

```python
import functools

import jax
import jax.numpy as jnp
from jax import lax
from jax.experimental import pallas as pl
from jax.experimental.pallas import tpu as pltpu

D_MODEL = 1024
DEPTH = 4
PAST_LEN = 4096
CHUNK = 64
D_MIX = 2 * D_MODEL
D_A = D_MIX // 2
A_HEADS = 8
A_DH = D_A // A_HEADS
A_CHUNK = 128
D_B = D_MIX - D_A
B_HEADS = 4
B_DK = D_B // B_HEADS
B_DV = D_B // B_HEADS
D_PROJ = 3 * D_A + 4 * D_B
ROPE_BASE = 10000.0
EPS = 1e-6
HALF = B_DK // 2
ROPE_SPLIT = 128

OFF_U, OFF_V, OFF_ZA = 0, D_A, 2 * D_A
OFF_Q, OFF_K, OFF_VB, OFF_ZB = 3 * D_A, 3 * D_A + D_B, 3 * D_A + 2 * D_B, 3 * D_A + 3 * D_B

RET_BLOCK = 256
PROMPT_TILE = 512
W_IN_CHUNK_ROWS = 32
W_OUT_CHUNK_ROWS = 128
W_STAGE_SLOTS = 6

SAMPLE_GROUP = 2
SAMPLE_GATE_GROUP = 4
W_TILE = 512
N_IN_TILES = D_PROJ // W_TILE
N_OUT_TILES = D_MIX // W_TILE
STREAM_LOOKAHEAD = 3
STREAM_SLOTS = STREAM_LOOKAHEAD + 1

VMEM_LIMIT_BYTES = 56 * 1024 * 1024


def _bf(x):
    return x.astype(jnp.bfloat16)


def _dot(a, b):
    return jnp.dot(a, b, preferred_element_type=jnp.float32)


def _dot_nt(a, b):
    return lax.dot_general(a, b, (((1,), (1,)), ((), ())), preferred_element_type=jnp.float32)


def _dot_tn(a, b):
    return lax.dot_general(a, b, (((0,), (0,)), ((), ())), preferred_element_type=jnp.float32)


def _rms(x):
    return x * lax.rsqrt(jnp.mean(x * x, axis=-1, keepdims=True) + EPS)


def _silu(z):
    return z * (1.0 / (1.0 + jnp.exp(-z)))


def _rotary(x, cos, sin):
    x1, x2 = x[:, :HALF], x[:, HALF:]
    return jnp.concatenate([x1 * cos - x2 * sin, x1 * sin + x2 * cos], axis=-1)


def _wide(t):
    return jnp.concatenate([t, t], axis=-1)


def _head_norm(v, gsgu):
    parts = [_rms(v[:, h * A_DH:(h + 1) * A_DH]) for h in range(v.shape[1] // A_DH)]
    return jnp.concatenate(parts, axis=-1) * gsgu


def _spatial_gate(vn, ws_ref, head0, bias):
    c = ws_ref.shape[-1]
    rows = []
    for r in range(vn.shape[0] // c):
        cols = [_dot(ws_ref[head0 + h], vn[r * c:(r + 1) * c, h * A_DH:(h + 1) * A_DH])
                for h in range(vn.shape[1] // A_DH)]
        rows.append(jnp.concatenate(cols, axis=-1) + bias)
    return jnp.concatenate(rows, axis=0)


def _retention(q, k, v, states, dec, rdec, cdec, sdec):
    n = len(states)
    seq = q.shape[0] // n
    qb, kb, vb = _bf(q), _bf(k), _bf(v)
    scores = _dot_nt(qb, kb) * dec
    intra = _dot(_bf(scores), vb)
    cross = [_dot(qb[i * seq:(i + 1) * seq], _bf(states[i])) for i in range(n)]
    cross = (cross[0] if n == 1 else jnp.concatenate(cross, axis=0)) * _wide(rdec)
    kw = _bf(k * _wide(cdec))
    new_states = [sdec * states[i] + _dot_tn(kw[i * seq:(i + 1) * seq], vb[i * seq:(i + 1) * seq])
                  for i in range(n)]
    return intra + cross, new_states


def _load_layer_weights(layer, win_hbm, wout_hbm, win_bf, wout_bf, stage_in, stage_out, sems):
    def chunk_jobs(src, dst, stage, sem_row, rows_total):
        n_slots, rows = stage.shape[0], stage.shape[1]
        jobs = []
        for c in range(rows_total // rows):
            r = slice(c * rows, (c + 1) * rows)
            slot = c % n_slots
            copy = pltpu.make_async_copy(src.at[layer, r, :], stage.at[slot], sems.at[sem_row, slot])
            jobs.append((copy, stage.at[slot], dst.at[r, :]))
        return jobs, n_slots

    for jobs, n_slots in (chunk_jobs(win_hbm, win_bf, stage_in, 0, D_MODEL),
                          chunk_jobs(wout_hbm, wout_bf, stage_out, 1, D_MIX)):
        for copy, _, _ in jobs[:n_slots]:
            copy.start()
        for i, (copy, stage, dst) in enumerate(jobs):
            copy.wait()
            dst[...] = _bf(stage[...])
            if i + n_slots < len(jobs):
                jobs[i + n_slots][0].start()


def _prompt_layer_kernel(x_ref, xnext_ref, cos_ref, sin_ref, gn_ref, win_hbm, gsgu_ref, ws_ref, bs_ref,
                         dec_ref, rdec_ref, cdec_ref, sdec_ref, wout_hbm, gfin_ref,
                         y_ref, s_ref,
                         win_ref, wout_ref, stage_in, stage_out, sems, h_ref, mix_ref, *, layer, final_norm):
    tile = x_ref.shape[0]
    step = pl.program_id(0)

    @pl.when(step == 0)
    def _():
        s_ref[...] = jnp.zeros_like(s_ref)
        h_ref[0] = _bf(_rms(x_ref[...]) * gn_ref[...])
        _load_layer_weights(layer, win_hbm, wout_hbm, win_ref, wout_ref, stage_in, stage_out, sems)

    slot = step % 2
    h_ref[1 - slot] = _bf(_rms(xnext_ref[...]) * gn_ref[...])

    for blk in range(tile // RET_BLOCK):
        r = slice(blk * RET_BLOCK, (blk + 1) * RET_BLOCK)
        h = h_ref[slot, r, :]

        u = _dot(h, win_ref[:, OFF_U:OFF_U + D_A])
        v = _dot(h, win_ref[:, OFF_V:OFF_V + D_A])
        za = _dot(h, win_ref[:, OFF_ZA:OFF_ZA + D_A])
        vn = _bf(_head_norm(v, gsgu_ref[...]))
        mixed = _spatial_gate(vn, ws_ref, 0, bs_ref[...])
        mix_ref[r, 0:D_A] = _bf(u * mixed * _silu(za))

        cos = cos_ref[r, :]
        sin = sin_ref[r, :]
        for hb in range(B_HEADS):
            c0 = hb * B_DK
            q = _rotary(_dot(h, win_ref[:, OFF_Q + c0:OFF_Q + c0 + B_DK]), cos, sin)
            k = _rotary(_dot(h, win_ref[:, OFF_K + c0:OFF_K + c0 + B_DK]), cos, sin) * (B_DK ** -0.5)
            vb = _dot(h, win_ref[:, OFF_VB + c0:OFF_VB + c0 + B_DV])
            zb = _dot(h, win_ref[:, OFF_ZB + c0:OFF_ZB + c0 + B_DV])
            o, (s_new,) = _retention(q, k, vb, [s_ref[hb]], dec_ref[hb], rdec_ref[hb], cdec_ref[hb],
                                     sdec_ref[hb])
            s_ref[hb] = s_new
            mix_ref[r, D_A + c0:D_A + c0 + B_DV] = _bf(_rms(o) * _silu(zb))

        y = x_ref[r, :] + _dot(mix_ref[r, :], wout_ref[...])
        if final_norm:
            y = _rms(y) * gfin_ref[...]
        y_ref[r, :] = y


class _TileStream:
    def __init__(self, hbm, stage, sems, sem_row, tiles_per_layer, tile_view):
        self.hbm, self.stage, self.sems, self.sem_row = hbm, stage, sems, sem_row
        self.tiles_per_layer, self.tile_view = tiles_per_layer, tile_view
        self.n_tiles = hbm.shape[0] * tiles_per_layer

    def _copy(self, n):
        slot = n % STREAM_SLOTS
        src = self.tile_view(self.hbm, n // self.tiles_per_layer, n % self.tiles_per_layer)
        return pltpu.make_async_copy(src, self.stage.at[slot], self.sems.at[self.sem_row, slot])

    def prime(self):
        for n in range(STREAM_LOOKAHEAD):
            self._copy(n).start()

    def take(self, n):
        @pl.when(n + STREAM_LOOKAHEAD < self.n_tiles)
        def _():
            self._copy(n + STREAM_LOOKAHEAD).start()

        self._copy(n).wait()
        return self.stage.at[n % STREAM_SLOTS]


def _tile_start(t):
    return t * W_TILE if isinstance(t, int) else pl.multiple_of(t * W_TILE, W_TILE)


def _win_tile(hbm, layer, t):
    return hbm.at[layer, :, pl.ds(_tile_start(t), W_TILE)]


def _wout_tile(hbm, layer, t):
    return hbm.at[layer, pl.ds(_tile_start(t), W_TILE), :]


def _sample_kernel(x_ref, s_in_ref, cos_ref, sin_ref, gn_ref, win_hbm, gsgu_ref, ws_ref, bs_ref,
                   dec_ref, rdec_ref, cdec_ref, sdec_ref, wout_hbm, gfin_ref,
                   y_ref, s_out_ref, vn_ref,
                   h_ref, p_ref, mix_ref, win_stage, wout_stage, sems, *, seq):
    layer = pl.program_id(0)
    step = pl.program_id(1)
    group = s_in_ref.shape[0]
    n_groups = x_ref.shape[0] // (group * seq)
    tiles_per_block = D_A // W_TILE
    win_stream = _TileStream(win_hbm, win_stage, sems, 0, N_IN_TILES, _win_tile)
    wout_stream = _TileStream(wout_hbm, wout_stage, sems, 1, N_OUT_TILES, _wout_tile)

    @pl.when((layer == 0) & (step == 0))
    def _():
        win_stream.prime()
        wout_stream.prime()
        y_ref[...] = x_ref[...]

    @pl.when(step == 0)
    def _():
        h_ref[...] = _bf(_rms(y_ref[...]) * gn_ref[...])

    @pl.when(step < N_IN_TILES)
    def _():
        w = win_stream.take(layer * N_IN_TILES + step)
        p_ref[step] = _dot(h_ref[...], _bf(w[...]))

    for t in range(tiles_per_block):
        @pl.when(step == 2 * tiles_per_block + t)
        def _(t=t):
            cols = slice(t * W_TILE, (t + 1) * W_TILE)
            u = p_ref[t]
            v = p_ref[tiles_per_block + t]
            za = p_ref[2 * tiles_per_block + t]
            vn = _head_norm(v, gsgu_ref[:, cols])
            vn_ref[...] = vn
            mixed = _spatial_gate(_bf(vn), ws_ref, t * (W_TILE // A_DH), bs_ref[:, cols])
            mix_ref[:, cols] = _bf(u * mixed * _silu(za))

    @pl.when((step >= N_IN_TILES) & (step < N_IN_TILES + n_groups))
    def _():
        r0 = pl.multiple_of((step - N_IN_TILES) * (group * seq), group * seq)
        rows = pl.ds(r0, group * seq)
        cos = cos_ref[...]
        sin = sin_ref[...]
        heads_per_tile = W_TILE // B_DK
        for hb in range(B_HEADS):
            t, c0 = hb // heads_per_tile, (hb % heads_per_tile) * B_DK
            blocks = [p_ref[(3 + i) * tiles_per_block + t, rows, c0:c0 + B_DK] for i in range(4)]
            q = _rotary(blocks[0], cos, sin)
            k = _rotary(blocks[1], cos, sin) * (B_DK ** -0.5)
            states = [s_in_ref[i, hb] for i in range(group)]
            o, new_states = _retention(q, k, blocks[2], states, dec_ref[hb], rdec_ref[hb], cdec_ref[hb],
                                       sdec_ref[hb])
            for i in range(group):
                s_out_ref[i, hb] = new_states[i]
            mix_ref[rows, D_A + hb * B_DV:D_A + (hb + 1) * B_DV] = _bf(_rms(o) * _silu(blocks[3]))

    @pl.when(step >= N_IN_TILES + n_groups)
    def _():
        w = wout_stream.take(layer * N_OUT_TILES + step - (N_IN_TILES + n_groups))
        for t in range(N_OUT_TILES):
            @pl.when(step == N_IN_TILES + n_groups + t)
            def _(t=t):
                y_ref[...] += _dot(mix_ref[:, t * W_TILE:(t + 1) * W_TILE], _bf(w[...]))

    @pl.when((layer == pl.num_programs(0) - 1) & (step == pl.num_programs(1) - 1))
    def _():
        y_ref[...] = _rms(y_ref[...]) * gfin_ref[...]


def _const_spec(shape, grid_rank=1):
    zeros = (0,) * len(shape)
    return pl.BlockSpec(shape, lambda *_: zeros, pipeline_mode=pl.Buffered(1))


def _decay_tables(length, n_seq=1):
    lg = jnp.log(1.0 - 2.0 ** (-5.0 - jnp.arange(B_HEADS, dtype=jnp.float32)))
    idx = jnp.arange(length, dtype=jnp.float32)
    diff = idx[:, None] - idx[None, :]
    causal = diff >= 0
    dec = jnp.where(causal[None], jnp.exp(lg[:, None, None] * jnp.where(causal, diff, 0.0)[None]), 0.0)
    rdec = jnp.exp(lg[:, None] * (idx[None, :] + 1.0))
    cdec = jnp.exp(lg[:, None] * (length - 1.0 - idx[None, :]))
    sdec = jnp.exp(lg * length)
    dec = jnp.einsum("ab,hij->haibj", jnp.eye(n_seq, dtype=dec.dtype), dec)
    dec = dec.reshape(B_HEADS, n_seq * length, n_seq * length)
    rdec = jnp.broadcast_to(jnp.tile(rdec, (1, n_seq))[:, :, None], (B_HEADS, n_seq * length, 128))
    cdec = jnp.broadcast_to(jnp.tile(cdec, (1, n_seq))[:, :, None], (B_HEADS, n_seq * length, 128))
    sdec = jnp.broadcast_to(sdec[:, None, None], (B_HEADS, 1, B_DV))
    return dec, rdec, cdec, sdec


def _rope_tables(start, length):
    inv = ROPE_BASE ** (-jnp.arange(HALF, dtype=jnp.float32) / HALF)
    if length <= ROPE_SPLIT:
        ang = (start + jnp.arange(length)).astype(jnp.float32)[:, None] * inv[None, :]
        return jnp.cos(ang), jnp.sin(ang)
    assert length % ROPE_SPLIT == 0
    hi = (start + ROPE_SPLIT * jnp.arange(length // ROPE_SPLIT)).astype(jnp.float32)[:, None] * inv[None, :]
    lo = jnp.arange(ROPE_SPLIT).astype(jnp.float32)[:, None] * inv[None, :]
    ch, sh = jnp.cos(hi)[:, None, :], jnp.sin(hi)[:, None, :]
    cl, sl = jnp.cos(lo)[None, :, :], jnp.sin(lo)[None, :, :]
    cos = (ch * cl - sh * sl).reshape(length, HALF)
    sin = (sh * cl + ch * sl).reshape(length, HALF)
    return cos, sin


def _gate_weights(w_s, b_s, c, n_seq=1):
    i = jnp.arange(c)
    mask = (i[None, :] // CHUNK) <= (i[:, None] // CHUNK)
    w = jnp.where(mask, w_s[..., :c, :c], 0.0)
    w = jnp.einsum("ab,...ij->...aibj", jnp.eye(n_seq, dtype=w.dtype), w)
    w = _bf(w.reshape(w.shape[:-4] + (n_seq * c, n_seq * c)))
    bias = jnp.swapaxes(b_s[..., :c], -1, -2)
    bias = jnp.broadcast_to(bias[..., None], bias.shape + (A_DH,)).reshape(bias.shape[:-1] + (D_A,))
    bias = jnp.tile(bias, (1,) * (bias.ndim - 2) + (n_seq, 1))
    return w, bias


def _prompt_layer(layer, x, cos, sin, gn, w_in, gsgu, ws, bs, decs, w_out, gfin, final_norm):
    seq = x.shape[0]
    tile = PROMPT_TILE
    dec, rdec, cdec, sdec = decs
    row = lambda i: (i, 0)
    kern = functools.partial(_prompt_layer_kernel, layer=layer, final_norm=final_norm)
    return pl.pallas_call(
        kern,
        grid=(seq // tile,),
        in_specs=[
            pl.BlockSpec((tile, D_MODEL), row),
            pl.BlockSpec((tile, D_MODEL), lambda i: (jnp.minimum(i + 1, seq // tile - 1), 0)),
            pl.BlockSpec((tile, HALF), row),
            pl.BlockSpec((tile, HALF), row),
            _const_spec((1, D_MODEL)),
            pl.BlockSpec(memory_space=pl.ANY),
            _const_spec((1, D_A)),
            _const_spec((A_HEADS, A_CHUNK, A_CHUNK)),
            _const_spec((A_CHUNK, D_A)),
            _const_spec((B_HEADS, RET_BLOCK, RET_BLOCK)),
            _const_spec((B_HEADS, RET_BLOCK, 128)),
            _const_spec((B_HEADS, RET_BLOCK, 128)),
            _const_spec((B_HEADS, 1, B_DV)),
            pl.BlockSpec(memory_space=pl.ANY),
            _const_spec((1, D_MODEL)),
        ],
        out_specs=[
            pl.BlockSpec((tile, D_MODEL), row),
            pl.BlockSpec((B_HEADS, B_DK, B_DV), lambda i: (0, 0, 0)),
        ],
        out_shape=[
            jax.ShapeDtypeStruct((seq, D_MODEL), jnp.float32),
            jax.ShapeDtypeStruct((B_HEADS, B_DK, B_DV), jnp.float32),
        ],
        scratch_shapes=[
            pltpu.VMEM((D_MODEL, D_PROJ), jnp.bfloat16),
            pltpu.VMEM((D_MIX, D_MODEL), jnp.bfloat16),
            pltpu.VMEM((W_STAGE_SLOTS, W_IN_CHUNK_ROWS, D_PROJ), jnp.float32),
            pltpu.VMEM((W_STAGE_SLOTS, W_OUT_CHUNK_ROWS, D_MODEL), jnp.float32),
            pltpu.SemaphoreType.DMA((2, W_STAGE_SLOTS)),
            pltpu.VMEM((2, tile, D_MODEL), jnp.bfloat16),
            pltpu.VMEM((tile, D_MIX), jnp.bfloat16),
        ],
        compiler_params=pltpu.CompilerParams(
            dimension_semantics=("arbitrary",), vmem_limit_bytes=VMEM_LIMIT_BYTES),
        name="prompt_layer",
    )(x, x, cos, sin, gn, w_in, gsgu, ws, bs, dec, rdec, cdec, sdec, w_out, gfin)


def _sample_trunk(x, state, cos, sin, gn, w_in, gsgu, ws, bs, decs, w_out, gfin, seq):
    rows = x.shape[0]
    nb = rows // seq
    group = SAMPLE_GROUP
    n_groups = nb // group
    n_steps = N_IN_TILES + n_groups + N_OUT_TILES
    dec, rdec, cdec, sdec = decs
    gate_rows = ws.shape[-1]
    tiles_per_block = D_A // W_TILE

    def per_layer(shape):
        zeros = (0,) * len(shape)
        return pl.BlockSpec((None,) + shape, lambda l, s: (l,) + zeros)

    def state_map(l, s):
        return (l, jnp.clip(s - N_IN_TILES, 0, n_groups - 1), 0, 0, 0)

    state_spec = pl.BlockSpec((None, group, B_HEADS, B_DK, B_DV), state_map)
    kern = functools.partial(_sample_kernel, seq=seq)
    return pl.pallas_call(
        kern,
        grid=(DEPTH, n_steps),
        in_specs=[
            _const_spec((rows, D_MODEL)),
            state_spec,
            _const_spec((group * seq, HALF)),
            _const_spec((group * seq, HALF)),
            per_layer((1, D_MODEL)),
            pl.BlockSpec(memory_space=pl.ANY),
            per_layer((1, D_A)),
            per_layer((A_HEADS, gate_rows, gate_rows)),
            per_layer((gate_rows, D_A)),
            _const_spec((B_HEADS, group * seq, group * seq)),
            _const_spec((B_HEADS, group * seq, 128)),
            _const_spec((B_HEADS, group * seq, 128)),
            _const_spec((B_HEADS, 1, B_DV)),
            pl.BlockSpec(memory_space=pl.ANY),
            _const_spec((1, D_MODEL)),
        ],
        out_specs=[
            pl.BlockSpec((rows, D_MODEL), lambda l, s: (0, 0)),
            state_spec,
            pl.BlockSpec((None, rows, W_TILE),
                         lambda l, s: (l, 0, jnp.clip(s - 2 * tiles_per_block, 0, tiles_per_block - 1))),
        ],
        out_shape=[
            jax.ShapeDtypeStruct((rows, D_MODEL), jnp.float32),
            jax.ShapeDtypeStruct(state.shape, jnp.float32),
            jax.ShapeDtypeStruct((DEPTH, rows, D_A), jnp.float32),
        ],
        scratch_shapes=[
            pltpu.VMEM((rows, D_MODEL), jnp.bfloat16),
            pltpu.VMEM((N_IN_TILES, rows, W_TILE), jnp.float32),
            pltpu.VMEM((rows, D_MIX), jnp.bfloat16),
            pltpu.VMEM((STREAM_SLOTS, D_MODEL, W_TILE), jnp.float32),
            pltpu.VMEM((STREAM_SLOTS, W_TILE, D_MODEL), jnp.float32),
            pltpu.SemaphoreType.DMA((2, STREAM_SLOTS)),
        ],
        compiler_params=pltpu.CompilerParams(
            dimension_semantics=("arbitrary", "arbitrary"), vmem_limit_bytes=VMEM_LIMIT_BYTES),
        name="sample_trunk",
    )(x, state, cos, sin, gn, w_in, gsgu, ws, bs, dec, rdec, cdec, sdec, w_out, gfin)


def kernel(x_prompt, x_sample, state_ret, g_norm, w_in, g_sgu, w_s, b_s, w_out, g_final):
    batch, seq, _ = x_prompt.shape
    dec_batch, dec_seq, _ = x_sample.shape
    assert batch == 1 and seq % PROMPT_TILE == 0
    assert PROMPT_TILE % RET_BLOCK == 0 and RET_BLOCK % A_CHUNK == 0
    assert dec_seq <= CHUNK and dec_batch % SAMPLE_GROUP == 0
    assert (dec_batch * dec_seq) % (SAMPLE_GATE_GROUP * dec_seq) == 0
    assert D_A % W_TILE == 0 and W_TILE % B_DK == 0 and W_TILE % A_DH == 0

    gfin = g_final.reshape(1, D_MODEL)
    gn = g_norm.reshape(DEPTH, 1, D_MODEL)
    gsgu = g_sgu.reshape(DEPTH, 1, D_A)

    cos_s, sin_s = _rope_tables(PAST_LEN, dec_seq)
    cos_s, sin_s = jnp.tile(cos_s, (SAMPLE_GROUP, 1)), jnp.tile(sin_s, (SAMPLE_GROUP, 1))
    ws_s, bs_s = _gate_weights(w_s, b_s, dec_seq, SAMPLE_GATE_GROUP)
    hs, new_ret_sample, vn = _sample_trunk(
        x_sample.reshape(dec_batch * dec_seq, D_MODEL), state_ret, cos_s, sin_s, gn, w_in, gsgu, ws_s, bs_s,
        _decay_tables(dec_seq, SAMPLE_GROUP), w_out, gfin, dec_seq)

    cos_p, sin_p = _rope_tables(0, seq)
    decs_p = _decay_tables(RET_BLOCK)
    ws_p, bs_p = _gate_weights(w_s, b_s, A_CHUNK)
    hp = x_prompt.reshape(seq, D_MODEL)
    ret_p = []
    for l in range(DEPTH):
        hp, sp = _prompt_layer(l, hp, cos_p, sin_p, gn[l], w_in, gsgu[l], ws_p[l], bs_p[l], decs_p, w_out,
                               gfin, l == DEPTH - 1)
        ret_p.append(sp.reshape(batch, B_HEADS, B_DK, B_DV))

    y_prompt = hp.reshape(batch, seq, D_MODEL)
    y_sample = hs.reshape(dec_batch, dec_seq, D_MODEL)
    new_chunk_v = vn.reshape(DEPTH, dec_batch, dec_seq, A_HEADS, A_DH)
    return (y_prompt, y_sample, jnp.stack(ret_p, axis=0), new_ret_sample, new_chunk_v)
```

```python
import functools

import jax
import jax.numpy as jnp
from jax import lax
from jax.experimental import pallas as pl
from jax.experimental.pallas import tpu as pltpu

D_MODEL = 1024
DEPTH = 4
PAST_LEN = 4096
CHUNK = 64
D_MIX = 2 * D_MODEL
D_A = D_MIX // 2
A_HEADS = 8
A_DH = D_A // A_HEADS
A_CHUNK = 128
D_B = D_MIX - D_A
B_HEADS = 4
B_DK = D_B // B_HEADS
B_DV = D_B // B_HEADS
D_PROJ = 3 * D_A + 4 * D_B
ROPE_BASE = 10000.0
EPS = 1e-6
HALF = B_DK // 2
ROPE_SPLIT = 128

OFF_U, OFF_V, OFF_ZA = 0, D_A, 2 * D_A
OFF_Q, OFF_K, OFF_VB, OFF_ZB = 3 * D_A, 3 * D_A + D_B, 3 * D_A + 2 * D_B, 3 * D_A + 3 * D_B

RET_BLOCK = 256
PROMPT_TILE = 512

SAMPLE_GROUP = 2
SAMPLE_GATE_GROUP = 4
W_TILE = 512
N_IN_TILES = D_PROJ // W_TILE
N_OUT_TILES = D_MIX // W_TILE
W_IN_STREAM_SLOTS = 4
W_OUT_STREAM_SLOTS = 3

VMEM_LIMIT_BYTES = 56 * 1024 * 1024


def _bf(x):
    return x.astype(jnp.bfloat16)


def _dot(a, b):
    return jnp.dot(a, b, preferred_element_type=jnp.float32)


def _dot_nt(a, b):
    return lax.dot_general(a, b, (((1,), (1,)), ((), ())), preferred_element_type=jnp.float32)


def _dot_tn(a, b):
    return lax.dot_general(a, b, (((0,), (0,)), ((), ())), preferred_element_type=jnp.float32)


def _rms(x):
    return x * lax.rsqrt(jnp.mean(x * x, axis=-1, keepdims=True) + EPS)


def _silu(z):
    return z * (1.0 / (1.0 + jnp.exp(-z)))


def _rotary(x, cos, sin):
    x1, x2 = x[:, :HALF], x[:, HALF:]
    return jnp.concatenate([x1 * cos - x2 * sin, x1 * sin + x2 * cos], axis=-1)


def _wide(t):
    return jnp.concatenate([t, t], axis=-1)


def _head_norm(v, gsgu):
    parts = [_rms(v[:, h * A_DH:(h + 1) * A_DH]) for h in range(v.shape[1] // A_DH)]
    return jnp.concatenate(parts, axis=-1) * gsgu


def _spatial_gate(vn, ws_ref, head0, bias):
    c = ws_ref.shape[-1]
    rows = []
    for r in range(vn.shape[0] // c):
        cols = [_dot(ws_ref[head0 + h], vn[r * c:(r + 1) * c, h * A_DH:(h + 1) * A_DH])
                for h in range(vn.shape[1] // A_DH)]
        rows.append(jnp.concatenate(cols, axis=-1) + bias)
    return jnp.concatenate(rows, axis=0)


def _retention(q, k, v, states, dec, rdec, cdec, sdec):
    n = len(states)
    seq = q.shape[0] // n
    qb, kb, vb = _bf(q), _bf(k), _bf(v)
    scores = _dot_nt(qb, kb) * dec
    intra = _dot(_bf(scores), vb)
    cross = [_dot(qb[i * seq:(i + 1) * seq], _bf(states[i])) for i in range(n)]
    cross = (cross[0] if n == 1 else jnp.concatenate(cross, axis=0)) * _wide(rdec)
    kw = _bf(k * _wide(cdec))
    new_states = [sdec * states[i] + _dot_tn(kw[i * seq:(i + 1) * seq], vb[i * seq:(i + 1) * seq])
                  for i in range(n)]
    return intra + cross, new_states


def _prompt_layer_kernel(x_ref, cos_ref, sin_ref, gn_ref, win_ref, gsgu_ref, ws_ref, bs_ref,
                         dec_ref, rdec_ref, cdec_ref, sdec_ref, wout_ref, gfin_ref,
                         y_ref, s_ref, mix_ref, *, final_norm):
    tile = x_ref.shape[0]

    @pl.when(pl.program_id(0) == 0)
    def _():
        s_ref[...] = jnp.zeros_like(s_ref)

    for blk in range(tile // RET_BLOCK):
        r = slice(blk * RET_BLOCK, (blk + 1) * RET_BLOCK)
        x = x_ref[r, :]
        h = _bf(_rms(x) * gn_ref[...])

        u = _dot(h, win_ref[:, OFF_U:OFF_U + D_A])
        v = _dot(h, win_ref[:, OFF_V:OFF_V + D_A])
        za = _dot(h, win_ref[:, OFF_ZA:OFF_ZA + D_A])
        vn = _bf(_head_norm(v, gsgu_ref[...]))
        mixed = _spatial_gate(vn, ws_ref, 0, bs_ref[...])
        mix_ref[r, 0:D_A] = _bf(u * mixed * _silu(za))

        cos = cos_ref[r, :]
        sin = sin_ref[r, :]
        for hb in range(B_HEADS):
            c0 = hb * B_DK
            q = _rotary(_dot(h, win_ref[:, OFF_Q + c0:OFF_Q + c0 + B_DK]), cos, sin)
            k = _rotary(_dot(h, win_ref[:, OFF_K + c0:OFF_K + c0 + B_DK]), cos, sin) * (B_DK ** -0.5)
            vb = _dot(h, win_ref[:, OFF_VB + c0:OFF_VB + c0 + B_DV])
            zb = _dot(h, win_ref[:, OFF_ZB + c0:OFF_ZB + c0 + B_DV])
            o, (s_new,) = _retention(q, k, vb, [s_ref[hb]], dec_ref[hb], rdec_ref[hb], cdec_ref[hb],
                                     sdec_ref[hb])
            s_ref[hb] = s_new
            mix_ref[r, D_A + c0:D_A + c0 + B_DV] = _bf(_rms(o) * _silu(zb))

        y = x + _dot(mix_ref[r, :], wout_ref[...])
        if final_norm:
            y = _rms(y) * gfin_ref[...]
        y_ref[r, :] = y


class _TileStream:
    def __init__(self, hbm, stage, sems, sem_row, tiles_per_layer, tile_view):
        self.hbm, self.stage, self.sems, self.sem_row = hbm, stage, sems, sem_row
        self.tiles_per_layer, self.tile_view = tiles_per_layer, tile_view
        self.n_tiles = hbm.shape[0] * tiles_per_layer
        self.n_slots = stage.shape[0]
        self.lookahead = self.n_slots - 1

    def _copy(self, n):
        slot = n % self.n_slots
        src = self.tile_view(self.hbm, n // self.tiles_per_layer, n % self.tiles_per_layer)
        return pltpu.make_async_copy(src, self.stage.at[slot], self.sems.at[self.sem_row, slot])

    def prime(self):
        for n in range(self.lookahead):
            self._copy(n).start()

    def take(self, n):
        @pl.when(n + self.lookahead < self.n_tiles)
        def _():
            self._copy(n + self.lookahead).start()

        self._copy(n).wait()
        return self.stage.at[n % self.n_slots]


def _tile_start(t):
    return t * W_TILE if isinstance(t, int) else pl.multiple_of(t * W_TILE, W_TILE)


def _win_tile(hbm, layer, t):
    return hbm.at[layer, :, pl.ds(_tile_start(t), W_TILE)]


def _wout_tile(hbm, layer, t):
    return hbm.at[layer, pl.ds(_tile_start(t), W_TILE), :]


def _sample_kernel(x_hbm, s_in_ref, cos_ref, sin_ref, gn_ref, win_hbm, gsgu_ref, ws_ref, bs_ref,
                   dec_ref, rdec_ref, cdec_ref, sdec_ref, wout_hbm, gfin_ref,
                   y_ref, s_out_ref, vn_ref, win_bf_ref, wout_bf_ref,
                   h_ref, p_ref, mix_ref, win_stage, wout_stage, sems, x_sem, *, seq):
    layer = pl.program_id(0)
    step = pl.program_id(1)
    group = s_in_ref.shape[0]
    n_groups = y_ref.shape[0] // (group * seq)
    tiles_per_block = D_A // W_TILE
    win_stream = _TileStream(win_hbm, win_stage, sems, 0, N_IN_TILES, _win_tile)
    wout_stream = _TileStream(wout_hbm, wout_stage, sems, 1, N_OUT_TILES, _wout_tile)

    @pl.when((layer == 0) & (step == 0))
    def _():
        win_stream.prime()
        wout_stream.prime()
        x_copy = pltpu.make_async_copy(x_hbm, y_ref, x_sem)
        x_copy.start()
        x_copy.wait()

    @pl.when(step == 0)
    def _():
        h_ref[...] = _bf(_rms(y_ref[...]) * gn_ref[...])

    @pl.when(step < N_IN_TILES)
    def _():
        w = _bf(win_stream.take(layer * N_IN_TILES + step)[...])
        win_bf_ref[...] = w
        p_ref[step] = _dot(h_ref[...], w)

    for t in range(tiles_per_block):
        @pl.when(step == 2 * tiles_per_block + t)
        def _(t=t):
            cols = slice(t * W_TILE, (t + 1) * W_TILE)
            u = p_ref[t]
            v = p_ref[tiles_per_block + t]
            za = p_ref[2 * tiles_per_block + t]
            vn = _head_norm(v, gsgu_ref[:, cols])
            vn_ref[...] = vn
            mixed = _spatial_gate(_bf(vn), ws_ref, t * (W_TILE // A_DH), bs_ref[:, cols])
            mix_ref[:, cols] = _bf(u * mixed * _silu(za))

    @pl.when((step >= N_IN_TILES) & (step < N_IN_TILES + n_groups))
    def _():
        r0 = pl.multiple_of((step - N_IN_TILES) * (group * seq), group * seq)
        rows = pl.ds(r0, group * seq)
        cos = cos_ref[...]
        sin = sin_ref[...]
        heads_per_tile = W_TILE // B_DK
        for hb in range(B_HEADS):
            t, c0 = hb // heads_per_tile, (hb % heads_per_tile) * B_DK
            blocks = [p_ref[(3 + i) * tiles_per_block + t, rows, c0:c0 + B_DK] for i in range(4)]
            q = _rotary(blocks[0], cos, sin)
            k = _rotary(blocks[1], cos, sin) * (B_DK ** -0.5)
            states = [s_in_ref[i, hb] for i in range(group)]
            o, new_states = _retention(q, k, blocks[2], states, dec_ref[hb], rdec_ref[hb], cdec_ref[hb],
                                       sdec_ref[hb])
            for i in range(group):
                s_out_ref[i, hb] = new_states[i]
            mix_ref[rows, D_A + hb * B_DV:D_A + (hb + 1) * B_DV] = _bf(_rms(o) * _silu(blocks[3]))

    @pl.when(step >= N_IN_TILES + n_groups)
    def _():
        w = _bf(wout_stream.take(layer * N_OUT_TILES + step - (N_IN_TILES + n_groups))[...])
        wout_bf_ref[...] = w
        for t in range(N_OUT_TILES):
            @pl.when(step == N_IN_TILES + n_groups + t)
            def _(t=t):
                y_ref[...] += _dot(mix_ref[:, t * W_TILE:(t + 1) * W_TILE], w)

    @pl.when((layer == pl.num_programs(0) - 1) & (step == pl.num_programs(1) - 1))
    def _():
        y_ref[...] = _rms(y_ref[...]) * gfin_ref[...]


def _const_spec(shape):
    zeros = (0,) * len(shape)
    return pl.BlockSpec(shape, lambda *_: zeros, pipeline_mode=pl.Buffered(1))


def _layer_spec(shape, layer):
    zeros = (0,) * len(shape)
    return pl.BlockSpec((None,) + shape, lambda i: (layer,) + zeros, pipeline_mode=pl.Buffered(1))


def _decay_tables(length, n_seq=1):
    lg = jnp.log(1.0 - 2.0 ** (-5.0 - jnp.arange(B_HEADS, dtype=jnp.float32)))
    idx = jnp.arange(length, dtype=jnp.float32)
    diff = idx[:, None] - idx[None, :]
    causal = diff >= 0
    dec = jnp.where(causal[None], jnp.exp(lg[:, None, None] * jnp.where(causal, diff, 0.0)[None]), 0.0)
    rdec = jnp.exp(lg[:, None] * (idx[None, :] + 1.0))
    cdec = jnp.exp(lg[:, None] * (length - 1.0 - idx[None, :]))
    sdec = jnp.exp(lg * length)
    dec = jnp.einsum("ab,hij->haibj", jnp.eye(n_seq, dtype=dec.dtype), dec)
    dec = dec.reshape(B_HEADS, n_seq * length, n_seq * length)
    rdec = jnp.broadcast_to(jnp.tile(rdec, (1, n_seq))[:, :, None], (B_HEADS, n_seq * length, 128))
    cdec = jnp.broadcast_to(jnp.tile(cdec, (1, n_seq))[:, :, None], (B_HEADS, n_seq * length, 128))
    sdec = jnp.broadcast_to(sdec[:, None, None], (B_HEADS, 1, B_DV))
    return dec, rdec, cdec, sdec


def _rope_tables(start, length):
    inv = ROPE_BASE ** (-jnp.arange(HALF, dtype=jnp.float32) / HALF)
    if length <= ROPE_SPLIT:
        ang = (start + jnp.arange(length)).astype(jnp.float32)[:, None] * inv[None, :]
        return jnp.cos(ang), jnp.sin(ang)
    assert length % ROPE_SPLIT == 0
    hi = (start + ROPE_SPLIT * jnp.arange(length // ROPE_SPLIT)).astype(jnp.float32)[:, None] * inv[None, :]
    lo = jnp.arange(ROPE_SPLIT).astype(jnp.float32)[:, None] * inv[None, :]
    ch, sh = jnp.cos(hi)[:, None, :], jnp.sin(hi)[:, None, :]
    cl, sl = jnp.cos(lo)[None, :, :], jnp.sin(lo)[None, :, :]
    cos = (ch * cl - sh * sl).reshape(length, HALF)
    sin = (sh * cl + ch * sl).reshape(length, HALF)
    return cos, sin


def _gate_weights(w_s, b_s, c, n_seq=1):
    i = jnp.arange(c)
    mask = (i[None, :] // CHUNK) <= (i[:, None] // CHUNK)
    w = jnp.where(mask, w_s[..., :c, :c], 0.0)
    w = jnp.einsum("ab,...ij->...aibj", jnp.eye(n_seq, dtype=w.dtype), w)
    w = _bf(w.reshape(w.shape[:-4] + (n_seq * c, n_seq * c)))
    bias = jnp.swapaxes(b_s[..., :c], -1, -2)
    bias = jnp.broadcast_to(bias[..., None], bias.shape + (A_DH,)).reshape(bias.shape[:-1] + (D_A,))
    bias = jnp.tile(bias, (1,) * (bias.ndim - 2) + (n_seq, 1))
    return w, bias


def _prompt_layer(layer, x, cos, sin, gn, win, gsgu, ws, bs, decs, wout, gfin, final_norm):
    seq = x.shape[0]
    tile = PROMPT_TILE
    dec, rdec, cdec, sdec = decs
    row = lambda i: (i, 0)
    kern = functools.partial(_prompt_layer_kernel, final_norm=final_norm)
    return pl.pallas_call(
        kern,
        grid=(seq // tile,),
        in_specs=[
            pl.BlockSpec((tile, D_MODEL), row),
            pl.BlockSpec((tile, HALF), row),
            pl.BlockSpec((tile, HALF), row),
            _const_spec((1, D_MODEL)),
            _layer_spec((D_MODEL, D_PROJ), layer),
            _const_spec((1, D_A)),
            _const_spec((A_HEADS, A_CHUNK, A_CHUNK)),
            _const_spec((A_CHUNK, D_A)),
            _const_spec((B_HEADS, RET_BLOCK, RET_BLOCK)),
            _const_spec((B_HEADS, RET_BLOCK, 128)),
            _const_spec((B_HEADS, RET_BLOCK, 128)),
            _const_spec((B_HEADS, 1, B_DV)),
            _layer_spec((D_MIX, D_MODEL), layer),
            _const_spec((1, D_MODEL)),
        ],
        out_specs=[
            pl.BlockSpec((tile, D_MODEL), row),
            pl.BlockSpec((B_HEADS, B_DK, B_DV), lambda i: (0, 0, 0)),
        ],
        out_shape=[
            jax.ShapeDtypeStruct((seq, D_MODEL), jnp.float32),
            jax.ShapeDtypeStruct((B_HEADS, B_DK, B_DV), jnp.float32),
        ],
        scratch_shapes=[pltpu.VMEM((tile, D_MIX), jnp.bfloat16)],
        compiler_params=pltpu.CompilerParams(
            dimension_semantics=("arbitrary",), vmem_limit_bytes=VMEM_LIMIT_BYTES),
        name="prompt_layer",
    )(x, cos, sin, gn, win, gsgu, ws, bs, dec, rdec, cdec, sdec, wout, gfin)


def _sample_trunk(x, state, cos, sin, gn, w_in, gsgu, ws, bs, decs, w_out, gfin, seq):
    rows = x.shape[0]
    nb = rows // seq
    group = SAMPLE_GROUP
    n_groups = nb // group
    n_steps = N_IN_TILES + n_groups + N_OUT_TILES
    dec, rdec, cdec, sdec = decs
    gate_rows = ws.shape[-1]
    tiles_per_block = D_A // W_TILE

    def per_layer(shape):
        zeros = (0,) * len(shape)
        return pl.BlockSpec((None,) + shape, lambda l, s: (l,) + zeros)

    def state_map(l, s):
        return (l, jnp.clip(s - N_IN_TILES, 0, n_groups - 1), 0, 0, 0)

    state_spec = pl.BlockSpec((None, group, B_HEADS, B_DK, B_DV), state_map)
    kern = functools.partial(_sample_kernel, seq=seq)
    return pl.pallas_call(
        kern,
        grid=(DEPTH, n_steps),
        in_specs=[
            pl.BlockSpec(memory_space=pl.ANY),
            state_spec,
            _const_spec((group * seq, HALF)),
            _const_spec((group * seq, HALF)),
            per_layer((1, D_MODEL)),
            pl.BlockSpec(memory_space=pl.ANY),
            per_layer((1, D_A)),
            per_layer((A_HEADS, gate_rows, gate_rows)),
            per_layer((gate_rows, D_A)),
            _const_spec((B_HEADS, group * seq, group * seq)),
            _const_spec((B_HEADS, group * seq, 128)),
            _const_spec((B_HEADS, group * seq, 128)),
            _const_spec((B_HEADS, 1, B_DV)),
            pl.BlockSpec(memory_space=pl.ANY),
            _const_spec((1, D_MODEL)),
        ],
        out_specs=[
            pl.BlockSpec((rows, D_MODEL), lambda l, s: (0, 0)),
            state_spec,
            pl.BlockSpec((None, rows, W_TILE),
                         lambda l, s: (l, 0, jnp.clip(s - 2 * tiles_per_block, 0, tiles_per_block - 1))),
            pl.BlockSpec((None, D_MODEL, W_TILE), lambda l, s: (l, 0, jnp.minimum(s, N_IN_TILES - 1))),
            pl.BlockSpec((None, W_TILE, D_MODEL),
                         lambda l, s: (l, jnp.clip(s - N_IN_TILES - n_groups, 0, N_OUT_TILES - 1), 0)),
        ],
        out_shape=[
            jax.ShapeDtypeStruct((rows, D_MODEL), jnp.float32),
            jax.ShapeDtypeStruct(state.shape, jnp.float32),
            jax.ShapeDtypeStruct((DEPTH, rows, D_A), jnp.float32),
            jax.ShapeDtypeStruct(w_in.shape, jnp.bfloat16),
            jax.ShapeDtypeStruct(w_out.shape, jnp.bfloat16),
        ],
        scratch_shapes=[
            pltpu.VMEM((rows, D_MODEL), jnp.bfloat16),
            pltpu.VMEM((N_IN_TILES, rows, W_TILE), jnp.float32),
            pltpu.VMEM((rows, D_MIX), jnp.bfloat16),
            pltpu.VMEM((W_IN_STREAM_SLOTS, D_MODEL, W_TILE), jnp.float32),
            pltpu.VMEM((W_OUT_STREAM_SLOTS, W_TILE, D_MODEL), jnp.float32),
            pltpu.SemaphoreType.DMA((2, max(W_IN_STREAM_SLOTS, W_OUT_STREAM_SLOTS))),
            pltpu.SemaphoreType.DMA(()),
        ],
        compiler_params=pltpu.CompilerParams(
            dimension_semantics=("arbitrary", "arbitrary"), vmem_limit_bytes=VMEM_LIMIT_BYTES),
        name="sample_trunk",
    )(x, state, cos, sin, gn, w_in, gsgu, ws, bs, dec, rdec, cdec, sdec, w_out, gfin)


def kernel(x_prompt, x_sample, state_ret, g_norm, w_in, g_sgu, w_s, b_s, w_out, g_final):
    batch, seq, _ = x_prompt.shape
    dec_batch, dec_seq, _ = x_sample.shape
    assert batch == 1 and seq % PROMPT_TILE == 0
    assert PROMPT_TILE % RET_BLOCK == 0 and RET_BLOCK % A_CHUNK == 0
    assert dec_seq <= CHUNK and dec_batch % SAMPLE_GROUP == 0
    assert (dec_batch * dec_seq) % (SAMPLE_GATE_GROUP * dec_seq) == 0
    assert D_A % W_TILE == 0 and W_TILE % B_DK == 0 and W_TILE % A_DH == 0

    gfin = g_final.reshape(1, D_MODEL)
    gn = g_norm.reshape(DEPTH, 1, D_MODEL)
    gsgu = g_sgu.reshape(DEPTH, 1, D_A)

    cos_s, sin_s = _rope_tables(PAST_LEN, dec_seq)
    cos_s, sin_s = jnp.tile(cos_s, (SAMPLE_GROUP, 1)), jnp.tile(sin_s, (SAMPLE_GROUP, 1))
    ws_s, bs_s = _gate_weights(w_s, b_s, dec_seq, SAMPLE_GATE_GROUP)
    hs, new_ret_sample, vn, win_bf, wout_bf = _sample_trunk(
        x_sample.reshape(dec_batch * dec_seq, D_MODEL), state_ret, cos_s, sin_s, gn, w_in, gsgu, ws_s, bs_s,
        _decay_tables(dec_seq, SAMPLE_GROUP), w_out, gfin, dec_seq)

    cos_p, sin_p = _rope_tables(0, seq)
    decs_p = _decay_tables(RET_BLOCK)
    ws_p, bs_p = _gate_weights(w_s, b_s, A_CHUNK)
    hp = x_prompt.reshape(seq, D_MODEL)
    ret_p = []
    for l in range(DEPTH):
        hp, sp = _prompt_layer(l, hp, cos_p, sin_p, gn[l], win_bf, gsgu[l], ws_p[l], bs_p[l], decs_p, wout_bf,
                               gfin, l == DEPTH - 1)
        ret_p.append(sp.reshape(batch, B_HEADS, B_DK, B_DV))

    y_prompt = hp.reshape(batch, seq, D_MODEL)
    y_sample = hs.reshape(dec_batch, dec_seq, D_MODEL)
    new_chunk_v = vn.reshape(DEPTH, dec_batch, dec_seq, A_HEADS, A_DH)
    return (y_prompt, y_sample, jnp.stack(ret_p, axis=0), new_ret_sample, new_chunk_v)
```

```python
import functools

import jax
import jax.numpy as jnp
from jax import lax
from jax.experimental import pallas as pl
from jax.experimental.pallas import tpu as pltpu

D_MODEL = 1024
DEPTH = 4
PAST_LEN = 4096
CHUNK = 64
D_MIX = 2 * D_MODEL
D_A = D_MIX // 2
A_HEADS = 8
A_DH = D_A // A_HEADS
A_CHUNK = 128
D_B = D_MIX - D_A
B_HEADS = 4
B_DK = D_B // B_HEADS
B_DV = D_B // B_HEADS
D_PROJ = 3 * D_A + 4 * D_B
ROPE_BASE = 10000.0
EPS = 1e-6
HALF = B_DK // 2
ROPE_SPLIT = 128

OFF_U, OFF_V, OFF_ZA = 0, D_A, 2 * D_A
OFF_Q, OFF_K, OFF_VB, OFF_ZB = 3 * D_A, 3 * D_A + D_B, 3 * D_A + 2 * D_B, 3 * D_A + 3 * D_B

RET_BLOCK = 256
PROMPT_TILE = 512

SAMPLE_GROUP = 2
SAMPLE_GATE_GROUP = 4
W_TILE = 512
N_IN_TILES = D_PROJ // W_TILE
N_OUT_TILES = D_MIX // W_TILE
W_IN_STREAM_SLOTS = 4
W_OUT_STREAM_SLOTS = 3
STATE_STREAM_SLOTS = 3

VMEM_LIMIT_BYTES = 56 * 1024 * 1024


def _bf(x):
    return x.astype(jnp.bfloat16)


def _dot(a, b):
    return jnp.dot(a, b, preferred_element_type=jnp.float32)


def _dot_nt(a, b):
    return lax.dot_general(a, b, (((1,), (1,)), ((), ())), preferred_element_type=jnp.float32)


def _dot_tn(a, b):
    return lax.dot_general(a, b, (((0,), (0,)), ((), ())), preferred_element_type=jnp.float32)


def _rms(x):
    return x * lax.rsqrt(jnp.mean(x * x, axis=-1, keepdims=True) + EPS)


def _silu(z):
    return z * (1.0 / (1.0 + jnp.exp(-z)))


def _rotary(x, cos, sin):
    x1, x2 = x[:, :HALF], x[:, HALF:]
    return jnp.concatenate([x1 * cos - x2 * sin, x1 * sin + x2 * cos], axis=-1)


def _wide(t):
    return jnp.concatenate([t, t], axis=-1)


def _head_norm(v, gsgu):
    parts = [_rms(v[:, h * A_DH:(h + 1) * A_DH]) for h in range(v.shape[1] // A_DH)]
    return jnp.concatenate(parts, axis=-1) * gsgu


def _spatial_gate(vn, ws_ref, head0, bias):
    c = ws_ref.shape[-1]
    rows = []
    for r in range(vn.shape[0] // c):
        cols = [_dot(ws_ref[head0 + h], vn[r * c:(r + 1) * c, h * A_DH:(h + 1) * A_DH])
                for h in range(vn.shape[1] // A_DH)]
        rows.append(jnp.concatenate(cols, axis=-1) + bias)
    return jnp.concatenate(rows, axis=0)


def _retention(q, k, v, states, dec, rdec, cdec, sdec):
    n = len(states)
    seq = q.shape[0] // n
    qb, kb, vb = _bf(q), _bf(k), _bf(v)
    scores = _dot_nt(qb, kb) * dec
    intra = _dot(_bf(scores), vb)
    cross = [_dot(qb[i * seq:(i + 1) * seq], _bf(states[i])) for i in range(n)]
    cross = (cross[0] if n == 1 else jnp.concatenate(cross, axis=0)) * _wide(rdec)
    kw = _bf(k * _wide(cdec))
    new_states = [sdec * states[i] + _dot_tn(kw[i * seq:(i + 1) * seq], vb[i * seq:(i + 1) * seq])
                  for i in range(n)]
    return intra + cross, new_states


def _prompt_layer_kernel(x_ref, cos_ref, sin_ref, gn_ref, win_ref, gsgu_ref, ws_ref, bs_ref,
                         dec_ref, rdec_ref, cdec_ref, sdec_ref, wout_ref, gfin_ref,
                         y_ref, s_ref, mix_ref, *, final_norm):
    tile = x_ref.shape[0]

    @pl.when(pl.program_id(0) == 0)
    def _():
        s_ref[...] = jnp.zeros_like(s_ref)

    for blk in range(tile // RET_BLOCK):
        r = slice(blk * RET_BLOCK, (blk + 1) * RET_BLOCK)
        x = x_ref[r, :]
        h = _bf(_rms(x) * gn_ref[...])

        u = _dot(h, win_ref[:, OFF_U:OFF_U + D_A])
        v = _dot(h, win_ref[:, OFF_V:OFF_V + D_A])
        za = _dot(h, win_ref[:, OFF_ZA:OFF_ZA + D_A])
        vn = _bf(_head_norm(v, gsgu_ref[...]))
        mixed = _spatial_gate(vn, ws_ref, 0, bs_ref[...])
        mix_ref[r, 0:D_A] = _bf(u * mixed * _silu(za))

        cos = cos_ref[r, :]
        sin = sin_ref[r, :]
        for hb in range(B_HEADS):
            c0 = hb * B_DK
            q = _rotary(_dot(h, win_ref[:, OFF_Q + c0:OFF_Q + c0 + B_DK]), cos, sin)
            k = _rotary(_dot(h, win_ref[:, OFF_K + c0:OFF_K + c0 + B_DK]), cos, sin) * (B_DK ** -0.5)
            vb = _dot(h, win_ref[:, OFF_VB + c0:OFF_VB + c0 + B_DV])
            zb = _dot(h, win_ref[:, OFF_ZB + c0:OFF_ZB + c0 + B_DV])
            o, (s_new,) = _retention(q, k, vb, [s_ref[hb]], dec_ref[hb], rdec_ref[hb], cdec_ref[hb],
                                     sdec_ref[hb])
            s_ref[hb] = s_new
            mix_ref[r, D_A + c0:D_A + c0 + B_DV] = _bf(_rms(o) * _silu(zb))

        y = x + _dot(mix_ref[r, :], wout_ref[...])
        if final_norm:
            y = _rms(y) * gfin_ref[...]
        y_ref[r, :] = y


class _TileStream:
    def __init__(self, hbm, stage, sems, sem_row, tiles_per_layer, tile_view):
        self.hbm, self.stage, self.sems, self.sem_row = hbm, stage, sems, sem_row
        self.tiles_per_layer, self.tile_view = tiles_per_layer, tile_view
        self.n_tiles = hbm.shape[0] * tiles_per_layer
        self.n_slots = stage.shape[0]
        self.lookahead = self.n_slots - 1

    def _copy(self, n):
        slot = n % self.n_slots
        src = self.tile_view(self.hbm, n // self.tiles_per_layer, n % self.tiles_per_layer)
        return pltpu.make_async_copy(src, self.stage.at[slot], self.sems.at[self.sem_row, slot])

    def prime(self):
        for n in range(self.lookahead):
            self._copy(n).start()

    def take(self, n):
        @pl.when(n + self.lookahead < self.n_tiles)
        def _():
            self._copy(n + self.lookahead).start()

        self._copy(n).wait()
        return self.stage.at[n % self.n_slots]


def _tile_start(t):
    return t * W_TILE if isinstance(t, int) else pl.multiple_of(t * W_TILE, W_TILE)


def _win_tile(hbm, layer, t):
    return hbm.at[layer, :, pl.ds(_tile_start(t), W_TILE)]


def _wout_tile(hbm, layer, t):
    return hbm.at[layer, pl.ds(_tile_start(t), W_TILE), :]


def _state_group(hbm, layer, g):
    return hbm.at[layer, pl.ds(g * SAMPLE_GROUP, SAMPLE_GROUP)]


def _sample_kernel(x_hbm, s_in_hbm, cos_ref, sin_ref, gn_ref, win_hbm, gsgu_ref, ws_ref, bs_ref,
                   dec_ref, rdec_ref, cdec_ref, sdec_ref, wout_hbm, gfin_ref,
                   y_ref, s_out_ref, vn_ref, win_bf_ref, wout_bf_ref,
                   h_ref, p_ref, mix_ref, win_stage, wout_stage, state_stage, sems, x_sem, *, seq):
    layer = pl.program_id(0)
    step = pl.program_id(1)
    group = SAMPLE_GROUP
    n_groups = y_ref.shape[0] // (group * seq)
    tiles_per_block = D_A // W_TILE
    win_stream = _TileStream(win_hbm, win_stage, sems, 0, N_IN_TILES, _win_tile)
    wout_stream = _TileStream(wout_hbm, wout_stage, sems, 1, N_OUT_TILES, _wout_tile)
    state_stream = _TileStream(s_in_hbm, state_stage, sems, 2, n_groups, _state_group)

    @pl.when((layer == 0) & (step == 0))
    def _():
        win_stream.prime()
        wout_stream.prime()
        state_stream.prime()
        x_copy = pltpu.make_async_copy(x_hbm, y_ref, x_sem)
        x_copy.start()
        x_copy.wait()

    @pl.when(step == 0)
    def _():
        h_ref[...] = _bf(_rms(y_ref[...]) * gn_ref[...])

    @pl.when(step < N_IN_TILES)
    def _():
        w = _bf(win_stream.take(layer * N_IN_TILES + step)[...])
        win_bf_ref[...] = w
        p_ref[step] = _dot(h_ref[...], w)

    for t in range(tiles_per_block):
        @pl.when(step == 2 * tiles_per_block + t)
        def _(t=t):
            cols = slice(t * W_TILE, (t + 1) * W_TILE)
            u = p_ref[t]
            v = p_ref[tiles_per_block + t]
            za = p_ref[2 * tiles_per_block + t]
            vn = _head_norm(v, gsgu_ref[:, cols])
            vn_ref[...] = vn
            mixed = _spatial_gate(_bf(vn), ws_ref, t * (W_TILE // A_DH), bs_ref[:, cols])
            mix_ref[:, cols] = _bf(u * mixed * _silu(za))

    @pl.when((step >= N_IN_TILES) & (step < N_IN_TILES + n_groups))
    def _():
        r0 = pl.multiple_of((step - N_IN_TILES) * (group * seq), group * seq)
        rows = pl.ds(r0, group * seq)
        cos = cos_ref[...]
        sin = sin_ref[...]
        s_in = state_stream.take(layer * n_groups + step - N_IN_TILES)
        heads_per_tile = W_TILE // B_DK
        for hb in range(B_HEADS):
            t, c0 = hb // heads_per_tile, (hb % heads_per_tile) * B_DK
            blocks = [p_ref[(3 + i) * tiles_per_block + t, rows, c0:c0 + B_DK] for i in range(4)]
            q = _rotary(blocks[0], cos, sin)
            k = _rotary(blocks[1], cos, sin) * (B_DK ** -0.5)
            states = [s_in[i, hb] for i in range(group)]
            o, new_states = _retention(q, k, blocks[2], states, dec_ref[hb], rdec_ref[hb], cdec_ref[hb],
                                       sdec_ref[hb])
            for i in range(group):
                s_out_ref[i, hb] = new_states[i]
            mix_ref[rows, D_A + hb * B_DV:D_A + (hb + 1) * B_DV] = _bf(_rms(o) * _silu(blocks[3]))

    @pl.when(step >= N_IN_TILES + n_groups)
    def _():
        w = _bf(wout_stream.take(layer * N_OUT_TILES + step - (N_IN_TILES + n_groups))[...])
        wout_bf_ref[...] = w
        for t in range(N_OUT_TILES):
            @pl.when(step == N_IN_TILES + n_groups + t)
            def _(t=t):
                y_ref[...] += _dot(mix_ref[:, t * W_TILE:(t + 1) * W_TILE], w)

    @pl.when((layer == pl.num_programs(0) - 1) & (step == pl.num_programs(1) - 1))
    def _():
        y_ref[...] = _rms(y_ref[...]) * gfin_ref[...]


def _const_spec(shape):
    zeros = (0,) * len(shape)
    return pl.BlockSpec(shape, lambda *_: zeros, pipeline_mode=pl.Buffered(1))


def _layer_spec(shape, layer):
    zeros = (0,) * len(shape)
    return pl.BlockSpec((None,) + shape, lambda i: (layer,) + zeros, pipeline_mode=pl.Buffered(1))


def _decay_tables(length, n_seq=1):
    lg = jnp.log(1.0 - 2.0 ** (-5.0 - jnp.arange(B_HEADS, dtype=jnp.float32)))
    idx = jnp.arange(length, dtype=jnp.float32)
    diff = idx[:, None] - idx[None, :]
    causal = diff >= 0
    dec = jnp.where(causal[None], jnp.exp(lg[:, None, None] * jnp.where(causal, diff, 0.0)[None]), 0.0)
    rdec = jnp.exp(lg[:, None] * (idx[None, :] + 1.0))
    cdec = jnp.exp(lg[:, None] * (length - 1.0 - idx[None, :]))
    sdec = jnp.exp(lg * length)
    dec = jnp.einsum("ab,hij->haibj", jnp.eye(n_seq, dtype=dec.dtype), dec)
    dec = dec.reshape(B_HEADS, n_seq * length, n_seq * length)
    rdec = jnp.broadcast_to(jnp.tile(rdec, (1, n_seq))[:, :, None], (B_HEADS, n_seq * length, 128))
    cdec = jnp.broadcast_to(jnp.tile(cdec, (1, n_seq))[:, :, None], (B_HEADS, n_seq * length, 128))
    sdec = jnp.broadcast_to(sdec[:, None, None], (B_HEADS, 1, B_DV))
    return dec, rdec, cdec, sdec


def _rope_tables(start, length):
    inv = ROPE_BASE ** (-jnp.arange(HALF, dtype=jnp.float32) / HALF)
    if length <= ROPE_SPLIT:
        ang = (start + jnp.arange(length)).astype(jnp.float32)[:, None] * inv[None, :]
        return jnp.cos(ang), jnp.sin(ang)
    assert length % ROPE_SPLIT == 0
    hi = (start + ROPE_SPLIT * jnp.arange(length // ROPE_SPLIT)).astype(jnp.float32)[:, None] * inv[None, :]
    lo = jnp.arange(ROPE_SPLIT).astype(jnp.float32)[:, None] * inv[None, :]
    ch, sh = jnp.cos(hi)[:, None, :], jnp.sin(hi)[:, None, :]
    cl, sl = jnp.cos(lo)[None, :, :], jnp.sin(lo)[None, :, :]
    cos = (ch * cl - sh * sl).reshape(length, HALF)
    sin = (sh * cl + ch * sl).reshape(length, HALF)
    return cos, sin


def _gate_weights(w_s, b_s, c, n_seq=1):
    i = jnp.arange(c)
    mask = (i[None, :] // CHUNK) <= (i[:, None] // CHUNK)
    w = jnp.where(mask, w_s[..., :c, :c], 0.0)
    w = jnp.einsum("ab,...ij->...aibj", jnp.eye(n_seq, dtype=w.dtype), w)
    w = _bf(w.reshape(w.shape[:-4] + (n_seq * c, n_seq * c)))
    bias = jnp.swapaxes(b_s[..., :c], -1, -2)
    bias = jnp.broadcast_to(bias[..., None], bias.shape + (A_DH,)).reshape(bias.shape[:-1] + (D_A,))
    bias = jnp.tile(bias, (1,) * (bias.ndim - 2) + (n_seq, 1))
    return w, bias


def _prompt_layer(layer, x, cos, sin, gn, win, gsgu, ws, bs, decs, wout, gfin, final_norm):
    seq = x.shape[0]
    tile = PROMPT_TILE
    dec, rdec, cdec, sdec = decs
    row = lambda i: (i, 0)
    kern = functools.partial(_prompt_layer_kernel, final_norm=final_norm)
    return pl.pallas_call(
        kern,
        grid=(seq // tile,),
        in_specs=[
            pl.BlockSpec((tile, D_MODEL), row),
            pl.BlockSpec((tile, HALF), row),
            pl.BlockSpec((tile, HALF), row),
            _const_spec((1, D_MODEL)),
            _layer_spec((D_MODEL, D_PROJ), layer),
            _const_spec((1, D_A)),
            _const_spec((A_HEADS, A_CHUNK, A_CHUNK)),
            _const_spec((A_CHUNK, D_A)),
            _const_spec((B_HEADS, RET_BLOCK, RET_BLOCK)),
            _const_spec((B_HEADS, RET_BLOCK, 128)),
            _const_spec((B_HEADS, RET_BLOCK, 128)),
            _const_spec((B_HEADS, 1, B_DV)),
            _layer_spec((D_MIX, D_MODEL), layer),
            _const_spec((1, D_MODEL)),
        ],
        out_specs=[
            pl.BlockSpec((tile, D_MODEL), row),
            pl.BlockSpec((B_HEADS, B_DK, B_DV), lambda i: (0, 0, 0)),
        ],
        out_shape=[
            jax.ShapeDtypeStruct((seq, D_MODEL), jnp.float32),
            jax.ShapeDtypeStruct((B_HEADS, B_DK, B_DV), jnp.float32),
        ],
        scratch_shapes=[pltpu.VMEM((tile, D_MIX), jnp.bfloat16)],
        compiler_params=pltpu.CompilerParams(
            dimension_semantics=("arbitrary",), vmem_limit_bytes=VMEM_LIMIT_BYTES),
        name="prompt_layer",
    )(x, cos, sin, gn, win, gsgu, ws, bs, dec, rdec, cdec, sdec, wout, gfin)


def _sample_trunk(x, state, cos, sin, gn, w_in, gsgu, ws, bs, decs, w_out, gfin, seq):
    rows = x.shape[0]
    nb = rows // seq
    group = SAMPLE_GROUP
    n_groups = nb // group
    n_steps = N_IN_TILES + n_groups + N_OUT_TILES
    dec, rdec, cdec, sdec = decs
    gate_rows = ws.shape[-1]
    tiles_per_block = D_A // W_TILE

    def per_layer(shape):
        zeros = (0,) * len(shape)
        return pl.BlockSpec((None,) + shape, lambda l, s: (l,) + zeros)

    def state_map(l, s):
        return (l, jnp.clip(s - N_IN_TILES, 0, n_groups - 1), 0, 0, 0)

    state_spec = pl.BlockSpec((None, group, B_HEADS, B_DK, B_DV), state_map)
    kern = functools.partial(_sample_kernel, seq=seq)
    return pl.pallas_call(
        kern,
        grid=(DEPTH, n_steps),
        in_specs=[
            pl.BlockSpec(memory_space=pl.ANY),
            pl.BlockSpec(memory_space=pl.ANY),
            _const_spec((group * seq, HALF)),
            _const_spec((group * seq, HALF)),
            per_layer((1, D_MODEL)),
            pl.BlockSpec(memory_space=pl.ANY),
            per_layer((1, D_A)),
            per_layer((A_HEADS, gate_rows, gate_rows)),
            per_layer((gate_rows, D_A)),
            _const_spec((B_HEADS, group * seq, group * seq)),
            _const_spec((B_HEADS, group * seq, 128)),
            _const_spec((B_HEADS, group * seq, 128)),
            _const_spec((B_HEADS, 1, B_DV)),
            pl.BlockSpec(memory_space=pl.ANY),
            _const_spec((1, D_MODEL)),
        ],
        out_specs=[
            pl.BlockSpec((rows, D_MODEL), lambda l, s: (0, 0)),
            state_spec,
            pl.BlockSpec((None, rows, W_TILE),
                         lambda l, s: (l, 0, jnp.clip(s - 2 * tiles_per_block, 0, tiles_per_block - 1))),
            pl.BlockSpec((None, D_MODEL, W_TILE), lambda l, s: (l, 0, jnp.minimum(s, N_IN_TILES - 1))),
            pl.BlockSpec((None, W_TILE, D_MODEL),
                         lambda l, s: (l, jnp.clip(s - N_IN_TILES - n_groups, 0, N_OUT_TILES - 1), 0)),
        ],
        out_shape=[
            jax.ShapeDtypeStruct((rows, D_MODEL), jnp.float32),
            jax.ShapeDtypeStruct(state.shape, jnp.float32),
            jax.ShapeDtypeStruct((DEPTH, rows, D_A), jnp.float32),
            jax.ShapeDtypeStruct(w_in.shape, jnp.bfloat16),
            jax.ShapeDtypeStruct(w_out.shape, jnp.bfloat16),
        ],
        scratch_shapes=[
            pltpu.VMEM((rows, D_MODEL), jnp.bfloat16),
            pltpu.VMEM((N_IN_TILES, rows, W_TILE), jnp.float32),
            pltpu.VMEM((rows, D_MIX), jnp.bfloat16),
            pltpu.VMEM((W_IN_STREAM_SLOTS, D_MODEL, W_TILE), jnp.float32),
            pltpu.VMEM((W_OUT_STREAM_SLOTS, W_TILE, D_MODEL), jnp.float32),
            pltpu.VMEM((STATE_STREAM_SLOTS, group, B_HEADS, B_DK, B_DV), jnp.float32),
            pltpu.SemaphoreType.DMA((3, max(W_IN_STREAM_SLOTS, W_OUT_STREAM_SLOTS, STATE_STREAM_SLOTS))),
            pltpu.SemaphoreType.DMA(()),
        ],
        compiler_params=pltpu.CompilerParams(
            dimension_semantics=("arbitrary", "arbitrary"), vmem_limit_bytes=VMEM_LIMIT_BYTES),
        name="sample_trunk",
    )(x, state, cos, sin, gn, w_in, gsgu, ws, bs, dec, rdec, cdec, sdec, w_out, gfin)


def kernel(x_prompt, x_sample, state_ret, g_norm, w_in, g_sgu, w_s, b_s, w_out, g_final):
    batch, seq, _ = x_prompt.shape
    dec_batch, dec_seq, _ = x_sample.shape
    assert batch == 1 and seq % PROMPT_TILE == 0
    assert PROMPT_TILE % RET_BLOCK == 0 and RET_BLOCK % A_CHUNK == 0
    assert dec_seq <= CHUNK and dec_batch % SAMPLE_GROUP == 0
    assert (dec_batch * dec_seq) % (SAMPLE_GATE_GROUP * dec_seq) == 0
    assert D_A % W_TILE == 0 and W_TILE % B_DK == 0 and W_TILE % A_DH == 0

    gfin = g_final.reshape(1, D_MODEL)
    gn = g_norm.reshape(DEPTH, 1, D_MODEL)
    gsgu = g_sgu.reshape(DEPTH, 1, D_A)

    cos_s, sin_s = _rope_tables(PAST_LEN, dec_seq)
    cos_s, sin_s = jnp.tile(cos_s, (SAMPLE_GROUP, 1)), jnp.tile(sin_s, (SAMPLE_GROUP, 1))
    ws_s, bs_s = _gate_weights(w_s, b_s, dec_seq, SAMPLE_GATE_GROUP)
    hs, new_ret_sample, vn, win_bf, wout_bf = _sample_trunk(
        x_sample.reshape(dec_batch * dec_seq, D_MODEL), state_ret, cos_s, sin_s, gn, w_in, gsgu, ws_s, bs_s,
        _decay_tables(dec_seq, SAMPLE_GROUP), w_out, gfin, dec_seq)

    cos_p, sin_p = _rope_tables(0, seq)
    decs_p = _decay_tables(RET_BLOCK)
    ws_p, bs_p = _gate_weights(w_s, b_s, A_CHUNK)
    hp = x_prompt.reshape(seq, D_MODEL)
    ret_p = []
    for l in range(DEPTH):
        hp, sp = _prompt_layer(l, hp, cos_p, sin_p, gn[l], win_bf, gsgu[l], ws_p[l], bs_p[l], decs_p, wout_bf,
                               gfin, l == DEPTH - 1)
        ret_p.append(sp.reshape(batch, B_HEADS, B_DK, B_DV))

    y_prompt = hp.reshape(batch, seq, D_MODEL)
    y_sample = hs.reshape(dec_batch, dec_seq, D_MODEL)
    new_chunk_v = vn.reshape(DEPTH, dec_batch, dec_seq, A_HEADS, A_DH)
    return (y_prompt, y_sample, jnp.stack(ret_p, axis=0), new_ret_sample, new_chunk_v)
```

```python
import functools

import jax
import jax.numpy as jnp
from jax import lax
from jax.experimental import pallas as pl
from jax.experimental.pallas import tpu as pltpu

D_MODEL = 1024
DEPTH = 4
PAST_LEN = 4096
CHUNK = 64
D_MIX = 2 * D_MODEL
D_A = D_MIX // 2
A_HEADS = 8
A_DH = D_A // A_HEADS
A_CHUNK = 128
D_B = D_MIX - D_A
B_HEADS = 4
B_DK = D_B // B_HEADS
B_DV = D_B // B_HEADS
D_PROJ = 3 * D_A + 4 * D_B
ROPE_BASE = 10000.0
EPS = 1e-6
HALF = B_DK // 2
ROPE_SPLIT = 128

OFF_U, OFF_V, OFF_ZA = 0, D_A, 2 * D_A
OFF_Q, OFF_K, OFF_VB, OFF_ZB = 3 * D_A, 3 * D_A + D_B, 3 * D_A + 2 * D_B, 3 * D_A + 3 * D_B

RET_BLOCK = 256
PROMPT_TILE = 1024

SAMPLE_GROUP = 2
SAMPLE_GATE_GROUP = 4
W_TILE = 512
N_IN_TILES = D_PROJ // W_TILE
N_OUT_TILES = D_MIX // W_TILE
W_IN_STREAM_SLOTS = 4
W_OUT_STREAM_SLOTS = 3
STATE_STREAM_SLOTS = 3

VMEM_LIMIT_BYTES = 56 * 1024 * 1024


def _bf(x):
    return x.astype(jnp.bfloat16)


def _dot(a, b):
    return jnp.dot(a, b, preferred_element_type=jnp.float32)


def _dot_nt(a, b):
    return lax.dot_general(a, b, (((1,), (1,)), ((), ())), preferred_element_type=jnp.float32)


def _dot_tn(a, b):
    return lax.dot_general(a, b, (((0,), (0,)), ((), ())), preferred_element_type=jnp.float32)


def _rms(x):
    return x * lax.rsqrt(jnp.mean(x * x, axis=-1, keepdims=True) + EPS)


def _silu(z):
    return z * (1.0 / (1.0 + jnp.exp(-z)))


def _rotary(x, cos, sin):
    x1, x2 = x[:, :HALF], x[:, HALF:]
    return jnp.concatenate([x1 * cos - x2 * sin, x1 * sin + x2 * cos], axis=-1)


def _wide(t):
    return jnp.concatenate([t, t], axis=-1)


def _head_norm(v, gsgu):
    parts = [_rms(v[:, h * A_DH:(h + 1) * A_DH]) for h in range(v.shape[1] // A_DH)]
    return jnp.concatenate(parts, axis=-1) * gsgu


def _spatial_gate(vn, ws_ref, head0, bias):
    c = ws_ref.shape[-1]
    rows = []
    for r in range(vn.shape[0] // c):
        cols = [_dot(ws_ref[head0 + h], vn[r * c:(r + 1) * c, h * A_DH:(h + 1) * A_DH])
                for h in range(vn.shape[1] // A_DH)]
        rows.append(jnp.concatenate(cols, axis=-1) + bias)
    return jnp.concatenate(rows, axis=0)


def _retention(q, k, v, states, dec, rdec, cdec, sdec):
    n = len(states)
    seq = q.shape[0] // n
    qb, kb, vb = _bf(q), _bf(k), _bf(v)
    scores = _dot_nt(qb, kb) * dec
    intra = _dot(_bf(scores), vb)
    cross = [_dot(qb[i * seq:(i + 1) * seq], _bf(states[i])) for i in range(n)]
    cross = (cross[0] if n == 1 else jnp.concatenate(cross, axis=0)) * _wide(rdec)
    kw = _bf(k * _wide(cdec))
    new_states = [sdec * states[i] + _dot_tn(kw[i * seq:(i + 1) * seq], vb[i * seq:(i + 1) * seq])
                  for i in range(n)]
    return intra + cross, new_states


def _rope_rows(rope_hi_ref, rope_lo_ref, block):
    ch = rope_hi_ref[0, pl.ds(block, 1), :]
    sh = rope_hi_ref[1, pl.ds(block, 1), :]
    cl = rope_lo_ref[0]
    sl = rope_lo_ref[1]
    return ch * cl - sh * sl, sh * cl + ch * sl


def _prompt_layer_kernel(x_ref, rope_hi_ref, rope_lo_ref, gn_ref, win_ref, gsgu_ref, ws_ref, bs_ref,
                         dec_ref, rdec_ref, cdec_ref, sdec_ref, wout_ref, gfin_ref,
                         y_ref, s_ref, mix_ref, *, final_norm):
    tile = x_ref.shape[0]
    block0 = pl.program_id(0) * (tile // ROPE_SPLIT)

    @pl.when(pl.program_id(0) == 0)
    def _():
        s_ref[...] = jnp.zeros_like(s_ref)

    for blk in range(tile // RET_BLOCK):
        r = slice(blk * RET_BLOCK, (blk + 1) * RET_BLOCK)
        x = x_ref[r, :]
        h = _bf(_rms(x) * gn_ref[...])

        u = _dot(h, win_ref[:, OFF_U:OFF_U + D_A])
        v = _dot(h, win_ref[:, OFF_V:OFF_V + D_A])
        za = _dot(h, win_ref[:, OFF_ZA:OFF_ZA + D_A])
        vn = _bf(_head_norm(v, gsgu_ref[...]))
        mixed = _spatial_gate(vn, ws_ref, 0, bs_ref[...])
        mix_ref[r, 0:D_A] = _bf(u * mixed * _silu(za))

        parts = [_rope_rows(rope_hi_ref, rope_lo_ref, block0 + blk * (RET_BLOCK // ROPE_SPLIT) + j)
                 for j in range(RET_BLOCK // ROPE_SPLIT)]
        cos = jnp.concatenate([p[0] for p in parts], axis=0)
        sin = jnp.concatenate([p[1] for p in parts], axis=0)
        for hb in range(B_HEADS):
            c0 = hb * B_DK
            q = _rotary(_dot(h, win_ref[:, OFF_Q + c0:OFF_Q + c0 + B_DK]), cos, sin)
            k = _rotary(_dot(h, win_ref[:, OFF_K + c0:OFF_K + c0 + B_DK]), cos, sin) * (B_DK ** -0.5)
            vb = _dot(h, win_ref[:, OFF_VB + c0:OFF_VB + c0 + B_DV])
            zb = _dot(h, win_ref[:, OFF_ZB + c0:OFF_ZB + c0 + B_DV])
            o, (s_new,) = _retention(q, k, vb, [s_ref[hb]], dec_ref[hb], rdec_ref[hb], cdec_ref[hb],
                                     sdec_ref[hb])
            s_ref[hb] = s_new
            mix_ref[r, D_A + c0:D_A + c0 + B_DV] = _bf(_rms(o) * _silu(zb))

        y = x + _dot(mix_ref[r, :], wout_ref[...])
        if final_norm:
            y = _rms(y) * gfin_ref[...]
        y_ref[r, :] = y


class _TileStream:
    def __init__(self, hbm, stage, sems, sem_row, tiles_per_layer, tile_view):
        self.hbm, self.stage, self.sems, self.sem_row = hbm, stage, sems, sem_row
        self.tiles_per_layer, self.tile_view = tiles_per_layer, tile_view
        self.n_tiles = hbm.shape[0] * tiles_per_layer
        self.n_slots = stage.shape[0]
        self.lookahead = self.n_slots - 1

    def _copy(self, n):
        slot = n % self.n_slots
        src = self.tile_view(self.hbm, n // self.tiles_per_layer, n % self.tiles_per_layer)
        return pltpu.make_async_copy(src, self.stage.at[slot], self.sems.at[self.sem_row, slot])

    def prime(self):
        for n in range(self.lookahead):
            self._copy(n).start()

    def take(self, n):
        @pl.when(n + self.lookahead < self.n_tiles)
        def _():
            self._copy(n + self.lookahead).start()

        self._copy(n).wait()
        return self.stage.at[n % self.n_slots]


def _tile_start(t):
    return t * W_TILE if isinstance(t, int) else pl.multiple_of(t * W_TILE, W_TILE)


def _win_tile(hbm, layer, t):
    return hbm.at[layer, :, pl.ds(_tile_start(t), W_TILE)]


def _wout_tile(hbm, layer, t):
    return hbm.at[layer, pl.ds(_tile_start(t), W_TILE), :]


def _state_group(hbm, layer, g):
    return hbm.at[layer, pl.ds(g * SAMPLE_GROUP, SAMPLE_GROUP)]


def _sample_kernel(x_hbm, s_in_hbm, cos_ref, sin_ref, gn_ref, win_hbm, gsgu_ref, ws_ref, bs_ref,
                   dec_ref, rdec_ref, cdec_ref, sdec_ref, wout_hbm, gfin_ref,
                   y_ref, s_out_ref, vn_ref, win_bf_ref, wout_bf_ref,
                   h_ref, p_ref, mix_ref, win_stage, wout_stage, state_stage, sems, x_sem, *, seq):
    layer = pl.program_id(0)
    step = pl.program_id(1)
    group = SAMPLE_GROUP
    n_groups = y_ref.shape[0] // (group * seq)
    tiles_per_block = D_A // W_TILE
    win_stream = _TileStream(win_hbm, win_stage, sems, 0, N_IN_TILES, _win_tile)
    wout_stream = _TileStream(wout_hbm, wout_stage, sems, 1, N_OUT_TILES, _wout_tile)
    state_stream = _TileStream(s_in_hbm, state_stage, sems, 2, n_groups, _state_group)

    @pl.when((layer == 0) & (step == 0))
    def _():
        win_stream.prime()
        wout_stream.prime()
        state_stream.prime()
        x_copy = pltpu.make_async_copy(x_hbm, y_ref, x_sem)
        x_copy.start()
        x_copy.wait()

    @pl.when(step == 0)
    def _():
        h_ref[...] = _bf(_rms(y_ref[...]) * gn_ref[...])

    @pl.when(step < N_IN_TILES)
    def _():
        w = _bf(win_stream.take(layer * N_IN_TILES + step)[...])
        win_bf_ref[...] = w
        p_ref[step] = _dot(h_ref[...], w)

    for t in range(tiles_per_block):
        @pl.when(step == 2 * tiles_per_block + t)
        def _(t=t):
            cols = slice(t * W_TILE, (t + 1) * W_TILE)
            u = p_ref[t]
            v = p_ref[tiles_per_block + t]
            za = p_ref[2 * tiles_per_block + t]
            vn = _head_norm(v, gsgu_ref[:, cols])
            vn_ref[...] = vn
            mixed = _spatial_gate(_bf(vn), ws_ref, t * (W_TILE // A_DH), bs_ref[:, cols])
            mix_ref[:, cols] = _bf(u * mixed * _silu(za))

    @pl.when((step >= N_IN_TILES) & (step < N_IN_TILES + n_groups))
    def _():
        r0 = pl.multiple_of((step - N_IN_TILES) * (group * seq), group * seq)
        rows = pl.ds(r0, group * seq)
        cos = cos_ref[...]
        sin = sin_ref[...]
        s_in = state_stream.take(layer * n_groups + step - N_IN_TILES)
        heads_per_tile = W_TILE // B_DK
        for hb in range(B_HEADS):
            t, c0 = hb // heads_per_tile, (hb % heads_per_tile) * B_DK
            blocks = [p_ref[(3 + i) * tiles_per_block + t, rows, c0:c0 + B_DK] for i in range(4)]
            q = _rotary(blocks[0], cos, sin)
            k = _rotary(blocks[1], cos, sin) * (B_DK ** -0.5)
            states = [s_in[i, hb] for i in range(group)]
            o, new_states = _retention(q, k, blocks[2], states, dec_ref[hb], rdec_ref[hb], cdec_ref[hb],
                                       sdec_ref[hb])
            for i in range(group):
                s_out_ref[i, hb] = new_states[i]
            mix_ref[rows, D_A + hb * B_DV:D_A + (hb + 1) * B_DV] = _bf(_rms(o) * _silu(blocks[3]))

    @pl.when(step >= N_IN_TILES + n_groups)
    def _():
        w = _bf(wout_stream.take(layer * N_OUT_TILES + step - (N_IN_TILES + n_groups))[...])
        wout_bf_ref[...] = w
        for t in range(N_OUT_TILES):
            @pl.when(step == N_IN_TILES + n_groups + t)
            def _(t=t):
                y_ref[...] += _dot(mix_ref[:, t * W_TILE:(t + 1) * W_TILE], w)

    @pl.when((layer == pl.num_programs(0) - 1) & (step == pl.num_programs(1) - 1))
    def _():
        y_ref[...] = _rms(y_ref[...]) * gfin_ref[...]


def _const_spec(shape):
    zeros = (0,) * len(shape)
    return pl.BlockSpec(shape, lambda *_: zeros, pipeline_mode=pl.Buffered(1))


def _layer_spec(shape, layer):
    zeros = (0,) * len(shape)
    return pl.BlockSpec((None,) + shape, lambda i: (layer,) + zeros, pipeline_mode=pl.Buffered(1))


def _decay_tables(length, n_seq=1):
    lg = jnp.log(1.0 - 2.0 ** (-5.0 - jnp.arange(B_HEADS, dtype=jnp.float32)))
    idx = jnp.arange(length, dtype=jnp.float32)
    diff = idx[:, None] - idx[None, :]
    causal = diff >= 0
    dec = jnp.where(causal[None], jnp.exp(lg[:, None, None] * jnp.where(causal, diff, 0.0)[None]), 0.0)
    rdec = jnp.exp(lg[:, None] * (idx[None, :] + 1.0))
    cdec = jnp.exp(lg[:, None] * (length - 1.0 - idx[None, :]))
    sdec = jnp.exp(lg * length)
    dec = jnp.einsum("ab,hij->haibj", jnp.eye(n_seq, dtype=dec.dtype), dec)
    dec = dec.reshape(B_HEADS, n_seq * length, n_seq * length)
    rdec = jnp.broadcast_to(jnp.tile(rdec, (1, n_seq))[:, :, None], (B_HEADS, n_seq * length, 128))
    cdec = jnp.broadcast_to(jnp.tile(cdec, (1, n_seq))[:, :, None], (B_HEADS, n_seq * length, 128))
    sdec = jnp.broadcast_to(sdec[:, None, None], (B_HEADS, 1, B_DV))
    return dec, rdec, cdec, sdec


def _rope_tables(start, length):
    inv = ROPE_BASE ** (-jnp.arange(HALF, dtype=jnp.float32) / HALF)
    ang = (start + jnp.arange(length)).astype(jnp.float32)[:, None] * inv[None, :]
    return jnp.cos(ang), jnp.sin(ang)


def _rope_split_tables(length):
    assert length % ROPE_SPLIT == 0
    inv = ROPE_BASE ** (-jnp.arange(HALF, dtype=jnp.float32) / HALF)
    ang_hi = (ROPE_SPLIT * jnp.arange(length // ROPE_SPLIT)).astype(jnp.float32)[:, None] * inv[None, :]
    ang_lo = jnp.arange(ROPE_SPLIT).astype(jnp.float32)[:, None] * inv[None, :]
    return (jnp.stack([jnp.cos(ang_hi), jnp.sin(ang_hi)], axis=0),
            jnp.stack([jnp.cos(ang_lo), jnp.sin(ang_lo)], axis=0))


def _gate_weights(w_s, b_s, c, n_seq=1):
    i = jnp.arange(c)
    mask = (i[None, :] // CHUNK) <= (i[:, None] // CHUNK)
    w = jnp.where(mask, w_s[..., :c, :c], 0.0)
    w = jnp.einsum("ab,...ij->...aibj", jnp.eye(n_seq, dtype=w.dtype), w)
    w = _bf(w.reshape(w.shape[:-4] + (n_seq * c, n_seq * c)))
    bias = jnp.swapaxes(b_s[..., :c], -1, -2)
    bias = jnp.broadcast_to(bias[..., None], bias.shape + (A_DH,)).reshape(bias.shape[:-1] + (D_A,))
    bias = jnp.tile(bias, (1,) * (bias.ndim - 2) + (n_seq, 1))
    return w, bias


def _prompt_layer(layer, x, rope_hi, rope_lo, gn, win, gsgu, ws, bs, decs, wout, gfin, final_norm):
    seq = x.shape[0]
    tile = PROMPT_TILE
    dec, rdec, cdec, sdec = decs
    row = lambda i: (i, 0)
    kern = functools.partial(_prompt_layer_kernel, final_norm=final_norm)
    return pl.pallas_call(
        kern,
        grid=(seq // tile,),
        in_specs=[
            pl.BlockSpec((tile, D_MODEL), row),
            _const_spec(rope_hi.shape),
            _const_spec(rope_lo.shape),
            _const_spec((1, D_MODEL)),
            _layer_spec((D_MODEL, D_PROJ), layer),
            _const_spec((1, D_A)),
            _const_spec((A_HEADS, A_CHUNK, A_CHUNK)),
            _const_spec((A_CHUNK, D_A)),
            _const_spec((B_HEADS, RET_BLOCK, RET_BLOCK)),
            _const_spec((B_HEADS, RET_BLOCK, 128)),
            _const_spec((B_HEADS, RET_BLOCK, 128)),
            _const_spec((B_HEADS, 1, B_DV)),
            _layer_spec((D_MIX, D_MODEL), layer),
            _const_spec((1, D_MODEL)),
        ],
        out_specs=[
            pl.BlockSpec((tile, D_MODEL), row),
            pl.BlockSpec((B_HEADS, B_DK, B_DV), lambda i: (0, 0, 0)),
        ],
        out_shape=[
            jax.ShapeDtypeStruct((seq, D_MODEL), jnp.float32),
            jax.ShapeDtypeStruct((B_HEADS, B_DK, B_DV), jnp.float32),
        ],
        scratch_shapes=[pltpu.VMEM((tile, D_MIX), jnp.bfloat16)],
        compiler_params=pltpu.CompilerParams(
            dimension_semantics=("arbitrary",), vmem_limit_bytes=VMEM_LIMIT_BYTES),
        name="prompt_layer",
    )(x, rope_hi, rope_lo, gn, win, gsgu, ws, bs, dec, rdec, cdec, sdec, wout, gfin)


def _sample_trunk(x, state, cos, sin, gn, w_in, gsgu, ws, bs, decs, w_out, gfin, seq):
    rows = x.shape[0]
    nb = rows // seq
    group = SAMPLE_GROUP
    n_groups = nb // group
    n_steps = N_IN_TILES + n_groups + N_OUT_TILES
    dec, rdec, cdec, sdec = decs
    gate_rows = ws.shape[-1]
    tiles_per_block = D_A // W_TILE

    def per_layer(shape):
        zeros = (0,) * len(shape)
        return pl.BlockSpec((None,) + shape, lambda l, s: (l,) + zeros)

    def state_map(l, s):
        return (l, jnp.clip(s - N_IN_TILES, 0, n_groups - 1), 0, 0, 0)

    state_spec = pl.BlockSpec((None, group, B_HEADS, B_DK, B_DV), state_map)
    kern = functools.partial(_sample_kernel, seq=seq)
    return pl.pallas_call(
        kern,
        grid=(DEPTH, n_steps),
        in_specs=[
            pl.BlockSpec(memory_space=pl.ANY),
            pl.BlockSpec(memory_space=pl.ANY),
            _const_spec((group * seq, HALF)),
            _const_spec((group * seq, HALF)),
            per_layer((1, D_MODEL)),
            pl.BlockSpec(memory_space=pl.ANY),
            per_layer((1, D_A)),
            per_layer((A_HEADS, gate_rows, gate_rows)),
            per_layer((gate_rows, D_A)),
            _const_spec((B_HEADS, group * seq, group * seq)),
            _const_spec((B_HEADS, group * seq, 128)),
            _const_spec((B_HEADS, group * seq, 128)),
            _const_spec((B_HEADS, 1, B_DV)),
            pl.BlockSpec(memory_space=pl.ANY),
            _const_spec((1, D_MODEL)),
        ],
        out_specs=[
            pl.BlockSpec((rows, D_MODEL), lambda l, s: (0, 0)),
            state_spec,
            pl.BlockSpec((None, rows, W_TILE),
                         lambda l, s: (l, 0, jnp.clip(s - 2 * tiles_per_block, 0, tiles_per_block - 1))),
            pl.BlockSpec((None, D_MODEL, W_TILE), lambda l, s: (l, 0, jnp.minimum(s, N_IN_TILES - 1))),
            pl.BlockSpec((None, W_TILE, D_MODEL),
                         lambda l, s: (l, jnp.clip(s - N_IN_TILES - n_groups, 0, N_OUT_TILES - 1), 0)),
        ],
        out_shape=[
            jax.ShapeDtypeStruct((rows, D_MODEL), jnp.float32),
            jax.ShapeDtypeStruct(state.shape, jnp.float32),
            jax.ShapeDtypeStruct((DEPTH, rows, D_A), jnp.float32),
            jax.ShapeDtypeStruct(w_in.shape, jnp.bfloat16),
            jax.ShapeDtypeStruct(w_out.shape, jnp.bfloat16),
        ],
        scratch_shapes=[
            pltpu.VMEM((rows, D_MODEL), jnp.bfloat16),
            pltpu.VMEM((N_IN_TILES, rows, W_TILE), jnp.float32),
            pltpu.VMEM((rows, D_MIX), jnp.bfloat16),
            pltpu.VMEM((W_IN_STREAM_SLOTS, D_MODEL, W_TILE), jnp.float32),
            pltpu.VMEM((W_OUT_STREAM_SLOTS, W_TILE, D_MODEL), jnp.float32),
            pltpu.VMEM((STATE_STREAM_SLOTS, group, B_HEADS, B_DK, B_DV), jnp.float32),
            pltpu.SemaphoreType.DMA((3, max(W_IN_STREAM_SLOTS, W_OUT_STREAM_SLOTS, STATE_STREAM_SLOTS))),
            pltpu.SemaphoreType.DMA(()),
        ],
        compiler_params=pltpu.CompilerParams(
            dimension_semantics=("arbitrary", "arbitrary"), vmem_limit_bytes=VMEM_LIMIT_BYTES),
        name="sample_trunk",
    )(x, state, cos, sin, gn, w_in, gsgu, ws, bs, dec, rdec, cdec, sdec, w_out, gfin)


def kernel(x_prompt, x_sample, state_ret, g_norm, w_in, g_sgu, w_s, b_s, w_out, g_final):
    batch, seq, _ = x_prompt.shape
    dec_batch, dec_seq, _ = x_sample.shape
    assert batch == 1 and seq % PROMPT_TILE == 0
    assert PROMPT_TILE % RET_BLOCK == 0 and RET_BLOCK % A_CHUNK == 0 and RET_BLOCK % ROPE_SPLIT == 0
    assert dec_seq <= CHUNK and dec_batch % SAMPLE_GROUP == 0
    assert (dec_batch * dec_seq) % (SAMPLE_GATE_GROUP * dec_seq) == 0
    assert D_A % W_TILE == 0 and W_TILE % B_DK == 0 and W_TILE % A_DH == 0

    gfin = g_final.reshape(1, D_MODEL)
    gn = g_norm.reshape(DEPTH, 1, D_MODEL)
    gsgu = g_sgu.reshape(DEPTH, 1, D_A)

    cos_s, sin_s = _rope_tables(PAST_LEN, dec_seq)
    cos_s, sin_s = jnp.tile(cos_s, (SAMPLE_GROUP, 1)), jnp.tile(sin_s, (SAMPLE_GROUP, 1))
    ws_s, bs_s = _gate_weights(w_s, b_s, dec_seq, SAMPLE_GATE_GROUP)
    hs, new_ret_sample, vn, win_bf, wout_bf = _sample_trunk(
        x_sample.reshape(dec_batch * dec_seq, D_MODEL), state_ret, cos_s, sin_s, gn, w_in, gsgu, ws_s, bs_s,
        _decay_tables(dec_seq, SAMPLE_GROUP), w_out, gfin, dec_seq)

    rope_hi, rope_lo = _rope_split_tables(seq)
    decs_p = _decay_tables(RET_BLOCK)
    ws_p, bs_p = _gate_weights(w_s, b_s, A_CHUNK)
    hp = x_prompt.reshape(seq, D_MODEL)
    ret_p = []
    for l in range(DEPTH):
        hp, sp = _prompt_layer(l, hp, rope_hi, rope_lo, gn[l], win_bf, gsgu[l], ws_p[l], bs_p[l], decs_p, wout_bf,
                               gfin, l == DEPTH - 1)
        ret_p.append(sp.reshape(batch, B_HEADS, B_DK, B_DV))

    y_prompt = hp.reshape(batch, seq, D_MODEL)
    y_sample = hs.reshape(dec_batch, dec_seq, D_MODEL)
    new_chunk_v = vn.reshape(DEPTH, dec_batch, dec_seq, A_HEADS, A_DH)
    return (y_prompt, y_sample, jnp.stack(ret_p, axis=0), new_ret_sample, new_chunk_v)
```

```python
import functools

import jax
import jax.numpy as jnp
from jax import lax
from jax.experimental import pallas as pl
from jax.experimental.pallas import tpu as pltpu

D_MODEL = 1024
DEPTH = 4
PAST_LEN = 4096
CHUNK = 64
D_MIX = 2 * D_MODEL
D_A = D_MIX // 2
A_HEADS = 8
A_DH = D_A // A_HEADS
A_CHUNK = 128
D_B = D_MIX - D_A
B_HEADS = 4
B_DK = D_B // B_HEADS
B_DV = D_B // B_HEADS
D_PROJ = 3 * D_A + 4 * D_B
ROPE_BASE = 10000.0
EPS = 1e-6
HALF = B_DK // 2
ROPE_SPLIT = 128

OFF_U, OFF_V, OFF_ZA = 0, D_A, 2 * D_A
OFF_Q, OFF_K, OFF_VB, OFF_ZB = 3 * D_A, 3 * D_A + D_B, 3 * D_A + 2 * D_B, 3 * D_A + 3 * D_B

RET_BLOCK = 256
PROMPT_TILE = 512

SAMPLE_GROUP = 2
SAMPLE_GATE_GROUP = 4
W_TILE = 512
N_IN_TILES = D_PROJ // W_TILE
N_OUT_TILES = D_MIX // W_TILE
W_IN_STREAM_SLOTS = 4
W_OUT_STREAM_SLOTS = 3
STATE_STREAM_SLOTS = 3

VMEM_LIMIT_BYTES = 56 * 1024 * 1024


def _bf(x):
    return x.astype(jnp.bfloat16)


def _dot(a, b):
    return jnp.dot(a, b, preferred_element_type=jnp.float32)


def _dot_nt(a, b):
    return lax.dot_general(a, b, (((1,), (1,)), ((), ())), preferred_element_type=jnp.float32)


def _dot_tn(a, b):
    return lax.dot_general(a, b, (((0,), (0,)), ((), ())), preferred_element_type=jnp.float32)


def _rms(x):
    return x * lax.rsqrt(jnp.mean(x * x, axis=-1, keepdims=True) + EPS)


def _silu(z):
    return z * (1.0 / (1.0 + jnp.exp(-z)))


def _rotary(x, cos, sin):
    x1, x2 = x[:, :HALF], x[:, HALF:]
    return jnp.concatenate([x1 * cos - x2 * sin, x1 * sin + x2 * cos], axis=-1)


def _wide(t):
    return jnp.concatenate([t, t], axis=-1)


def _head_norm(v, gsgu):
    parts = [_rms(v[:, h * A_DH:(h + 1) * A_DH]) for h in range(v.shape[1] // A_DH)]
    return jnp.concatenate(parts, axis=-1) * gsgu


def _spatial_gate(vn, ws_ref, head0, bias):
    c = ws_ref.shape[-1]
    rows = []
    for r in range(vn.shape[0] // c):
        cols = [_dot(ws_ref[head0 + h], vn[r * c:(r + 1) * c, h * A_DH:(h + 1) * A_DH])
                for h in range(vn.shape[1] // A_DH)]
        rows.append(jnp.concatenate(cols, axis=-1) + bias)
    return jnp.concatenate(rows, axis=0)


def _retention(q, k, v, states, dec, rdec, cdec, sdec):
    n = len(states)
    seq = q.shape[0] // n
    qb, kb, vb = _bf(q), _bf(k), _bf(v)
    scores = _dot_nt(qb, kb) * dec
    intra = _dot(_bf(scores), vb)
    cross = [_dot(qb[i * seq:(i + 1) * seq], _bf(states[i])) for i in range(n)]
    cross = (cross[0] if n == 1 else jnp.concatenate(cross, axis=0)) * _wide(rdec)
    kw = _bf(k * _wide(cdec))
    new_states = [sdec * states[i] + _dot_tn(kw[i * seq:(i + 1) * seq], vb[i * seq:(i + 1) * seq])
                  for i in range(n)]
    return intra + cross, new_states


def _rope_rows(rope_hi_ref, rope_lo_ref, block):
    ch = rope_hi_ref[0, pl.ds(block, 1), :]
    sh = rope_hi_ref[1, pl.ds(block, 1), :]
    cl = rope_lo_ref[0]
    sl = rope_lo_ref[1]
    return ch * cl - sh * sl, sh * cl + ch * sl


def _prompt_layer_kernel(x_ref, rope_hi_ref, rope_lo_ref, gn_ref, win_ref, gsgu_ref, ws_ref, bs_ref,
                         dec_ref, rdec_ref, cdec_ref, sdec_ref, wout_ref, gfin_ref,
                         y_ref, s_ref, mix_ref, *, final_norm):
    tile = x_ref.shape[0]
    block0 = pl.program_id(0) * (tile // ROPE_SPLIT)

    @pl.when(pl.program_id(0) == 0)
    def _():
        s_ref[...] = jnp.zeros_like(s_ref)

    for blk in range(tile // RET_BLOCK):
        r = slice(blk * RET_BLOCK, (blk + 1) * RET_BLOCK)
        x = x_ref[r, :]
        h = _bf(_rms(x) * gn_ref[...])

        u = _dot(h, win_ref[:, OFF_U:OFF_U + D_A])
        v = _dot(h, win_ref[:, OFF_V:OFF_V + D_A])
        za = _dot(h, win_ref[:, OFF_ZA:OFF_ZA + D_A])
        vn = _bf(_head_norm(v, gsgu_ref[...]))
        mixed = _spatial_gate(vn, ws_ref, 0, bs_ref[...])
        mix_ref[r, 0:D_A] = _bf(u * mixed * _silu(za))

        parts = [_rope_rows(rope_hi_ref, rope_lo_ref, block0 + blk * (RET_BLOCK // ROPE_SPLIT) + j)
                 for j in range(RET_BLOCK // ROPE_SPLIT)]
        cos = jnp.concatenate([p[0] for p in parts], axis=0)
        sin = jnp.concatenate([p[1] for p in parts], axis=0)
        for hb in range(B_HEADS):
            c0 = hb * B_DK
            q = _rotary(_dot(h, win_ref[:, OFF_Q + c0:OFF_Q + c0 + B_DK]), cos, sin)
            k = _rotary(_dot(h, win_ref[:, OFF_K + c0:OFF_K + c0 + B_DK]), cos, sin) * (B_DK ** -0.5)
            vb = _dot(h, win_ref[:, OFF_VB + c0:OFF_VB + c0 + B_DV])
            zb = _dot(h, win_ref[:, OFF_ZB + c0:OFF_ZB + c0 + B_DV])
            o, (s_new,) = _retention(q, k, vb, [s_ref[hb]], dec_ref[hb], rdec_ref[hb], cdec_ref[hb],
                                     sdec_ref[hb])
            s_ref[hb] = s_new
            mix_ref[r, D_A + c0:D_A + c0 + B_DV] = _bf(_rms(o) * _silu(zb))

        y = x + _dot(mix_ref[r, :], wout_ref[...])
        if final_norm:
            y = _rms(y) * gfin_ref[...]
        y_ref[r, :] = y


class _TileStream:
    def __init__(self, hbm, stage, sems, sem_row, tiles_per_layer, tile_view):
        self.hbm, self.stage, self.sems, self.sem_row = hbm, stage, sems, sem_row
        self.tiles_per_layer, self.tile_view = tiles_per_layer, tile_view
        self.n_tiles = hbm.shape[0] * tiles_per_layer
        self.n_slots = stage.shape[0]
        self.lookahead = self.n_slots - 1

    def _copy(self, n):
        slot = n % self.n_slots
        src = self.tile_view(self.hbm, n // self.tiles_per_layer, n % self.tiles_per_layer)
        return pltpu.make_async_copy(src, self.stage.at[slot], self.sems.at[self.sem_row, slot])

    def prime(self):
        for n in range(self.lookahead):
            self._copy(n).start()

    def take(self, n):
        @pl.when(n + self.lookahead < self.n_tiles)
        def _():
            self._copy(n + self.lookahead).start()

        self._copy(n).wait()
        return self.stage.at[n % self.n_slots]


def _tile_start(t):
    return t * W_TILE if isinstance(t, int) else pl.multiple_of(t * W_TILE, W_TILE)


def _win_tile(hbm, layer, t):
    return hbm.at[layer, :, pl.ds(_tile_start(t), W_TILE)]


def _wout_tile(hbm, layer, t):
    return hbm.at[layer, pl.ds(_tile_start(t), W_TILE), :]


def _state_group(hbm, layer, g):
    return hbm.at[layer, pl.ds(g * SAMPLE_GROUP, SAMPLE_GROUP)]


def _sample_kernel(x_hbm, s_in_hbm, cos_ref, sin_ref, gn_ref, win_hbm, gsgu_ref, ws_ref, bs_ref,
                   dec_ref, rdec_ref, cdec_ref, sdec_ref, wout_hbm, gfin_ref,
                   y_ref, s_out_ref, vn_ref, win_bf_ref, wout_bf_ref,
                   h_ref, p_ref, mix_ref, win_stage, wout_stage, state_stage, sems, x_sem, *, seq):
    layer = pl.program_id(0)
    step = pl.program_id(1)
    group = SAMPLE_GROUP
    n_groups = y_ref.shape[0] // (group * seq)
    tiles_per_block = D_A // W_TILE
    win_stream = _TileStream(win_hbm, win_stage, sems, 0, N_IN_TILES, _win_tile)
    wout_stream = _TileStream(wout_hbm, wout_stage, sems, 1, N_OUT_TILES, _wout_tile)
    state_stream = _TileStream(s_in_hbm, state_stage, sems, 2, n_groups, _state_group)

    @pl.when((layer == 0) & (step == 0))
    def _():
        win_stream.prime()
        wout_stream.prime()
        state_stream.prime()
        x_copy = pltpu.make_async_copy(x_hbm, y_ref, x_sem)
        x_copy.start()
        x_copy.wait()

    @pl.when(step == 0)
    def _():
        h_ref[...] = _bf(_rms(y_ref[...]) * gn_ref[...])

    @pl.when(step < N_IN_TILES)
    def _():
        w = _bf(win_stream.take(layer * N_IN_TILES + step)[...])
        win_bf_ref[...] = w
        p_ref[step] = _dot(h_ref[...], w)

    for t in range(tiles_per_block):
        @pl.when(step == 2 * tiles_per_block + t)
        def _(t=t):
            cols = slice(t * W_TILE, (t + 1) * W_TILE)
            u = p_ref[t]
            v = p_ref[tiles_per_block + t]
            za = p_ref[2 * tiles_per_block + t]
            vn = _head_norm(v, gsgu_ref[:, cols])
            vn_ref[...] = vn
            mixed = _spatial_gate(_bf(vn), ws_ref, t * (W_TILE // A_DH), bs_ref[:, cols])
            mix_ref[:, cols] = _bf(u * mixed * _silu(za))

    @pl.when((step >= N_IN_TILES) & (step < N_IN_TILES + n_groups))
    def _():
        r0 = pl.multiple_of((step - N_IN_TILES) * (group * seq), group * seq)
        rows = pl.ds(r0, group * seq)
        cos = cos_ref[...]
        sin = sin_ref[...]
        s_in = state_stream.take(layer * n_groups + step - N_IN_TILES)
        heads_per_tile = W_TILE // B_DK
        for hb in range(B_HEADS):
            t, c0 = hb // heads_per_tile, (hb % heads_per_tile) * B_DK
            blocks = [p_ref[(3 + i) * tiles_per_block + t, rows, c0:c0 + B_DK] for i in range(4)]
            q = _rotary(blocks[0], cos, sin)
            k = _rotary(blocks[1], cos, sin) * (B_DK ** -0.5)
            states = [s_in[i, hb] for i in range(group)]
            o, new_states = _retention(q, k, blocks[2], states, dec_ref[hb], rdec_ref[hb], cdec_ref[hb],
                                       sdec_ref[hb])
            for i in range(group):
                s_out_ref[i, hb] = new_states[i]
            mix_ref[rows, D_A + hb * B_DV:D_A + (hb + 1) * B_DV] = _bf(_rms(o) * _silu(blocks[3]))

    @pl.when(step >= N_IN_TILES + n_groups)
    def _():
        w = _bf(wout_stream.take(layer * N_OUT_TILES + step - (N_IN_TILES + n_groups))[...])
        wout_bf_ref[...] = w
        for t in range(N_OUT_TILES):
            @pl.when(step == N_IN_TILES + n_groups + t)
            def _(t=t):
                y_ref[...] += _dot(mix_ref[:, t * W_TILE:(t + 1) * W_TILE], w)

    @pl.when((layer == pl.num_programs(0) - 1) & (step == pl.num_programs(1) - 1))
    def _():
        y_ref[...] = _rms(y_ref[...]) * gfin_ref[...]


def _const_spec(shape):
    zeros = (0,) * len(shape)
    return pl.BlockSpec(shape, lambda *_: zeros, pipeline_mode=pl.Buffered(1))


def _layer_spec(shape, layer):
    zeros = (0,) * len(shape)
    return pl.BlockSpec((None,) + shape, lambda i: (layer,) + zeros, pipeline_mode=pl.Buffered(1))


def _decay_tables(length, n_seq=1):
    lg = jnp.log(1.0 - 2.0 ** (-5.0 - jnp.arange(B_HEADS, dtype=jnp.float32)))
    idx = jnp.arange(length, dtype=jnp.float32)
    diff = idx[:, None] - idx[None, :]
    causal = diff >= 0
    dec = jnp.where(causal[None], jnp.exp(lg[:, None, None] * jnp.where(causal, diff, 0.0)[None]), 0.0)
    rdec = jnp.exp(lg[:, None] * (idx[None, :] + 1.0))
    cdec = jnp.exp(lg[:, None] * (length - 1.0 - idx[None, :]))
    sdec = jnp.exp(lg * length)
    dec = jnp.einsum("ab,hij->haibj", jnp.eye(n_seq, dtype=dec.dtype), dec)
    dec = dec.reshape(B_HEADS, n_seq * length, n_seq * length)
    rdec = jnp.broadcast_to(jnp.tile(rdec, (1, n_seq))[:, :, None], (B_HEADS, n_seq * length, 128))
    cdec = jnp.broadcast_to(jnp.tile(cdec, (1, n_seq))[:, :, None], (B_HEADS, n_seq * length, 128))
    sdec = jnp.broadcast_to(sdec[:, None, None], (B_HEADS, 1, B_DV))
    return dec, rdec, cdec, sdec


def _rope_tables(start, length):
    inv = ROPE_BASE ** (-jnp.arange(HALF, dtype=jnp.float32) / HALF)
    ang = (start + jnp.arange(length)).astype(jnp.float32)[:, None] * inv[None, :]
    return jnp.cos(ang), jnp.sin(ang)


def _rope_split_tables(length):
    assert length % ROPE_SPLIT == 0
    inv = ROPE_BASE ** (-jnp.arange(HALF, dtype=jnp.float32) / HALF)
    ang_hi = (ROPE_SPLIT * jnp.arange(length // ROPE_SPLIT)).astype(jnp.float32)[:, None] * inv[None, :]
    ang_lo = jnp.arange(ROPE_SPLIT).astype(jnp.float32)[:, None] * inv[None, :]
    return (jnp.stack([jnp.cos(ang_hi), jnp.sin(ang_hi)], axis=0),
            jnp.stack([jnp.cos(ang_lo), jnp.sin(ang_lo)], axis=0))


def _gate_weights(w_s, b_s, c, n_seq=1):
    i = jnp.arange(c)
    mask = (i[None, :] // CHUNK) <= (i[:, None] // CHUNK)
    w = jnp.where(mask, w_s[..., :c, :c], 0.0)
    w = jnp.einsum("ab,...ij->...aibj", jnp.eye(n_seq, dtype=w.dtype), w)
    w = _bf(w.reshape(w.shape[:-4] + (n_seq * c, n_seq * c)))
    bias = jnp.swapaxes(b_s[..., :c], -1, -2)
    bias = jnp.broadcast_to(bias[..., None], bias.shape + (A_DH,)).reshape(bias.shape[:-1] + (D_A,))
    bias = jnp.tile(bias, (1,) * (bias.ndim - 2) + (n_seq, 1))
    return w, bias


def _prompt_layer(layer, x, rope_hi, rope_lo, gn, win, gsgu, ws, bs, decs, wout, gfin, final_norm, in_place):
    seq = x.shape[0]
    tile = PROMPT_TILE
    dec, rdec, cdec, sdec = decs
    row = lambda i: (i, 0)
    kern = functools.partial(_prompt_layer_kernel, final_norm=final_norm)
    return pl.pallas_call(
        kern,
        grid=(seq // tile,),
        in_specs=[
            pl.BlockSpec((tile, D_MODEL), row),
            _const_spec(rope_hi.shape),
            _const_spec(rope_lo.shape),
            _const_spec((1, D_MODEL)),
            _layer_spec((D_MODEL, D_PROJ), layer),
            _const_spec((1, D_A)),
            _const_spec((A_HEADS, A_CHUNK, A_CHUNK)),
            _const_spec((A_CHUNK, D_A)),
            _const_spec((B_HEADS, RET_BLOCK, RET_BLOCK)),
            _const_spec((B_HEADS, RET_BLOCK, 128)),
            _const_spec((B_HEADS, RET_BLOCK, 128)),
            _const_spec((B_HEADS, 1, B_DV)),
            _layer_spec((D_MIX, D_MODEL), layer),
            _const_spec((1, D_MODEL)),
        ],
        out_specs=[
            pl.BlockSpec((tile, D_MODEL), row),
            pl.BlockSpec((B_HEADS, B_DK, B_DV), lambda i: (0, 0, 0)),
        ],
        out_shape=[
            jax.ShapeDtypeStruct((seq, D_MODEL), jnp.float32),
            jax.ShapeDtypeStruct((B_HEADS, B_DK, B_DV), jnp.float32),
        ],
        scratch_shapes=[pltpu.VMEM((tile, D_MIX), jnp.bfloat16)],
        input_output_aliases={0: 0} if in_place else {},
        compiler_params=pltpu.CompilerParams(
            dimension_semantics=("arbitrary",), vmem_limit_bytes=VMEM_LIMIT_BYTES),
        name="prompt_layer",
    )(x, rope_hi, rope_lo, gn, win, gsgu, ws, bs, dec, rdec, cdec, sdec, wout, gfin)


def _sample_trunk(x, state, cos, sin, gn, w_in, gsgu, ws, bs, decs, w_out, gfin, seq):
    rows = x.shape[0]
    nb = rows // seq
    group = SAMPLE_GROUP
    n_groups = nb // group
    n_steps = N_IN_TILES + n_groups + N_OUT_TILES
    dec, rdec, cdec, sdec = decs
    gate_rows = ws.shape[-1]
    tiles_per_block = D_A // W_TILE

    def per_layer(shape):
        zeros = (0,) * len(shape)
        return pl.BlockSpec((None,) + shape, lambda l, s: (l,) + zeros)

    def state_map(l, s):
        return (l, jnp.clip(s - N_IN_TILES, 0, n_groups - 1), 0, 0, 0)

    state_spec = pl.BlockSpec((None, group, B_HEADS, B_DK, B_DV), state_map)
    kern = functools.partial(_sample_kernel, seq=seq)
    return pl.pallas_call(
        kern,
        grid=(DEPTH, n_steps),
        in_specs=[
            pl.BlockSpec(memory_space=pl.ANY),
            pl.BlockSpec(memory_space=pl.ANY),
            _const_spec((group * seq, HALF)),
            _const_spec((group * seq, HALF)),
            per_layer((1, D_MODEL)),
            pl.BlockSpec(memory_space=pl.ANY),
            per_layer((1, D_A)),
            per_layer((A_HEADS, gate_rows, gate_rows)),
            per_layer((gate_rows, D_A)),
            _const_spec((B_HEADS, group * seq, group * seq)),
            _const_spec((B_HEADS, group * seq, 128)),
            _const_spec((B_HEADS, group * seq, 128)),
            _const_spec((B_HEADS, 1, B_DV)),
            pl.BlockSpec(memory_space=pl.ANY),
            _const_spec((1, D_MODEL)),
        ],
        out_specs=[
            pl.BlockSpec((rows, D_MODEL), lambda l, s: (0, 0)),
            state_spec,
            pl.BlockSpec((None, rows, W_TILE),
                         lambda l, s: (l, 0, jnp.clip(s - 2 * tiles_per_block, 0, tiles_per_block - 1))),
            pl.BlockSpec((None, D_MODEL, W_TILE), lambda l, s: (l, 0, jnp.minimum(s, N_IN_TILES - 1))),
            pl.BlockSpec((None, W_TILE, D_MODEL),
                         lambda l, s: (l, jnp.clip(s - N_IN_TILES - n_groups, 0, N_OUT_TILES - 1), 0)),
        ],
        out_shape=[
            jax.ShapeDtypeStruct((rows, D_MODEL), jnp.float32),
            jax.ShapeDtypeStruct(state.shape, jnp.float32),
            jax.ShapeDtypeStruct((DEPTH, rows, D_A), jnp.float32),
            jax.ShapeDtypeStruct(w_in.shape, jnp.bfloat16),
            jax.ShapeDtypeStruct(w_out.shape, jnp.bfloat16),
        ],
        scratch_shapes=[
            pltpu.VMEM((rows, D_MODEL), jnp.bfloat16),
            pltpu.VMEM((N_IN_TILES, rows, W_TILE), jnp.float32),
            pltpu.VMEM((rows, D_MIX), jnp.bfloat16),
            pltpu.VMEM((W_IN_STREAM_SLOTS, D_MODEL, W_TILE), jnp.float32),
            pltpu.VMEM((W_OUT_STREAM_SLOTS, W_TILE, D_MODEL), jnp.float32),
            pltpu.VMEM((STATE_STREAM_SLOTS, group, B_HEADS, B_DK, B_DV), jnp.float32),
            pltpu.SemaphoreType.DMA((3, max(W_IN_STREAM_SLOTS, W_OUT_STREAM_SLOTS, STATE_STREAM_SLOTS))),
            pltpu.SemaphoreType.DMA(()),
        ],
        compiler_params=pltpu.CompilerParams(
            dimension_semantics=("arbitrary", "arbitrary"), vmem_limit_bytes=VMEM_LIMIT_BYTES),
        name="sample_trunk",
    )(x, state, cos, sin, gn, w_in, gsgu, ws, bs, dec, rdec, cdec, sdec, w_out, gfin)


def kernel(x_prompt, x_sample, state_ret, g_norm, w_in, g_sgu, w_s, b_s, w_out, g_final):
    batch, seq, _ = x_prompt.shape
    dec_batch, dec_seq, _ = x_sample.shape
    assert batch == 1 and seq % PROMPT_TILE == 0
    assert PROMPT_TILE % RET_BLOCK == 0 and RET_BLOCK % A_CHUNK == 0 and RET_BLOCK % ROPE_SPLIT == 0
    assert dec_seq <= CHUNK and dec_batch % SAMPLE_GROUP == 0
    assert (dec_batch * dec_seq) % (SAMPLE_GATE_GROUP * dec_seq) == 0
    assert D_A % W_TILE == 0 and W_TILE % B_DK == 0 and W_TILE % A_DH == 0

    gfin = g_final.reshape(1, D_MODEL)
    gn = g_norm.reshape(DEPTH, 1, D_MODEL)
    gsgu = g_sgu.reshape(DEPTH, 1, D_A)

    cos_s, sin_s = _rope_tables(PAST_LEN, dec_seq)
    cos_s, sin_s = jnp.tile(cos_s, (SAMPLE_GROUP, 1)), jnp.tile(sin_s, (SAMPLE_GROUP, 1))
    ws_s, bs_s = _gate_weights(w_s, b_s, dec_seq, SAMPLE_GATE_GROUP)
    hs, new_ret_sample, vn, win_bf, wout_bf = _sample_trunk(
        x_sample.reshape(dec_batch * dec_seq, D_MODEL), state_ret, cos_s, sin_s, gn, w_in, gsgu, ws_s, bs_s,
        _decay_tables(dec_seq, SAMPLE_GROUP), w_out, gfin, dec_seq)

    rope_hi, rope_lo = _rope_split_tables(seq)
    decs_p = _decay_tables(RET_BLOCK)
    ws_p, bs_p = _gate_weights(w_s, b_s, A_CHUNK)
    hp = x_prompt.reshape(seq, D_MODEL)
    ret_p = []
    for l in range(DEPTH):
        hp, sp = _prompt_layer(l, hp, rope_hi, rope_lo, gn[l], win_bf, gsgu[l], ws_p[l], bs_p[l], decs_p, wout_bf,
                               gfin, l == DEPTH - 1, l > 0)
        ret_p.append(sp.reshape(batch, B_HEADS, B_DK, B_DV))

    y_prompt = hp.reshape(batch, seq, D_MODEL)
    y_sample = hs.reshape(dec_batch, dec_seq, D_MODEL)
    new_chunk_v = vn.reshape(DEPTH, dec_batch, dec_seq, A_HEADS, A_DH)
    return (y_prompt, y_sample, jnp.stack(ret_p, axis=0), new_ret_sample, new_chunk_v)
```

```python
import functools

import jax
import jax.numpy as jnp
from jax import lax
from jax.experimental import pallas as pl
from jax.experimental.pallas import tpu as pltpu

D_MODEL = 1024
DEPTH = 4
PAST_LEN = 4096
CHUNK = 64
D_MIX = 2 * D_MODEL
D_A = D_MIX // 2
A_HEADS = 8
A_DH = D_A // A_HEADS
A_CHUNK = 128
D_B = D_MIX - D_A
B_HEADS = 4
B_DK = D_B // B_HEADS
B_DV = D_B // B_HEADS
D_PROJ = 3 * D_A + 4 * D_B
ROPE_BASE = 10000.0
EPS = 1e-6
HALF = B_DK // 2
ROPE_SPLIT = 128

OFF_U, OFF_V, OFF_ZA = 0, D_A, 2 * D_A
OFF_Q, OFF_K, OFF_VB, OFF_ZB = 3 * D_A, 3 * D_A + D_B, 3 * D_A + 2 * D_B, 3 * D_A + 3 * D_B

RET_BLOCK = 256
PROMPT_TILE = 1024

SAMPLE_GROUP = 2
SAMPLE_GATE_GROUP = 4
W_TILE = 512
N_IN_TILES = D_PROJ // W_TILE
N_OUT_TILES = D_MIX // W_TILE
W_IN_STREAM_SLOTS = 4
W_OUT_STREAM_SLOTS = 3
STATE_STREAM_SLOTS = 3

VMEM_LIMIT_BYTES = 56 * 1024 * 1024


def _bf(x):
    return x.astype(jnp.bfloat16)


def _dot(a, b):
    return jnp.dot(a, b, preferred_element_type=jnp.float32)


def _dot_nt(a, b):
    return lax.dot_general(a, b, (((1,), (1,)), ((), ())), preferred_element_type=jnp.float32)


def _dot_tn(a, b):
    return lax.dot_general(a, b, (((0,), (0,)), ((), ())), preferred_element_type=jnp.float32)


def _rms(x):
    return x * lax.rsqrt(jnp.mean(x * x, axis=-1, keepdims=True) + EPS)


def _silu(z):
    return z * (1.0 / (1.0 + jnp.exp(-z)))


def _rotary(x, cos, sin):
    x1, x2 = x[:, :HALF], x[:, HALF:]
    return jnp.concatenate([x1 * cos - x2 * sin, x1 * sin + x2 * cos], axis=-1)


def _wide(t):
    return jnp.concatenate([t, t], axis=-1)


def _head_norm(v, gsgu):
    parts = [_rms(v[:, h * A_DH:(h + 1) * A_DH]) for h in range(v.shape[1] // A_DH)]
    return jnp.concatenate(parts, axis=-1) * gsgu


def _spatial_gate(vn, ws_ref, head0, bias):
    c = ws_ref.shape[-1]
    rows = []
    for r in range(vn.shape[0] // c):
        cols = [_dot(ws_ref[head0 + h], vn[r * c:(r + 1) * c, h * A_DH:(h + 1) * A_DH])
                for h in range(vn.shape[1] // A_DH)]
        rows.append(jnp.concatenate(cols, axis=-1) + bias)
    return jnp.concatenate(rows, axis=0)


def _retention(q, k, v, states, dec, rdec, cdec, sdec):
    n = len(states)
    seq = q.shape[0] // n
    qb, kb, vb = _bf(q), _bf(k), _bf(v)
    scores = _dot_nt(qb, kb) * dec
    intra = _dot(_bf(scores), vb)
    cross = [_dot(qb[i * seq:(i + 1) * seq], _bf(states[i])) for i in range(n)]
    cross = (cross[0] if n == 1 else jnp.concatenate(cross, axis=0)) * _wide(rdec)
    kw = _bf(k * _wide(cdec))
    new_states = [sdec * states[i] + _dot_tn(kw[i * seq:(i + 1) * seq], vb[i * seq:(i + 1) * seq])
                  for i in range(n)]
    return intra + cross, new_states


def _rope_rows(rope_hi_ref, rope_lo_ref, block):
    ch = rope_hi_ref[0, pl.ds(block, 1), :]
    sh = rope_hi_ref[1, pl.ds(block, 1), :]
    cl = rope_lo_ref[0]
    sl = rope_lo_ref[1]
    return ch * cl - sh * sl, sh * cl + ch * sl


def _prompt_layer_kernel(x_ref, rope_hi_ref, rope_lo_ref, gn_ref, win_ref, gsgu_ref, ws_ref, bs_ref,
                         dec_ref, rdec_ref, cdec_ref, sdec_ref, wout_ref, gfin_ref,
                         y_ref, s_ref, mix_ref, *, final_norm):
    tile = x_ref.shape[0]
    block0 = pl.program_id(0) * (tile // ROPE_SPLIT)

    @pl.when(pl.program_id(0) == 0)
    def _():
        s_ref[...] = jnp.zeros_like(s_ref)

    for blk in range(tile // RET_BLOCK):
        r = slice(blk * RET_BLOCK, (blk + 1) * RET_BLOCK)
        x = x_ref[r, :]
        h = _bf(_rms(x) * gn_ref[...])

        u = _dot(h, win_ref[:, OFF_U:OFF_U + D_A])
        v = _dot(h, win_ref[:, OFF_V:OFF_V + D_A])
        za = _dot(h, win_ref[:, OFF_ZA:OFF_ZA + D_A])
        vn = _bf(_head_norm(v, gsgu_ref[...]))
        mixed = _spatial_gate(vn, ws_ref, 0, bs_ref[...])
        mix_ref[r, 0:D_A] = _bf(u * mixed * _silu(za))

        parts = [_rope_rows(rope_hi_ref, rope_lo_ref, block0 + blk * (RET_BLOCK // ROPE_SPLIT) + j)
                 for j in range(RET_BLOCK // ROPE_SPLIT)]
        cos = jnp.concatenate([p[0] for p in parts], axis=0)
        sin = jnp.concatenate([p[1] for p in parts], axis=0)
        for hb in range(B_HEADS):
            c0 = hb * B_DK
            q = _rotary(_dot(h, win_ref[:, OFF_Q + c0:OFF_Q + c0 + B_DK]), cos, sin)
            k = _rotary(_dot(h, win_ref[:, OFF_K + c0:OFF_K + c0 + B_DK]), cos, sin) * (B_DK ** -0.5)
            vb = _dot(h, win_ref[:, OFF_VB + c0:OFF_VB + c0 + B_DV])
            zb = _dot(h, win_ref[:, OFF_ZB + c0:OFF_ZB + c0 + B_DV])
            o, (s_new,) = _retention(q, k, vb, [s_ref[hb]], dec_ref[hb], rdec_ref[hb], cdec_ref[hb],
                                     sdec_ref[hb])
            s_ref[hb] = s_new
            mix_ref[r, D_A + c0:D_A + c0 + B_DV] = _bf(_rms(o) * _silu(zb))

        y = x + _dot(mix_ref[r, :], wout_ref[...])
        if final_norm:
            y = _rms(y) * gfin_ref[...]
        y_ref[r, :] = y


class _TileStream:
    def __init__(self, hbm, stage, sems, sem_row, tiles_per_layer, tile_view):
        self.hbm, self.stage, self.sems, self.sem_row = hbm, stage, sems, sem_row
        self.tiles_per_layer, self.tile_view = tiles_per_layer, tile_view
        self.n_tiles = hbm.shape[0] * tiles_per_layer
        self.n_slots = stage.shape[0]
        self.lookahead = self.n_slots - 1

    def _copy(self, n):
        slot = n % self.n_slots
        src = self.tile_view(self.hbm, n // self.tiles_per_layer, n % self.tiles_per_layer)
        return pltpu.make_async_copy(src, self.stage.at[slot], self.sems.at[self.sem_row, slot])

    def prime(self):
        for n in range(self.lookahead):
            self._copy(n).start()

    def take(self, n):
        @pl.when(n + self.lookahead < self.n_tiles)
        def _():
            self._copy(n + self.lookahead).start()

        self._copy(n).wait()
        return self.stage.at[n % self.n_slots]


def _tile_start(t):
    return t * W_TILE if isinstance(t, int) else pl.multiple_of(t * W_TILE, W_TILE)


def _win_tile(hbm, layer, t):
    return hbm.at[layer, :, pl.ds(_tile_start(t), W_TILE)]


def _wout_tile(hbm, layer, t):
    return hbm.at[layer, pl.ds(_tile_start(t), W_TILE), :]


def _state_group(hbm, layer, g):
    return hbm.at[layer, pl.ds(g * SAMPLE_GROUP, SAMPLE_GROUP)]


def _sample_kernel(x_hbm, s_in_hbm, cos_ref, sin_ref, gn_ref, win_hbm, gsgu_ref, ws_ref, bs_ref,
                   dec_ref, rdec_ref, cdec_ref, sdec_ref, wout_hbm, gfin_ref,
                   y_ref, s_out_ref, vn_ref, win_bf_ref, wout_bf_ref,
                   h_ref, p_ref, mix_ref, win_stage, wout_stage, state_stage, sems, x_sem, *, seq):
    layer = pl.program_id(0)
    step = pl.program_id(1)
    group = SAMPLE_GROUP
    n_groups = y_ref.shape[0] // (group * seq)
    tiles_per_block = D_A // W_TILE
    win_stream = _TileStream(win_hbm, win_stage, sems, 0, N_IN_TILES, _win_tile)
    wout_stream = _TileStream(wout_hbm, wout_stage, sems, 1, N_OUT_TILES, _wout_tile)
    state_stream = _TileStream(s_in_hbm, state_stage, sems, 2, n_groups, _state_group)

    @pl.when((layer == 0) & (step == 0))
    def _():
        win_stream.prime()
        wout_stream.prime()
        state_stream.prime()
        x_copy = pltpu.make_async_copy(x_hbm, y_ref, x_sem)
        x_copy.start()
        x_copy.wait()

    @pl.when(step == 0)
    def _():
        h_ref[...] = _bf(_rms(y_ref[...]) * gn_ref[...])

    @pl.when(step < N_IN_TILES)
    def _():
        w = _bf(win_stream.take(layer * N_IN_TILES + step)[...])
        win_bf_ref[...] = w
        p_ref[step] = _dot(h_ref[...], w)

    for t in range(tiles_per_block):
        @pl.when(step == 2 * tiles_per_block + t)
        def _(t=t):
            cols = slice(t * W_TILE, (t + 1) * W_TILE)
            u = p_ref[t]
            v = p_ref[tiles_per_block + t]
            za = p_ref[2 * tiles_per_block + t]
            vn = _head_norm(v, gsgu_ref[:, cols])
            vn_ref[...] = vn
            mixed = _spatial_gate(_bf(vn), ws_ref, t * (W_TILE // A_DH), bs_ref[:, cols])
            mix_ref[:, cols] = _bf(u * mixed * _silu(za))

    @pl.when((step >= N_IN_TILES) & (step < N_IN_TILES + n_groups))
    def _():
        r0 = pl.multiple_of((step - N_IN_TILES) * (group * seq), group * seq)
        rows = pl.ds(r0, group * seq)
        cos = cos_ref[...]
        sin = sin_ref[...]
        s_in = state_stream.take(layer * n_groups + step - N_IN_TILES)
        heads_per_tile = W_TILE // B_DK
        for hb in range(B_HEADS):
            t, c0 = hb // heads_per_tile, (hb % heads_per_tile) * B_DK
            blocks = [p_ref[(3 + i) * tiles_per_block + t, rows, c0:c0 + B_DK] for i in range(4)]
            q = _rotary(blocks[0], cos, sin)
            k = _rotary(blocks[1], cos, sin) * (B_DK ** -0.5)
            states = [s_in[i, hb] for i in range(group)]
            o, new_states = _retention(q, k, blocks[2], states, dec_ref[hb], rdec_ref[hb], cdec_ref[hb],
                                       sdec_ref[hb])
            for i in range(group):
                s_out_ref[i, hb] = new_states[i]
            mix_ref[rows, D_A + hb * B_DV:D_A + (hb + 1) * B_DV] = _bf(_rms(o) * _silu(blocks[3]))

    @pl.when(step >= N_IN_TILES + n_groups)
    def _():
        w = _bf(wout_stream.take(layer * N_OUT_TILES + step - (N_IN_TILES + n_groups))[...])
        wout_bf_ref[...] = w
        for t in range(N_OUT_TILES):
            @pl.when(step == N_IN_TILES + n_groups + t)
            def _(t=t):
                y_ref[...] += _dot(mix_ref[:, t * W_TILE:(t + 1) * W_TILE], w)

    @pl.when((layer == pl.num_programs(0) - 1) & (step == pl.num_programs(1) - 1))
    def _():
        y_ref[...] = _rms(y_ref[...]) * gfin_ref[...]


def _const_spec(shape):
    zeros = (0,) * len(shape)
    return pl.BlockSpec(shape, lambda *_: zeros, pipeline_mode=pl.Buffered(1))


def _layer_spec(shape, layer):
    zeros = (0,) * len(shape)
    return pl.BlockSpec((None,) + shape, lambda i: (layer,) + zeros, pipeline_mode=pl.Buffered(1))


def _decay_tables(length, n_seq=1):
    lg = jnp.log(1.0 - 2.0 ** (-5.0 - jnp.arange(B_HEADS, dtype=jnp.float32)))
    idx = jnp.arange(length, dtype=jnp.float32)
    diff = idx[:, None] - idx[None, :]
    causal = diff >= 0
    dec = jnp.where(causal[None], jnp.exp(lg[:, None, None] * jnp.where(causal, diff, 0.0)[None]), 0.0)
    rdec = jnp.exp(lg[:, None] * (idx[None, :] + 1.0))
    cdec = jnp.exp(lg[:, None] * (length - 1.0 - idx[None, :]))
    sdec = jnp.exp(lg * length)
    dec = jnp.einsum("ab,hij->haibj", jnp.eye(n_seq, dtype=dec.dtype), dec)
    dec = dec.reshape(B_HEADS, n_seq * length, n_seq * length)
    rdec = jnp.broadcast_to(jnp.tile(rdec, (1, n_seq))[:, :, None], (B_HEADS, n_seq * length, 128))
    cdec = jnp.broadcast_to(jnp.tile(cdec, (1, n_seq))[:, :, None], (B_HEADS, n_seq * length, 128))
    sdec = jnp.broadcast_to(sdec[:, None, None], (B_HEADS, 1, B_DV))
    return dec, rdec, cdec, sdec


def _rope_tables(start, length):
    inv = ROPE_BASE ** (-jnp.arange(HALF, dtype=jnp.float32) / HALF)
    ang = (start + jnp.arange(length)).astype(jnp.float32)[:, None] * inv[None, :]
    return jnp.cos(ang), jnp.sin(ang)


def _rope_split_tables(length):
    assert length % ROPE_SPLIT == 0
    inv = ROPE_BASE ** (-jnp.arange(HALF, dtype=jnp.float32) / HALF)
    ang_hi = (ROPE_SPLIT * jnp.arange(length // ROPE_SPLIT)).astype(jnp.float32)[:, None] * inv[None, :]
    ang_lo = jnp.arange(ROPE_SPLIT).astype(jnp.float32)[:, None] * inv[None, :]
    return (jnp.stack([jnp.cos(ang_hi), jnp.sin(ang_hi)], axis=0),
            jnp.stack([jnp.cos(ang_lo), jnp.sin(ang_lo)], axis=0))


def _gate_weights(w_s, b_s, c, n_seq=1):
    i = jnp.arange(c)
    mask = (i[None, :] // CHUNK) <= (i[:, None] // CHUNK)
    w = jnp.where(mask, w_s[..., :c, :c], 0.0)
    w = jnp.einsum("ab,...ij->...aibj", jnp.eye(n_seq, dtype=w.dtype), w)
    w = _bf(w.reshape(w.shape[:-4] + (n_seq * c, n_seq * c)))
    bias = jnp.swapaxes(b_s[..., :c], -1, -2)
    bias = jnp.broadcast_to(bias[..., None], bias.shape + (A_DH,)).reshape(bias.shape[:-1] + (D_A,))
    bias = jnp.tile(bias, (1,) * (bias.ndim - 2) + (n_seq, 1))
    return w, bias


def _prompt_layer(layer, x, rope_hi, rope_lo, gn, win, gsgu, ws, bs, decs, wout, gfin, final_norm, in_place):
    seq = x.shape[0]
    tile = PROMPT_TILE
    dec, rdec, cdec, sdec = decs
    row = lambda i: (i, 0)
    kern = functools.partial(_prompt_layer_kernel, final_norm=final_norm)
    return pl.pallas_call(
        kern,
        grid=(seq // tile,),
        in_specs=[
            pl.BlockSpec((tile, D_MODEL), row),
            _const_spec(rope_hi.shape),
            _const_spec(rope_lo.shape),
            _const_spec((1, D_MODEL)),
            _layer_spec((D_MODEL, D_PROJ), layer),
            _const_spec((1, D_A)),
            _const_spec((A_HEADS, A_CHUNK, A_CHUNK)),
            _const_spec((A_CHUNK, D_A)),
            _const_spec((B_HEADS, RET_BLOCK, RET_BLOCK)),
            _const_spec((B_HEADS, RET_BLOCK, 128)),
            _const_spec((B_HEADS, RET_BLOCK, 128)),
            _const_spec((B_HEADS, 1, B_DV)),
            _layer_spec((D_MIX, D_MODEL), layer),
            _const_spec((1, D_MODEL)),
        ],
        out_specs=[
            pl.BlockSpec((tile, D_MODEL), row),
            pl.BlockSpec((B_HEADS, B_DK, B_DV), lambda i: (0, 0, 0)),
        ],
        out_shape=[
            jax.ShapeDtypeStruct((seq, D_MODEL), jnp.float32),
            jax.ShapeDtypeStruct((B_HEADS, B_DK, B_DV), jnp.float32),
        ],
        scratch_shapes=[pltpu.VMEM((tile, D_MIX), jnp.bfloat16)],
        input_output_aliases={0: 0} if in_place else {},
        compiler_params=pltpu.CompilerParams(
            dimension_semantics=("arbitrary",), vmem_limit_bytes=VMEM_LIMIT_BYTES),
        name="prompt_layer",
    )(x, rope_hi, rope_lo, gn, win, gsgu, ws, bs, dec, rdec, cdec, sdec, wout, gfin)


def _sample_trunk(x, state, cos, sin, gn, w_in, gsgu, ws, bs, decs, w_out, gfin, seq):
    rows = x.shape[0]
    nb = rows // seq
    group = SAMPLE_GROUP
    n_groups = nb // group
    n_steps = N_IN_TILES + n_groups + N_OUT_TILES
    dec, rdec, cdec, sdec = decs
    gate_rows = ws.shape[-1]
    tiles_per_block = D_A // W_TILE

    def per_layer(shape):
        zeros = (0,) * len(shape)
        return pl.BlockSpec((None,) + shape, lambda l, s: (l,) + zeros)

    def state_map(l, s):
        return (l, jnp.clip(s - N_IN_TILES, 0, n_groups - 1), 0, 0, 0)

    state_spec = pl.BlockSpec((None, group, B_HEADS, B_DK, B_DV), state_map)
    kern = functools.partial(_sample_kernel, seq=seq)
    return pl.pallas_call(
        kern,
        grid=(DEPTH, n_steps),
        in_specs=[
            pl.BlockSpec(memory_space=pl.ANY),
            pl.BlockSpec(memory_space=pl.ANY),
            _const_spec((group * seq, HALF)),
            _const_spec((group * seq, HALF)),
            per_layer((1, D_MODEL)),
            pl.BlockSpec(memory_space=pl.ANY),
            per_layer((1, D_A)),
            per_layer((A_HEADS, gate_rows, gate_rows)),
            per_layer((gate_rows, D_A)),
            _const_spec((B_HEADS, group * seq, group * seq)),
            _const_spec((B_HEADS, group * seq, 128)),
            _const_spec((B_HEADS, group * seq, 128)),
            _const_spec((B_HEADS, 1, B_DV)),
            pl.BlockSpec(memory_space=pl.ANY),
            _const_spec((1, D_MODEL)),
        ],
        out_specs=[
            pl.BlockSpec((rows, D_MODEL), lambda l, s: (0, 0)),
            state_spec,
            pl.BlockSpec((None, rows, W_TILE),
                         lambda l, s: (l, 0, jnp.clip(s - 2 * tiles_per_block, 0, tiles_per_block - 1))),
            pl.BlockSpec((None, D_MODEL, W_TILE), lambda l, s: (l, 0, jnp.minimum(s, N_IN_TILES - 1))),
            pl.BlockSpec((None, W_TILE, D_MODEL),
                         lambda l, s: (l, jnp.clip(s - N_IN_TILES - n_groups, 0, N_OUT_TILES - 1), 0)),
        ],
        out_shape=[
            jax.ShapeDtypeStruct((rows, D_MODEL), jnp.float32),
            jax.ShapeDtypeStruct(state.shape, jnp.float32),
            jax.ShapeDtypeStruct((DEPTH, rows, D_A), jnp.float32),
            jax.ShapeDtypeStruct(w_in.shape, jnp.bfloat16),
            jax.ShapeDtypeStruct(w_out.shape, jnp.bfloat16),
        ],
        scratch_shapes=[
            pltpu.VMEM((rows, D_MODEL), jnp.bfloat16),
            pltpu.VMEM((N_IN_TILES, rows, W_TILE), jnp.float32),
            pltpu.VMEM((rows, D_MIX), jnp.bfloat16),
            pltpu.VMEM((W_IN_STREAM_SLOTS, D_MODEL, W_TILE), jnp.float32),
            pltpu.VMEM((W_OUT_STREAM_SLOTS, W_TILE, D_MODEL), jnp.float32),
            pltpu.VMEM((STATE_STREAM_SLOTS, group, B_HEADS, B_DK, B_DV), jnp.float32),
            pltpu.SemaphoreType.DMA((3, max(W_IN_STREAM_SLOTS, W_OUT_STREAM_SLOTS, STATE_STREAM_SLOTS))),
            pltpu.SemaphoreType.DMA(()),
        ],
        compiler_params=pltpu.CompilerParams(
            dimension_semantics=("arbitrary", "arbitrary"), vmem_limit_bytes=VMEM_LIMIT_BYTES),
        name="sample_trunk",
    )(x, state, cos, sin, gn, w_in, gsgu, ws, bs, dec, rdec, cdec, sdec, w_out, gfin)


def kernel(x_prompt, x_sample, state_ret, g_norm, w_in, g_sgu, w_s, b_s, w_out, g_final):
    batch, seq, _ = x_prompt.shape
    dec_batch, dec_seq, _ = x_sample.shape
    assert batch == 1 and seq % PROMPT_TILE == 0
    assert PROMPT_TILE % RET_BLOCK == 0 and RET_BLOCK % A_CHUNK == 0 and RET_BLOCK % ROPE_SPLIT == 0
    assert dec_seq <= CHUNK and dec_batch % SAMPLE_GROUP == 0
    assert (dec_batch * dec_seq) % (SAMPLE_GATE_GROUP * dec_seq) == 0
    assert D_A % W_TILE == 0 and W_TILE % B_DK == 0 and W_TILE % A_DH == 0

    gfin = g_final.reshape(1, D_MODEL)
    gn = g_norm.reshape(DEPTH, 1, D_MODEL)
    gsgu = g_sgu.reshape(DEPTH, 1, D_A)

    cos_s, sin_s = _rope_tables(PAST_LEN, dec_seq)
    cos_s, sin_s = jnp.tile(cos_s, (SAMPLE_GROUP, 1)), jnp.tile(sin_s, (SAMPLE_GROUP, 1))
    ws_s, bs_s = _gate_weights(w_s, b_s, dec_seq, SAMPLE_GATE_GROUP)
    hs, new_ret_sample, vn, win_bf, wout_bf = _sample_trunk(
        x_sample.reshape(dec_batch * dec_seq, D_MODEL), state_ret, cos_s, sin_s, gn, w_in, gsgu, ws_s, bs_s,
        _decay_tables(dec_seq, SAMPLE_GROUP), w_out, gfin, dec_seq)

    rope_hi, rope_lo = _rope_split_tables(seq)
    decs_p = _decay_tables(RET_BLOCK)
    ws_p, bs_p = _gate_weights(w_s, b_s, A_CHUNK)
    hp = x_prompt.reshape(seq, D_MODEL)
    ret_p = []
    for l in range(DEPTH):
        hp, sp = _prompt_layer(l, hp, rope_hi, rope_lo, gn[l], win_bf, gsgu[l], ws_p[l], bs_p[l], decs_p, wout_bf,
                               gfin, l == DEPTH - 1, l > 0)
        ret_p.append(sp.reshape(batch, B_HEADS, B_DK, B_DV))

    y_prompt = hp.reshape(batch, seq, D_MODEL)
    y_sample = hs.reshape(dec_batch, dec_seq, D_MODEL)
    new_chunk_v = vn.reshape(DEPTH, dec_batch, dec_seq, A_HEADS, A_DH)
    return (y_prompt, y_sample, jnp.stack(ret_p, axis=0), new_ret_sample, new_chunk_v)
```

```python
import functools

import jax
import jax.numpy as jnp
from jax import lax
from jax.experimental import pallas as pl
from jax.experimental.pallas import tpu as pltpu

D_MODEL = 1024
DEPTH = 4
PAST_LEN = 4096
CHUNK = 64
D_MIX = 2 * D_MODEL
D_A = D_MIX // 2
A_HEADS = 8
A_DH = D_A // A_HEADS
A_CHUNK = 128
D_B = D_MIX - D_A
B_HEADS = 4
B_DK = D_B // B_HEADS
B_DV = D_B // B_HEADS
D_PROJ = 3 * D_A + 4 * D_B
ROPE_BASE = 10000.0
EPS = 1e-6
HALF = B_DK // 2
ROPE_SPLIT = 128

OFF_U, OFF_V, OFF_ZA = 0, D_A, 2 * D_A
OFF_Q, OFF_K, OFF_VB, OFF_ZB = 3 * D_A, 3 * D_A + D_B, 3 * D_A + 2 * D_B, 3 * D_A + 3 * D_B

RET_BLOCK = 256
PROMPT_TILE = 512

SAMPLE_GROUP = 2
SAMPLE_GATE_GROUP = 4
W_TILE = 512
N_IN_TILES = D_PROJ // W_TILE
N_OUT_TILES = D_MIX // W_TILE
W_IN_STREAM_SLOTS = 4
W_OUT_STREAM_SLOTS = 3
STATE_STREAM_SLOTS = 3

VMEM_LIMIT_BYTES = 56 * 1024 * 1024


def _bf(x):
    return x.astype(jnp.bfloat16)


def _dot(a, b):
    return jnp.dot(a, b, preferred_element_type=jnp.float32)


def _dot_nt(a, b):
    return lax.dot_general(a, b, (((1,), (1,)), ((), ())), preferred_element_type=jnp.float32)


def _dot_tn(a, b):
    return lax.dot_general(a, b, (((0,), (0,)), ((), ())), preferred_element_type=jnp.float32)


def _rms(x):
    return x * lax.rsqrt(jnp.mean(x * x, axis=-1, keepdims=True) + EPS)


def _silu(z):
    return z * (1.0 / (1.0 + jnp.exp(-z)))


def _rotary(x, cos, sin):
    x1, x2 = x[:, :HALF], x[:, HALF:]
    return jnp.concatenate([x1 * cos - x2 * sin, x1 * sin + x2 * cos], axis=-1)


def _wide(t):
    return jnp.concatenate([t, t], axis=-1)


def _head_norm(v, gsgu):
    parts = [_rms(v[:, h * A_DH:(h + 1) * A_DH]) for h in range(v.shape[1] // A_DH)]
    return jnp.concatenate(parts, axis=-1) * gsgu


def _spatial_gate(vn, ws_ref, head0, bias):
    c = ws_ref.shape[-1]
    rows = []
    for r in range(vn.shape[0] // c):
        cols = [_dot(ws_ref[head0 + h], vn[r * c:(r + 1) * c, h * A_DH:(h + 1) * A_DH])
                for h in range(vn.shape[1] // A_DH)]
        rows.append(jnp.concatenate(cols, axis=-1) + bias)
    return jnp.concatenate(rows, axis=0)


def _retention_front(q, k, v, states, cdec):
    n = len(states)
    seq = q.shape[0] // n
    qb, kb, vb = _bf(q), _bf(k), _bf(v)
    kw = _bf(k * _wide(cdec))
    scores = _dot_nt(qb, kb)
    cross = [_dot(qb[i * seq:(i + 1) * seq], _bf(states[i])) for i in range(n)]
    return scores, cross, kw, vb


def _retention_back(front, states, dec, rdec, sdec):
    scores, cross, kw, vb = front
    n = len(states)
    seq = vb.shape[0] // n
    updates = [_dot_tn(kw[i * seq:(i + 1) * seq], vb[i * seq:(i + 1) * seq]) for i in range(n)]
    intra = _dot(_bf(scores * dec), vb)
    cross = (cross[0] if n == 1 else jnp.concatenate(cross, axis=0)) * _wide(rdec)
    new_states = [sdec * states[i] + updates[i] for i in range(n)]
    return intra + cross, new_states


def _rope_rows(rope_hi_ref, rope_lo_ref, block):
    ch = rope_hi_ref[0, pl.ds(block, 1), :]
    sh = rope_hi_ref[1, pl.ds(block, 1), :]
    cl = rope_lo_ref[0]
    sl = rope_lo_ref[1]
    return ch * cl - sh * sl, sh * cl + ch * sl


def _prompt_layer_kernel(x_ref, rope_hi_ref, rope_lo_ref, gn_ref, win_ref, gsgu_ref, ws_ref, bs_ref,
                         dec_ref, rdec_ref, cdec_ref, sdec_ref, wout_ref, gfin_ref,
                         y_ref, s_ref, mix_ref, *, final_norm):
    tile = x_ref.shape[0]
    block0 = pl.program_id(0) * (tile // ROPE_SPLIT)

    @pl.when(pl.program_id(0) == 0)
    def _():
        s_ref[...] = jnp.zeros_like(s_ref)

    def start_block(blk):
        r = slice(blk * RET_BLOCK, (blk + 1) * RET_BLOCK)
        x = x_ref[r, :]
        h = _bf(_rms(x) * gn_ref[...])
        v = _dot(h, win_ref[:, OFF_V:OFF_V + D_A])
        u = _dot(h, win_ref[:, OFF_U:OFF_U + D_A])
        za = _dot(h, win_ref[:, OFF_ZA:OFF_ZA + D_A])
        return r, x, h, u, v, za

    n_blocks = tile // RET_BLOCK
    split = D_A + (B_HEADS // 2) * B_DV
    started = start_block(0)
    for blk in range(n_blocks):
        r, x, h, u, v, za = started

        vn = _bf(_head_norm(v, gsgu_ref[...]))
        mixed = _spatial_gate(vn, ws_ref, 0, bs_ref[...])
        mix_ref[r, 0:D_A] = _bf(u * mixed * _silu(za))

        parts = [_rope_rows(rope_hi_ref, rope_lo_ref, block0 + blk * (RET_BLOCK // ROPE_SPLIT) + j)
                 for j in range(RET_BLOCK // ROPE_SPLIT)]
        cos = jnp.concatenate([p[0] for p in parts], axis=0)
        sin = jnp.concatenate([p[1] for p in parts], axis=0)

        def project(hb):
            c0 = hb * B_DK
            q = _rotary(_dot(h, win_ref[:, OFF_Q + c0:OFF_Q + c0 + B_DK]), cos, sin)
            k = _rotary(_dot(h, win_ref[:, OFF_K + c0:OFF_K + c0 + B_DK]), cos, sin) * (B_DK ** -0.5)
            vb = _dot(h, win_ref[:, OFF_VB + c0:OFF_VB + c0 + B_DV])
            zb = _dot(h, win_ref[:, OFF_ZB + c0:OFF_ZB + c0 + B_DV])
            return q, k, vb, zb

        proj = project(0)
        for hb in range(B_HEADS):
            q, k, vb, zb = proj
            state = [s_ref[hb]]
            front = _retention_front(q, k, vb, state, cdec_ref[hb])
            if hb + 1 < B_HEADS:
                proj = project(hb + 1)
            else:
                y = x + _dot(mix_ref[r, 0:split], wout_ref[0:split, :])
            o, (s_new,) = _retention_back(front, state, dec_ref[hb], rdec_ref[hb], sdec_ref[hb])
            s_ref[hb] = s_new
            mix_ref[r, D_A + hb * B_DV:D_A + (hb + 1) * B_DV] = _bf(_rms(o) * _silu(zb))

        if blk + 1 < n_blocks:
            started = start_block(blk + 1)
        y = y + _dot(mix_ref[r, split:D_MIX], wout_ref[split:D_MIX, :])
        if final_norm:
            y = _rms(y) * gfin_ref[...]
        y_ref[r, :] = y


class _TileStream:
    def __init__(self, hbm, stage, sems, sem_row, tiles_per_layer, tile_view):
        self.hbm, self.stage, self.sems, self.sem_row = hbm, stage, sems, sem_row
        self.tiles_per_layer, self.tile_view = tiles_per_layer, tile_view
        self.n_tiles = hbm.shape[0] * tiles_per_layer
        self.n_slots = stage.shape[0]
        self.lookahead = self.n_slots - 1

    def _copy(self, n):
        slot = n % self.n_slots
        src = self.tile_view(self.hbm, n // self.tiles_per_layer, n % self.tiles_per_layer)
        return pltpu.make_async_copy(src, self.stage.at[slot], self.sems.at[self.sem_row, slot])

    def prime(self):
        for n in range(self.lookahead):
            self._copy(n).start()

    def take(self, n):
        @pl.when(n + self.lookahead < self.n_tiles)
        def _():
            self._copy(n + self.lookahead).start()

        self._copy(n).wait()
        return self.stage.at[n % self.n_slots]


def _tile_start(t):
    return t * W_TILE if isinstance(t, int) else pl.multiple_of(t * W_TILE, W_TILE)


def _win_tile(hbm, layer, t):
    return hbm.at[layer, :, pl.ds(_tile_start(t), W_TILE)]


def _wout_tile(hbm, layer, t):
    return hbm.at[layer, pl.ds(_tile_start(t), W_TILE), :]


def _state_group(hbm, layer, g):
    return hbm.at[layer, pl.ds(g * SAMPLE_GROUP, SAMPLE_GROUP)]


def _sample_kernel(x_hbm, s_in_hbm, cos_ref, sin_ref, gn_ref, win_hbm, gsgu_ref, ws_ref, bs_ref,
                   dec_ref, rdec_ref, cdec_ref, sdec_ref, wout_hbm, gfin_ref,
                   y_ref, s_out_ref, vn_ref, win_bf_ref, wout_bf_ref,
                   h_ref, p_ref, mix_ref, win_stage, wout_stage, state_stage, sems, x_sem, *, seq):
    layer = pl.program_id(0)
    step = pl.program_id(1)
    group = SAMPLE_GROUP
    n_groups = y_ref.shape[0] // (group * seq)
    tiles_per_block = D_A // W_TILE
    win_stream = _TileStream(win_hbm, win_stage, sems, 0, N_IN_TILES, _win_tile)
    wout_stream = _TileStream(wout_hbm, wout_stage, sems, 1, N_OUT_TILES, _wout_tile)
    state_stream = _TileStream(s_in_hbm, state_stage, sems, 2, n_groups, _state_group)

    @pl.when((layer == 0) & (step == 0))
    def _():
        win_stream.prime()
        wout_stream.prime()
        state_stream.prime()
        x_copy = pltpu.make_async_copy(x_hbm, y_ref, x_sem)
        x_copy.start()
        x_copy.wait()

    @pl.when(step == 0)
    def _():
        h_ref[...] = _bf(_rms(y_ref[...]) * gn_ref[...])

    @pl.when(step < N_IN_TILES)
    def _():
        w = _bf(win_stream.take(layer * N_IN_TILES + step)[...])
        win_bf_ref[...] = w
        p_ref[step] = _dot(h_ref[...], w)

    for t in range(tiles_per_block):
        @pl.when(step == 2 * tiles_per_block + t)
        def _(t=t):
            cols = slice(t * W_TILE, (t + 1) * W_TILE)
            u = p_ref[t]
            v = p_ref[tiles_per_block + t]
            za = p_ref[2 * tiles_per_block + t]
            vn = _head_norm(v, gsgu_ref[:, cols])
            vn_ref[...] = vn
            mixed = _spatial_gate(_bf(vn), ws_ref, t * (W_TILE // A_DH), bs_ref[:, cols])
            mix_ref[:, cols] = _bf(u * mixed * _silu(za))

    @pl.when((step >= N_IN_TILES) & (step < N_IN_TILES + n_groups))
    def _():
        r0 = pl.multiple_of((step - N_IN_TILES) * (group * seq), group * seq)
        rows = pl.ds(r0, group * seq)
        cos = cos_ref[...]
        sin = sin_ref[...]
        s_in = state_stream.take(layer * n_groups + step - N_IN_TILES)
        heads_per_tile = W_TILE // B_DK
        for hb in range(B_HEADS):
            t, c0 = hb // heads_per_tile, (hb % heads_per_tile) * B_DK
            blocks = [p_ref[(3 + i) * tiles_per_block + t, rows, c0:c0 + B_DK] for i in range(4)]
            q = _rotary(blocks[0], cos, sin)
            k = _rotary(blocks[1], cos, sin) * (B_DK ** -0.5)
            states = [s_in[i, hb] for i in range(group)]
            front = _retention_front(q, k, blocks[2], states, cdec_ref[hb])
            o, new_states = _retention_back(front, states, dec_ref[hb], rdec_ref[hb], sdec_ref[hb])
            for i in range(group):
                s_out_ref[i, hb] = new_states[i]
            mix_ref[rows, D_A + hb * B_DV:D_A + (hb + 1) * B_DV] = _bf(_rms(o) * _silu(blocks[3]))

    @pl.when(step >= N_IN_TILES + n_groups)
    def _():
        w = _bf(wout_stream.take(layer * N_OUT_TILES + step - (N_IN_TILES + n_groups))[...])
        wout_bf_ref[...] = w
        for t in range(N_OUT_TILES):
            @pl.when(step == N_IN_TILES + n_groups + t)
            def _(t=t):
                y_ref[...] += _dot(mix_ref[:, t * W_TILE:(t + 1) * W_TILE], w)

    @pl.when((layer == pl.num_programs(0) - 1) & (step == pl.num_programs(1) - 1))
    def _():
        y_ref[...] = _rms(y_ref[...]) * gfin_ref[...]


def _const_spec(shape):
    zeros = (0,) * len(shape)
    return pl.BlockSpec(shape, lambda *_: zeros, pipeline_mode=pl.Buffered(1))


def _layer_spec(shape, layer):
    zeros = (0,) * len(shape)
    return pl.BlockSpec((None,) + shape, lambda i: (layer,) + zeros, pipeline_mode=pl.Buffered(1))


def _decay_tables(length, n_seq=1):
    lg = jnp.log(1.0 - 2.0 ** (-5.0 - jnp.arange(B_HEADS, dtype=jnp.float32)))
    idx = jnp.arange(length, dtype=jnp.float32)
    diff = idx[:, None] - idx[None, :]
    causal = diff >= 0
    dec = jnp.where(causal[None], jnp.exp(lg[:, None, None] * jnp.where(causal, diff, 0.0)[None]), 0.0)
    rdec = jnp.exp(lg[:, None] * (idx[None, :] + 1.0))
    cdec = jnp.exp(lg[:, None] * (length - 1.0 - idx[None, :]))
    sdec = jnp.exp(lg * length)
    dec = jnp.einsum("ab,hij->haibj", jnp.eye(n_seq, dtype=dec.dtype), dec)
    dec = dec.reshape(B_HEADS, n_seq * length, n_seq * length)
    rdec = jnp.broadcast_to(jnp.tile(rdec, (1, n_seq))[:, :, None], (B_HEADS, n_seq * length, 128))
    cdec = jnp.broadcast_to(jnp.tile(cdec, (1, n_seq))[:, :, None], (B_HEADS, n_seq * length, 128))
    sdec = jnp.broadcast_to(sdec[:, None, None], (B_HEADS, 1, B_DV))
    return dec, rdec, cdec, sdec


def _rope_tables(start, length):
    inv = ROPE_BASE ** (-jnp.arange(HALF, dtype=jnp.float32) / HALF)
    ang = (start + jnp.arange(length)).astype(jnp.float32)[:, None] * inv[None, :]
    return jnp.cos(ang), jnp.sin(ang)


def _rope_split_tables(length):
    assert length % ROPE_SPLIT == 0
    inv = ROPE_BASE ** (-jnp.arange(HALF, dtype=jnp.float32) / HALF)
    ang_hi = (ROPE_SPLIT * jnp.arange(length // ROPE_SPLIT)).astype(jnp.float32)[:, None] * inv[None, :]
    ang_lo = jnp.arange(ROPE_SPLIT).astype(jnp.float32)[:, None] * inv[None, :]
    return (jnp.stack([jnp.cos(ang_hi), jnp.sin(ang_hi)], axis=0),
            jnp.stack([jnp.cos(ang_lo), jnp.sin(ang_lo)], axis=0))


def _gate_weights(w_s, b_s, c, n_seq=1):
    i = jnp.arange(c)
    mask = (i[None, :] // CHUNK) <= (i[:, None] // CHUNK)
    w = jnp.where(mask, w_s[..., :c, :c], 0.0)
    w = jnp.einsum("ab,...ij->...aibj", jnp.eye(n_seq, dtype=w.dtype), w)
    w = _bf(w.reshape(w.shape[:-4] + (n_seq * c, n_seq * c)))
    bias = jnp.swapaxes(b_s[..., :c], -1, -2)
    bias = jnp.broadcast_to(bias[..., None], bias.shape + (A_DH,)).reshape(bias.shape[:-1] + (D_A,))
    bias = jnp.tile(bias, (1,) * (bias.ndim - 2) + (n_seq, 1))
    return w, bias


def _prompt_layer(layer, x, rope_hi, rope_lo, gn, win, gsgu, ws, bs, decs, wout, gfin, final_norm, in_place):
    seq = x.shape[0]
    tile = PROMPT_TILE
    dec, rdec, cdec, sdec = decs
    row = lambda i: (i, 0)
    kern = functools.partial(_prompt_layer_kernel, final_norm=final_norm)
    return pl.pallas_call(
        kern,
        grid=(seq // tile,),
        in_specs=[
            pl.BlockSpec((tile, D_MODEL), row),
            _const_spec(rope_hi.shape),
            _const_spec(rope_lo.shape),
            _const_spec((1, D_MODEL)),
            _layer_spec((D_MODEL, D_PROJ), layer),
            _const_spec((1, D_A)),
            _const_spec((A_HEADS, A_CHUNK, A_CHUNK)),
            _const_spec((A_CHUNK, D_A)),
            _const_spec((B_HEADS, RET_BLOCK, RET_BLOCK)),
            _const_spec((B_HEADS, RET_BLOCK, 128)),
            _const_spec((B_HEADS, RET_BLOCK, 128)),
            _const_spec((B_HEADS, 1, B_DV)),
            _layer_spec((D_MIX, D_MODEL), layer),
            _const_spec((1, D_MODEL)),
        ],
        out_specs=[
            pl.BlockSpec((tile, D_MODEL), row),
            pl.BlockSpec((B_HEADS, B_DK, B_DV), lambda i: (0, 0, 0)),
        ],
        out_shape=[
            jax.ShapeDtypeStruct((seq, D_MODEL), jnp.float32),
            jax.ShapeDtypeStruct((B_HEADS, B_DK, B_DV), jnp.float32),
        ],
        scratch_shapes=[pltpu.VMEM((tile, D_MIX), jnp.bfloat16)],
        input_output_aliases={0: 0} if in_place else {},
        compiler_params=pltpu.CompilerParams(
            dimension_semantics=("arbitrary",), vmem_limit_bytes=VMEM_LIMIT_BYTES),
        name="prompt_layer",
    )(x, rope_hi, rope_lo, gn, win, gsgu, ws, bs, dec, rdec, cdec, sdec, wout, gfin)


def _sample_trunk(x, state, cos, sin, gn, w_in, gsgu, ws, bs, decs, w_out, gfin, seq):
    rows = x.shape[0]
    nb = rows // seq
    group = SAMPLE_GROUP
    n_groups = nb // group
    n_steps = N_IN_TILES + n_groups + N_OUT_TILES
    dec, rdec, cdec, sdec = decs
    gate_rows = ws.shape[-1]
    tiles_per_block = D_A // W_TILE

    def per_layer(shape):
        zeros = (0,) * len(shape)
        return pl.BlockSpec((None,) + shape, lambda l, s: (l,) + zeros)

    def state_map(l, s):
        return (l, jnp.clip(s - N_IN_TILES, 0, n_groups - 1), 0, 0, 0)

    state_spec = pl.BlockSpec((None, group, B_HEADS, B_DK, B_DV), state_map)
    kern = functools.partial(_sample_kernel, seq=seq)
    return pl.pallas_call(
        kern,
        grid=(DEPTH, n_steps),
        in_specs=[
            pl.BlockSpec(memory_space=pl.ANY),
            pl.BlockSpec(memory_space=pl.ANY),
            _const_spec((group * seq, HALF)),
            _const_spec((group * seq, HALF)),
            per_layer((1, D_MODEL)),
            pl.BlockSpec(memory_space=pl.ANY),
            per_layer((1, D_A)),
            per_layer((A_HEADS, gate_rows, gate_rows)),
            per_layer((gate_rows, D_A)),
            _const_spec((B_HEADS, group * seq, group * seq)),
            _const_spec((B_HEADS, group * seq, 128)),
            _const_spec((B_HEADS, group * seq, 128)),
            _const_spec((B_HEADS, 1, B_DV)),
            pl.BlockSpec(memory_space=pl.ANY),
            _const_spec((1, D_MODEL)),
        ],
        out_specs=[
            pl.BlockSpec((rows, D_MODEL), lambda l, s: (0, 0)),
            state_spec,
            pl.BlockSpec((None, rows, W_TILE),
                         lambda l, s: (l, 0, jnp.clip(s - 2 * tiles_per_block, 0, tiles_per_block - 1))),
            pl.BlockSpec((None, D_MODEL, W_TILE), lambda l, s: (l, 0, jnp.minimum(s, N_IN_TILES - 1))),
            pl.BlockSpec((None, W_TILE, D_MODEL),
                         lambda l, s: (l, jnp.clip(s - N_IN_TILES - n_groups, 0, N_OUT_TILES - 1), 0)),
        ],
        out_shape=[
            jax.ShapeDtypeStruct((rows, D_MODEL), jnp.float32),
            jax.ShapeDtypeStruct(state.shape, jnp.float32),
            jax.ShapeDtypeStruct((DEPTH, rows, D_A), jnp.float32),
            jax.ShapeDtypeStruct(w_in.shape, jnp.bfloat16),
            jax.ShapeDtypeStruct(w_out.shape, jnp.bfloat16),
        ],
        scratch_shapes=[
            pltpu.VMEM((rows, D_MODEL), jnp.bfloat16),
            pltpu.VMEM((N_IN_TILES, rows, W_TILE), jnp.float32),
            pltpu.VMEM((rows, D_MIX), jnp.bfloat16),
            pltpu.VMEM((W_IN_STREAM_SLOTS, D_MODEL, W_TILE), jnp.float32),
            pltpu.VMEM((W_OUT_STREAM_SLOTS, W_TILE, D_MODEL), jnp.float32),
            pltpu.VMEM((STATE_STREAM_SLOTS, group, B_HEADS, B_DK, B_DV), jnp.float32),
            pltpu.SemaphoreType.DMA((3, max(W_IN_STREAM_SLOTS, W_OUT_STREAM_SLOTS, STATE_STREAM_SLOTS))),
            pltpu.SemaphoreType.DMA(()),
        ],
        compiler_params=pltpu.CompilerParams(
            dimension_semantics=("arbitrary", "arbitrary"), vmem_limit_bytes=VMEM_LIMIT_BYTES),
        name="sample_trunk",
    )(x, state, cos, sin, gn, w_in, gsgu, ws, bs, dec, rdec, cdec, sdec, w_out, gfin)


def kernel(x_prompt, x_sample, state_ret, g_norm, w_in, g_sgu, w_s, b_s, w_out, g_final):
    batch, seq, _ = x_prompt.shape
    dec_batch, dec_seq, _ = x_sample.shape
    assert batch == 1 and seq % PROMPT_TILE == 0
    assert PROMPT_TILE % RET_BLOCK == 0 and RET_BLOCK % A_CHUNK == 0 and RET_BLOCK % ROPE_SPLIT == 0
    assert dec_seq <= CHUNK and dec_batch % SAMPLE_GROUP == 0
    assert (dec_batch * dec_seq) % (SAMPLE_GATE_GROUP * dec_seq) == 0
    assert D_A % W_TILE == 0 and W_TILE % B_DK == 0 and W_TILE % A_DH == 0

    gfin = g_final.reshape(1, D_MODEL)
    gn = g_norm.reshape(DEPTH, 1, D_MODEL)
    gsgu = g_sgu.reshape(DEPTH, 1, D_A)

    cos_s, sin_s = _rope_tables(PAST_LEN, dec_seq)
    cos_s, sin_s = jnp.tile(cos_s, (SAMPLE_GROUP, 1)), jnp.tile(sin_s, (SAMPLE_GROUP, 1))
    ws_s, bs_s = _gate_weights(w_s, b_s, dec_seq, SAMPLE_GATE_GROUP)
    hs, new_ret_sample, vn, win_bf, wout_bf = _sample_trunk(
        x_sample.reshape(dec_batch * dec_seq, D_MODEL), state_ret, cos_s, sin_s, gn, w_in, gsgu, ws_s, bs_s,
        _decay_tables(dec_seq, SAMPLE_GROUP), w_out, gfin, dec_seq)

    rope_hi, rope_lo = _rope_split_tables(seq)
    decs_p = _decay_tables(RET_BLOCK)
    ws_p, bs_p = _gate_weights(w_s, b_s, A_CHUNK)
    hp = x_prompt.reshape(seq, D_MODEL)
    ret_p = []
    for l in range(DEPTH):
        hp, sp = _prompt_layer(l, hp, rope_hi, rope_lo, gn[l], win_bf, gsgu[l], ws_p[l], bs_p[l], decs_p, wout_bf,
                               gfin, l == DEPTH - 1, l > 0)
        ret_p.append(sp.reshape(batch, B_HEADS, B_DK, B_DV))

    y_prompt = hp.reshape(batch, seq, D_MODEL)
    y_sample = hs.reshape(dec_batch, dec_seq, D_MODEL)
    new_chunk_v = vn.reshape(DEPTH, dec_batch, dec_seq, A_HEADS, A_DH)
    return (y_prompt, y_sample, jnp.stack(ret_p, axis=0), new_ret_sample, new_chunk_v)
```

```python
import functools

import jax
import jax.numpy as jnp
from jax import lax
from jax.experimental import pallas as pl
from jax.experimental.pallas import tpu as pltpu

D_MODEL = 1024
DEPTH = 4
PAST_LEN = 4096
CHUNK = 64
D_MIX = 2 * D_MODEL
D_A = D_MIX // 2
A_HEADS = 8
A_DH = D_A // A_HEADS
A_CHUNK = 128
D_B = D_MIX - D_A
B_HEADS = 4
B_DK = D_B // B_HEADS
B_DV = D_B // B_HEADS
D_PROJ = 3 * D_A + 4 * D_B
ROPE_BASE = 10000.0
EPS = 1e-6
HALF = B_DK // 2
LANES = 128
ROPE_SPLIT = 128

OFF_U, OFF_V, OFF_ZA = 0, D_A, 2 * D_A
OFF_Q, OFF_K, OFF_VB, OFF_ZB = 3 * D_A, 3 * D_A + D_B, 3 * D_A + 2 * D_B, 3 * D_A + 3 * D_B

RET_BLOCK = 256
PROMPT_TILE = 512

SAMPLE_GROUP = 2
SAMPLE_GATE_GROUP = 4
W_TILE = 512
N_IN_TILES = D_PROJ // W_TILE
N_OUT_TILES = D_MIX // W_TILE
W_IN_STREAM_SLOTS = 4
W_OUT_STREAM_SLOTS = 3
STATE_STREAM_SLOTS = 3

VMEM_LIMIT_BYTES = 56 * 1024 * 1024


def _bf(x):
    return x.astype(jnp.bfloat16)


def _dot(a, b):
    return jnp.dot(a, b, preferred_element_type=jnp.float32)


def _dot_nt(a, b):
    return lax.dot_general(a, b, (((1,), (1,)), ((), ())), preferred_element_type=jnp.float32)


def _dot_tn(a, b):
    return lax.dot_general(a, b, (((0,), (0,)), ((), ())), preferred_element_type=jnp.float32)


def _rms(x):
    return x * lax.rsqrt(jnp.mean(x * x, axis=-1, keepdims=True) + EPS)


def _silu(z):
    return z * (1.0 / (1.0 + jnp.exp(-z)))


def _rotary(x, cos, sin):
    x1, x2 = x[:, :HALF], x[:, HALF:]
    return jnp.concatenate([x1 * cos - x2 * sin, x1 * sin + x2 * cos], axis=-1)


def _wide(t):
    return jnp.concatenate([t, t], axis=-1)


def _head_norm(v, gsgu):
    parts = [_rms(v[:, h * A_DH:(h + 1) * A_DH]) for h in range(v.shape[1] // A_DH)]
    return jnp.concatenate(parts, axis=-1) * gsgu


def _spatial_gate(vn, ws_ref, head0, bias):
    c = ws_ref.shape[-1]
    rows = []
    for r in range(vn.shape[0] // c):
        cols = [_dot(ws_ref[head0 + h], vn[r * c:(r + 1) * c, h * A_DH:(h + 1) * A_DH])
                for h in range(vn.shape[1] // A_DH)]
        rows.append(jnp.concatenate(cols, axis=-1) + bias)
    return jnp.concatenate(rows, axis=0)


def _retention_front(q, k, v, states, cdec):
    n = len(states)
    seq = q.shape[0] // n
    qb, kb, vb = _bf(q), _bf(k), _bf(v)
    kw = _bf(k * _wide(cdec))
    scores = _dot_nt(qb, kb)
    cross = [_dot(qb[i * seq:(i + 1) * seq], _bf(states[i])) for i in range(n)]
    return scores, cross, kw, vb


def _retention_back(front, states, dec, rdec, sdec):
    scores, cross, kw, vb = front
    n = len(states)
    seq = vb.shape[0] // n
    updates = [_dot_tn(kw[i * seq:(i + 1) * seq], vb[i * seq:(i + 1) * seq]) for i in range(n)]
    intra = _dot(_bf(scores * dec), vb)
    cross = (cross[0] if n == 1 else jnp.concatenate(cross, axis=0)) * _wide(rdec)
    new_states = [sdec * states[i] + updates[i] for i in range(n)]
    return intra + cross, new_states


def _rope_rows(rope_hi_ref, rope_lo_ref, block):
    ch = rope_hi_ref[0, pl.ds(block, 1), :]
    sh = rope_hi_ref[1, pl.ds(block, 1), :]
    cl = rope_lo_ref[0]
    sl = rope_lo_ref[1]
    return ch * cl - sh * sl, sh * cl + ch * sl


def _prompt_layer_kernel(x_ref, rope_hi_ref, rope_lo_ref, gn_ref, win_ref, gsgu_ref, ws_ref, bs_ref,
                         dec_ref, rdec_ref, cdec_ref, sdec_ref, wout_ref, gfin_ref,
                         y_ref, s_ref, mix_ref, *, final_norm):
    tile = x_ref.shape[0]
    block0 = pl.program_id(0) * (tile // ROPE_SPLIT)

    @pl.when(pl.program_id(0) == 0)
    def _():
        s_ref[...] = jnp.zeros_like(s_ref)

    def start_block(blk):
        r = slice(blk * RET_BLOCK, (blk + 1) * RET_BLOCK)
        x = x_ref[r, :]
        h = _bf(_rms(x) * gn_ref[...])
        v = _dot(h, win_ref[:, OFF_V:OFF_V + D_A])
        u = _dot(h, win_ref[:, OFF_U:OFF_U + D_A])
        za = _dot(h, win_ref[:, OFF_ZA:OFF_ZA + D_A])
        return r, x, h, u, v, za

    n_blocks = tile // RET_BLOCK
    rows_a = D_A + (B_HEADS - 2) * B_DV
    rows_b = rows_a + B_DV
    started = start_block(0)
    for blk in range(n_blocks):
        r, x, h, u, v, za = started

        vn = _bf(_head_norm(v, gsgu_ref[...]))
        mixed = _spatial_gate(vn, ws_ref, 0, bs_ref[...])
        mix_ref[r, 0:D_A] = _bf(u * mixed * _silu(za))

        parts = [_rope_rows(rope_hi_ref, rope_lo_ref, block0 + blk * (RET_BLOCK // ROPE_SPLIT) + j)
                 for j in range(RET_BLOCK // ROPE_SPLIT)]
        cos = jnp.concatenate([p[0] for p in parts], axis=0)
        sin = jnp.concatenate([p[1] for p in parts], axis=0)

        def project(hb):
            c0 = hb * B_DK
            q = _rotary(_dot(h, win_ref[:, OFF_Q + c0:OFF_Q + c0 + B_DK]), cos, sin)
            k = _rotary(_dot(h, win_ref[:, OFF_K + c0:OFF_K + c0 + B_DK]), cos, sin) * (B_DK ** -0.5)
            vb = _dot(h, win_ref[:, OFF_VB + c0:OFF_VB + c0 + B_DV])
            zb = _dot(h, win_ref[:, OFF_ZB + c0:OFF_ZB + c0 + B_DV])
            return q, k, vb, zb

        proj = project(0)
        for hb in range(B_HEADS):
            q, k, vb, zb = proj
            state = [s_ref[hb]]
            front = _retention_front(q, k, vb, state, cdec_ref[hb])
            if hb + 1 < B_HEADS:
                proj = project(hb + 1)
            else:
                y = x + _dot(mix_ref[r, 0:rows_a], wout_ref[0:rows_a, :])
            o, (s_new,) = _retention_back(front, state, dec_ref[hb], rdec_ref[hb], sdec_ref[hb])
            s_ref[hb] = s_new
            mix_ref[r, D_A + hb * B_DV:D_A + (hb + 1) * B_DV] = _bf(_rms(o) * _silu(zb))

        y = y + _dot(mix_ref[r, rows_a:rows_b], wout_ref[rows_a:rows_b, :])
        if blk + 1 < n_blocks:
            started = start_block(blk + 1)
        y = y + _dot(mix_ref[r, rows_b:D_MIX], wout_ref[rows_b:D_MIX, :])
        if final_norm:
            y = _rms(y) * gfin_ref[...]
        y_ref[r, :] = y


class _TileStream:
    def __init__(self, hbm, stage, sems, sem_row, tiles_per_layer, tile_view):
        self.hbm, self.stage, self.sems, self.sem_row = hbm, stage, sems, sem_row
        self.tiles_per_layer, self.tile_view = tiles_per_layer, tile_view
        self.n_tiles = hbm.shape[0] * tiles_per_layer
        self.n_slots = stage.shape[0]
        self.lookahead = self.n_slots - 1

    def _copy(self, n):
        slot = n % self.n_slots
        src = self.tile_view(self.hbm, n // self.tiles_per_layer, n % self.tiles_per_layer)
        return pltpu.make_async_copy(src, self.stage.at[slot], self.sems.at[self.sem_row, slot])

    def prime(self):
        for n in range(self.lookahead):
            self._copy(n).start()

    def take(self, n):
        @pl.when(n + self.lookahead < self.n_tiles)
        def _():
            self._copy(n + self.lookahead).start()

        self._copy(n).wait()
        return self.stage.at[n % self.n_slots]


def _tile_start(t):
    return t * W_TILE if isinstance(t, int) else pl.multiple_of(t * W_TILE, W_TILE)


def _win_tile(hbm, layer, t):
    return hbm.at[layer, :, pl.ds(_tile_start(t), W_TILE)]


def _wout_tile(hbm, layer, t):
    return hbm.at[layer, pl.ds(_tile_start(t), W_TILE), :]


def _state_group(hbm, layer, g):
    return hbm.at[layer, pl.ds(g * SAMPLE_GROUP, SAMPLE_GROUP)]


def _sample_kernel(x_hbm, s_in_hbm, cos_ref, sin_ref, gn_ref, win_hbm, gsgu_ref, ws_ref, bs_ref,
                   dec_ref, rdec_ref, cdec_ref, sdec_ref, wout_hbm, gfin_ref,
                   y_ref, s_out_ref, vn_ref, win_bf_ref, wout_bf_ref,
                   h_ref, p_ref, mix_ref, win_stage, wout_stage, state_stage, sems, x_sem, *, seq):
    layer = pl.program_id(0)
    step = pl.program_id(1)
    group = SAMPLE_GROUP
    n_groups = y_ref.shape[0] // (group * seq)
    tiles_per_block = D_A // W_TILE
    win_stream = _TileStream(win_hbm, win_stage, sems, 0, N_IN_TILES, _win_tile)
    wout_stream = _TileStream(wout_hbm, wout_stage, sems, 1, N_OUT_TILES, _wout_tile)
    state_stream = _TileStream(s_in_hbm, state_stage, sems, 2, n_groups, _state_group)

    @pl.when((layer == 0) & (step == 0))
    def _():
        win_stream.prime()
        wout_stream.prime()
        state_stream.prime()
        x_copy = pltpu.make_async_copy(x_hbm, y_ref, x_sem)
        x_copy.start()
        x_copy.wait()

    @pl.when(step == 0)
    def _():
        h_ref[...] = _bf(_rms(y_ref[...]) * gn_ref[...])

    @pl.when(step < N_IN_TILES)
    def _():
        w = _bf(win_stream.take(layer * N_IN_TILES + step)[...])
        win_bf_ref[...] = w
        p_ref[step] = _dot(h_ref[...], w)

    for t in range(tiles_per_block):
        @pl.when(step == 2 * tiles_per_block + t)
        def _(t=t):
            cols = slice(t * W_TILE, (t + 1) * W_TILE)
            u = p_ref[t]
            v = p_ref[tiles_per_block + t]
            za = p_ref[2 * tiles_per_block + t]
            vn = _head_norm(v, gsgu_ref[:, cols])
            vn_ref[...] = vn
            mixed = _spatial_gate(_bf(vn), ws_ref, t * (W_TILE // A_DH), bs_ref[:, cols])
            mix_ref[:, cols] = _bf(u * mixed * _silu(za))

    @pl.when((step >= N_IN_TILES) & (step < N_IN_TILES + n_groups))
    def _():
        r0 = pl.multiple_of((step - N_IN_TILES) * (group * seq), group * seq)
        rows = pl.ds(r0, group * seq)
        cos = cos_ref[...]
        sin = sin_ref[...]
        s_in = state_stream.take(layer * n_groups + step - N_IN_TILES)
        heads_per_tile = W_TILE // B_DK
        for hb in range(B_HEADS):
            t, c0 = hb // heads_per_tile, (hb % heads_per_tile) * B_DK
            blocks = [p_ref[(3 + i) * tiles_per_block + t, rows, c0:c0 + B_DK] for i in range(4)]
            q = _rotary(blocks[0], cos, sin)
            k = _rotary(blocks[1], cos, sin) * (B_DK ** -0.5)
            states = [s_in[i, hb] for i in range(group)]
            front = _retention_front(q, k, blocks[2], states, cdec_ref[hb])
            o, new_states = _retention_back(front, states, dec_ref[hb], rdec_ref[hb], sdec_ref[hb])
            for i in range(group):
                s_out_ref[i, hb] = new_states[i]
            mix_ref[rows, D_A + hb * B_DV:D_A + (hb + 1) * B_DV] = _bf(_rms(o) * _silu(blocks[3]))

    @pl.when(step >= N_IN_TILES + n_groups)
    def _():
        w = _bf(wout_stream.take(layer * N_OUT_TILES + step - (N_IN_TILES + n_groups))[...])
        wout_bf_ref[...] = w
        for t in range(N_OUT_TILES):
            @pl.when(step == N_IN_TILES + n_groups + t)
            def _(t=t):
                y_ref[...] += _dot(mix_ref[:, t * W_TILE:(t + 1) * W_TILE], w)

    @pl.when((layer == pl.num_programs(0) - 1) & (step == pl.num_programs(1) - 1))
    def _():
        y_ref[...] = _rms(y_ref[...]) * gfin_ref[...]


def _const_spec(shape):
    zeros = (0,) * len(shape)
    return pl.BlockSpec(shape, lambda *_: zeros, pipeline_mode=pl.Buffered(1))


def _layer_spec(shape, layer):
    zeros = (0,) * len(shape)
    return pl.BlockSpec((None,) + shape, lambda i: (layer,) + zeros, pipeline_mode=pl.Buffered(1))


def _decay_tables(length, n_seq=1):
    lg = jnp.log(1.0 - 2.0 ** (-5.0 - jnp.arange(B_HEADS, dtype=jnp.float32)))
    idx = jnp.arange(length, dtype=jnp.float32)
    diff = idx[:, None] - idx[None, :]
    causal = diff >= 0
    dec = jnp.where(causal[None], jnp.exp(lg[:, None, None] * jnp.where(causal, diff, 0.0)[None]), 0.0)
    rdec = jnp.exp(lg[:, None] * (idx[None, :] + 1.0))
    cdec = jnp.exp(lg[:, None] * (length - 1.0 - idx[None, :]))
    sdec = jnp.exp(lg * length)
    dec = jnp.einsum("ab,hij->haibj", jnp.eye(n_seq, dtype=dec.dtype), dec)
    dec = dec.reshape(B_HEADS, n_seq * length, n_seq * length)
    rdec = jnp.broadcast_to(jnp.tile(rdec, (1, n_seq))[:, :, None], (B_HEADS, n_seq * length, LANES))
    cdec = jnp.broadcast_to(jnp.tile(cdec, (1, n_seq))[:, :, None], (B_HEADS, n_seq * length, LANES))
    sdec = jnp.broadcast_to(sdec[:, None, None], (B_HEADS, 1, B_DV))
    return dec, rdec, cdec, sdec


def _rope_tables(start, length):
    inv = ROPE_BASE ** (-jnp.arange(HALF, dtype=jnp.float32) / HALF)
    ang = (start + jnp.arange(length)).astype(jnp.float32)[:, None] * inv[None, :]
    return jnp.cos(ang), jnp.sin(ang)


def _rope_split_tables(length):
    assert length % ROPE_SPLIT == 0
    inv = ROPE_BASE ** (-jnp.arange(HALF, dtype=jnp.float32) / HALF)
    ang_hi = (ROPE_SPLIT * jnp.arange(length // ROPE_SPLIT)).astype(jnp.float32)[:, None] * inv[None, :]
    ang_lo = jnp.arange(ROPE_SPLIT).astype(jnp.float32)[:, None] * inv[None, :]
    return (jnp.stack([jnp.cos(ang_hi), jnp.sin(ang_hi)], axis=0),
            jnp.stack([jnp.cos(ang_lo), jnp.sin(ang_lo)], axis=0))


def _gate_weights(w_s, b_s, c, n_seq=1):
    i = jnp.arange(c)
    mask = (i[None, :] // CHUNK) <= (i[:, None] // CHUNK)
    w = jnp.where(mask, w_s[..., :c, :c], 0.0)
    w = jnp.einsum("ab,...ij->...aibj", jnp.eye(n_seq, dtype=w.dtype), w)
    w = _bf(w.reshape(w.shape[:-4] + (n_seq * c, n_seq * c)))
    bias = jnp.swapaxes(b_s[..., :c], -1, -2)
    bias = jnp.broadcast_to(bias[..., None], bias.shape + (A_DH,)).reshape(bias.shape[:-1] + (D_A,))
    bias = jnp.tile(bias, (1,) * (bias.ndim - 2) + (n_seq, 1))
    return w, bias


def _prompt_layer(layer, x, rope_hi, rope_lo, gn, win, gsgu, ws, bs, decs, wout, gfin, final_norm, in_place):
    seq = x.shape[0]
    tile = PROMPT_TILE
    dec, rdec, cdec, sdec = decs
    row = lambda i: (i, 0)
    kern = functools.partial(_prompt_layer_kernel, final_norm=final_norm)
    return pl.pallas_call(
        kern,
        grid=(seq // tile,),
        in_specs=[
            pl.BlockSpec((tile, D_MODEL), row),
            _const_spec(rope_hi.shape),
            _const_spec(rope_lo.shape),
            _const_spec((1, D_MODEL)),
            _layer_spec((D_MODEL, D_PROJ), layer),
            _const_spec((1, D_A)),
            _const_spec((A_HEADS, A_CHUNK, A_CHUNK)),
            _const_spec((A_CHUNK, D_A)),
            _const_spec((B_HEADS, RET_BLOCK, RET_BLOCK)),
            _const_spec((B_HEADS, RET_BLOCK, LANES)),
            _const_spec((B_HEADS, RET_BLOCK, LANES)),
            _const_spec((B_HEADS, 1, B_DV)),
            _layer_spec((D_MIX, D_MODEL), layer),
            _const_spec((1, D_MODEL)),
        ],
        out_specs=[
            pl.BlockSpec((tile, D_MODEL), row),
            pl.BlockSpec((B_HEADS, B_DK, B_DV), lambda i: (0, 0, 0)),
        ],
        out_shape=[
            jax.ShapeDtypeStruct((seq, D_MODEL), jnp.float32),
            jax.ShapeDtypeStruct((B_HEADS, B_DK, B_DV), jnp.float32),
        ],
        scratch_shapes=[pltpu.VMEM((tile, D_MIX), jnp.bfloat16)],
        input_output_aliases={0: 0} if in_place else {},
        compiler_params=pltpu.CompilerParams(
            dimension_semantics=("arbitrary",), vmem_limit_bytes=VMEM_LIMIT_BYTES),
        name="prompt_layer",
    )(x, rope_hi, rope_lo, gn, win, gsgu, ws, bs, dec, rdec, cdec, sdec, wout, gfin)


def _sample_trunk(x, state, cos, sin, gn, w_in, gsgu, ws, bs, decs, w_out, gfin, seq):
    rows = x.shape[0]
    nb = rows // seq
    group = SAMPLE_GROUP
    n_groups = nb // group
    n_steps = N_IN_TILES + n_groups + N_OUT_TILES
    dec, rdec, cdec, sdec = decs
    gate_rows = ws.shape[-1]
    tiles_per_block = D_A // W_TILE

    def per_layer(shape):
        zeros = (0,) * len(shape)
        return pl.BlockSpec((None,) + shape, lambda l, s: (l,) + zeros)

    def state_map(l, s):
        return (l, jnp.clip(s - N_IN_TILES, 0, n_groups - 1), 0, 0, 0)

    state_spec = pl.BlockSpec((None, group, B_HEADS, B_DK, B_DV), state_map)
    kern = functools.partial(_sample_kernel, seq=seq)
    return pl.pallas_call(
        kern,
        grid=(DEPTH, n_steps),
        in_specs=[
            pl.BlockSpec(memory_space=pl.ANY),
            pl.BlockSpec(memory_space=pl.ANY),
            _const_spec((group * seq, HALF)),
            _const_spec((group * seq, HALF)),
            per_layer((1, D_MODEL)),
            pl.BlockSpec(memory_space=pl.ANY),
            per_layer((1, D_A)),
            per_layer((A_HEADS, gate_rows, gate_rows)),
            per_layer((gate_rows, D_A)),
            _const_spec((B_HEADS, group * seq, group * seq)),
            _const_spec((B_HEADS, group * seq, LANES)),
            _const_spec((B_HEADS, group * seq, LANES)),
            _const_spec((B_HEADS, 1, B_DV)),
            pl.BlockSpec(memory_space=pl.ANY),
            _const_spec((1, D_MODEL)),
        ],
        out_specs=[
            pl.BlockSpec((rows, D_MODEL), lambda l, s: (0, 0)),
            state_spec,
            pl.BlockSpec((None, rows, W_TILE),
                         lambda l, s: (l, 0, jnp.clip(s - 2 * tiles_per_block, 0, tiles_per_block - 1))),
            pl.BlockSpec((None, D_MODEL, W_TILE), lambda l, s: (l, 0, jnp.minimum(s, N_IN_TILES - 1))),
            pl.BlockSpec((None, W_TILE, D_MODEL),
                         lambda l, s: (l, jnp.clip(s - N_IN_TILES - n_groups, 0, N_OUT_TILES - 1), 0)),
        ],
        out_shape=[
            jax.ShapeDtypeStruct((rows, D_MODEL), jnp.float32),
            jax.ShapeDtypeStruct(state.shape, jnp.float32),
            jax.ShapeDtypeStruct((DEPTH, rows, D_A), jnp.float32),
            jax.ShapeDtypeStruct(w_in.shape, jnp.bfloat16),
            jax.ShapeDtypeStruct(w_out.shape, jnp.bfloat16),
        ],
        scratch_shapes=[
            pltpu.VMEM((rows, D_MODEL), jnp.bfloat16),
            pltpu.VMEM((N_IN_TILES, rows, W_TILE), jnp.float32),
            pltpu.VMEM((rows, D_MIX), jnp.bfloat16),
            pltpu.VMEM((W_IN_STREAM_SLOTS, D_MODEL, W_TILE), jnp.float32),
            pltpu.VMEM((W_OUT_STREAM_SLOTS, W_TILE, D_MODEL), jnp.float32),
            pltpu.VMEM((STATE_STREAM_SLOTS, group, B_HEADS, B_DK, B_DV), jnp.float32),
            pltpu.SemaphoreType.DMA((3, max(W_IN_STREAM_SLOTS, W_OUT_STREAM_SLOTS, STATE_STREAM_SLOTS))),
            pltpu.SemaphoreType.DMA(()),
        ],
        compiler_params=pltpu.CompilerParams(
            dimension_semantics=("arbitrary", "arbitrary"), vmem_limit_bytes=VMEM_LIMIT_BYTES),
        name="sample_trunk",
    )(x, state, cos, sin, gn, w_in, gsgu, ws, bs, dec, rdec, cdec, sdec, w_out, gfin)


def kernel(x_prompt, x_sample, state_ret, g_norm, w_in, g_sgu, w_s, b_s, w_out, g_final):
    batch, seq, _ = x_prompt.shape
    dec_batch, dec_seq, _ = x_sample.shape
    assert batch == 1 and seq % PROMPT_TILE == 0
    assert PROMPT_TILE % RET_BLOCK == 0 and RET_BLOCK % A_CHUNK == 0 and RET_BLOCK % ROPE_SPLIT == 0
    assert dec_seq <= CHUNK and dec_batch % SAMPLE_GROUP == 0
    assert (dec_batch * dec_seq) % (SAMPLE_GATE_GROUP * dec_seq) == 0
    assert D_A % W_TILE == 0 and W_TILE % B_DK == 0 and W_TILE % A_DH == 0

    gfin = g_final.reshape(1, D_MODEL)
    gn = g_norm.reshape(DEPTH, 1, D_MODEL)
    gsgu = g_sgu.reshape(DEPTH, 1, D_A)

    cos_s, sin_s = _rope_tables(PAST_LEN, dec_seq)
    cos_s, sin_s = jnp.tile(cos_s, (SAMPLE_GROUP, 1)), jnp.tile(sin_s, (SAMPLE_GROUP, 1))
    ws_s, bs_s = _gate_weights(w_s, b_s, dec_seq, SAMPLE_GATE_GROUP)
    hs, new_ret_sample, vn, win_bf, wout_bf = _sample_trunk(
        x_sample.reshape(dec_batch * dec_seq, D_MODEL), state_ret, cos_s, sin_s, gn, w_in, gsgu, ws_s, bs_s,
        _decay_tables(dec_seq, SAMPLE_GROUP), w_out, gfin, dec_seq)

    rope_hi, rope_lo = _rope_split_tables(seq)
    decs_p = _decay_tables(RET_BLOCK)
    ws_p, bs_p = _gate_weights(w_s, b_s, A_CHUNK)
    hp = x_prompt.reshape(seq, D_MODEL)
    ret_p = []
    for l in range(DEPTH):
        hp, sp = _prompt_layer(l, hp, rope_hi, rope_lo, gn[l], win_bf, gsgu[l], ws_p[l], bs_p[l], decs_p, wout_bf,
                               gfin, l == DEPTH - 1, l > 0)
        ret_p.append(sp.reshape(batch, B_HEADS, B_DK, B_DV))

    y_prompt = hp.reshape(batch, seq, D_MODEL)
    y_sample = hs.reshape(dec_batch, dec_seq, D_MODEL)
    new_chunk_v = vn.reshape(DEPTH, dec_batch, dec_seq, A_HEADS, A_DH)
    return (y_prompt, y_sample, jnp.stack(ret_p, axis=0), new_ret_sample, new_chunk_v)
```

```python
import functools

import jax
import jax.numpy as jnp
from jax import lax
from jax.experimental import pallas as pl
from jax.experimental.pallas import tpu as pltpu

D_MODEL = 1024
DEPTH = 4
PAST_LEN = 4096
CHUNK = 64
D_MIX = 2 * D_MODEL
D_A = D_MIX // 2
A_HEADS = 8
A_DH = D_A // A_HEADS
A_CHUNK = 128
D_B = D_MIX - D_A
B_HEADS = 4
B_DK = D_B // B_HEADS
B_DV = D_B // B_HEADS
D_PROJ = 3 * D_A + 4 * D_B
ROPE_BASE = 10000.0
EPS = 1e-6
HALF = B_DK // 2
LANES = 128
ROPE_SPLIT = 128

OFF_U, OFF_V, OFF_ZA = 0, D_A, 2 * D_A
OFF_Q, OFF_K, OFF_VB, OFF_ZB = 3 * D_A, 3 * D_A + D_B, 3 * D_A + 2 * D_B, 3 * D_A + 3 * D_B

RET_BLOCK = 256
PROMPT_TILE = 512

SAMPLE_GROUP = 2
SAMPLE_GATE_GROUP = 4
W_TILE = 512
N_IN_TILES = D_PROJ // W_TILE
N_OUT_TILES = D_MIX // W_TILE
W_IN_STREAM_SLOTS = 4
W_OUT_STREAM_SLOTS = 3
STATE_STREAM_SLOTS = 3

VMEM_LIMIT_BYTES = 56 * 1024 * 1024


def _bf(x):
    return x.astype(jnp.bfloat16)


def _dot(a, b):
    return jnp.dot(a, b, preferred_element_type=jnp.float32)


def _dot_nt(a, b):
    return lax.dot_general(a, b, (((1,), (1,)), ((), ())), preferred_element_type=jnp.float32)


def _dot_tn(a, b):
    return lax.dot_general(a, b, (((0,), (0,)), ((), ())), preferred_element_type=jnp.float32)


def _rms(x):
    return x * lax.rsqrt(jnp.mean(x * x, axis=-1, keepdims=True) + EPS)


def _silu(z):
    return z * (1.0 / (1.0 + jnp.exp(-z)))


def _rotary(x, cos, sin):
    x1, x2 = x[:, :HALF], x[:, HALF:]
    return jnp.concatenate([x1 * cos - x2 * sin, x1 * sin + x2 * cos], axis=-1)


def _wide(t):
    return jnp.concatenate([t, t], axis=-1)


def _head_norm(v, gsgu):
    parts = [_rms(v[:, h * A_DH:(h + 1) * A_DH]) for h in range(v.shape[1] // A_DH)]
    return jnp.concatenate(parts, axis=-1) * gsgu


def _spatial_gate(vn, ws_ref, head0, bias):
    c = ws_ref.shape[-1]
    rows = []
    for r in range(vn.shape[0] // c):
        cols = [_dot(ws_ref[head0 + h], vn[r * c:(r + 1) * c, h * A_DH:(h + 1) * A_DH])
                for h in range(vn.shape[1] // A_DH)]
        rows.append(jnp.concatenate(cols, axis=-1) + bias)
    return jnp.concatenate(rows, axis=0)


def _spatial_gate_paired(vn, wpair_ref, bias):
    c = wpair_ref.shape[1]
    n_chunks = vn.shape[0] // c
    zeros = jnp.zeros((c, A_DH), vn.dtype)
    outs = []
    for j in range(wpair_ref.shape[0]):
        top, bottom = [], []
        for r in range(n_chunks):
            top += [vn[r * c:(r + 1) * c, (2 * j) * A_DH:(2 * j + 1) * A_DH], zeros]
            bottom += [zeros, vn[r * c:(r + 1) * c, (2 * j + 1) * A_DH:(2 * j + 2) * A_DH]]
        rhs = jnp.concatenate([jnp.concatenate(top, axis=1), jnp.concatenate(bottom, axis=1)], axis=0)
        outs.append(_dot(wpair_ref[j], rhs))
    rows = [jnp.concatenate([o[:, r * 2 * A_DH:(r + 1) * 2 * A_DH] for o in outs], axis=1) + bias
            for r in range(n_chunks)]
    return jnp.concatenate(rows, axis=0)


def _retention_front(q, k, v, states, cdec):
    n = len(states)
    seq = q.shape[0] // n
    qb, kb, vb = _bf(q), _bf(k), _bf(v)
    kw = _bf(k * _wide(cdec))
    scores = _dot_nt(qb, kb)
    cross = [_dot(qb[i * seq:(i + 1) * seq], _bf(states[i])) for i in range(n)]
    return scores, cross, kw, vb


def _retention_back(front, states, dec, rdec, sdec):
    scores, cross, kw, vb = front
    n = len(states)
    seq = vb.shape[0] // n
    updates = [_dot_tn(kw[i * seq:(i + 1) * seq], vb[i * seq:(i + 1) * seq]) for i in range(n)]
    intra = _dot(_bf(scores * dec), vb)
    cross = (cross[0] if n == 1 else jnp.concatenate(cross, axis=0)) * _wide(rdec)
    new_states = [sdec * states[i] + updates[i] for i in range(n)]
    return intra + cross, new_states


def _rope_rows(rope_hi_ref, rope_lo_ref, block):
    ch = rope_hi_ref[0, pl.ds(block, 1), :]
    sh = rope_hi_ref[1, pl.ds(block, 1), :]
    cl = rope_lo_ref[0]
    sl = rope_lo_ref[1]
    return ch * cl - sh * sl, sh * cl + ch * sl


def _prompt_layer_kernel(x_ref, rope_hi_ref, rope_lo_ref, gn_ref, win_ref, gsgu_ref, ws_ref, bs_ref,
                         dec_ref, rdec_ref, cdec_ref, sdec_ref, wout_ref, gfin_ref,
                         y_ref, s_ref, mix_ref, *, final_norm):
    tile = x_ref.shape[0]
    block0 = pl.program_id(0) * (tile // ROPE_SPLIT)

    @pl.when(pl.program_id(0) == 0)
    def _():
        s_ref[...] = jnp.zeros_like(s_ref)

    def start_block(blk):
        r = slice(blk * RET_BLOCK, (blk + 1) * RET_BLOCK)
        x = x_ref[r, :]
        h = _bf(_rms(x) * gn_ref[...])
        v = _dot(h, win_ref[:, OFF_V:OFF_V + D_A])
        u = _dot(h, win_ref[:, OFF_U:OFF_U + D_A])
        za = _dot(h, win_ref[:, OFF_ZA:OFF_ZA + D_A])
        return r, x, h, u, v, za

    n_blocks = tile // RET_BLOCK
    rows_a = D_A + (B_HEADS - 2) * B_DV
    rows_b = rows_a + B_DV
    started = start_block(0)
    for blk in range(n_blocks):
        r, x, h, u, v, za = started

        vn = _bf(_head_norm(v, gsgu_ref[...]))
        mixed = _spatial_gate_paired(vn, ws_ref, bs_ref[...])
        mix_ref[r, 0:D_A] = _bf(u * mixed * _silu(za))

        parts = [_rope_rows(rope_hi_ref, rope_lo_ref, block0 + blk * (RET_BLOCK // ROPE_SPLIT) + j)
                 for j in range(RET_BLOCK // ROPE_SPLIT)]
        cos = jnp.concatenate([p[0] for p in parts], axis=0)
        sin = jnp.concatenate([p[1] for p in parts], axis=0)

        def project(hb):
            c0 = hb * B_DK
            q = _rotary(_dot(h, win_ref[:, OFF_Q + c0:OFF_Q + c0 + B_DK]), cos, sin)
            k = _rotary(_dot(h, win_ref[:, OFF_K + c0:OFF_K + c0 + B_DK]), cos, sin) * (B_DK ** -0.5)
            vb = _dot(h, win_ref[:, OFF_VB + c0:OFF_VB + c0 + B_DV])
            zb = _dot(h, win_ref[:, OFF_ZB + c0:OFF_ZB + c0 + B_DV])
            return q, k, vb, zb

        proj = project(0)
        for hb in range(B_HEADS):
            q, k, vb, zb = proj
            state = [s_ref[hb]]
            front = _retention_front(q, k, vb, state, cdec_ref[hb])
            if hb + 1 < B_HEADS:
                proj = project(hb + 1)
            else:
                y = x + _dot(mix_ref[r, 0:rows_a], wout_ref[0:rows_a, :])
            o, (s_new,) = _retention_back(front, state, dec_ref[hb], rdec_ref[hb], sdec_ref[hb])
            s_ref[hb] = s_new
            mix_ref[r, D_A + hb * B_DV:D_A + (hb + 1) * B_DV] = _bf(_rms(o) * _silu(zb))

        y = y + _dot(mix_ref[r, rows_a:rows_b], wout_ref[rows_a:rows_b, :])
        if blk + 1 < n_blocks:
            started = start_block(blk + 1)
        y = y + _dot(mix_ref[r, rows_b:D_MIX], wout_ref[rows_b:D_MIX, :])
        if final_norm:
            y = _rms(y) * gfin_ref[...]
        y_ref[r, :] = y


class _TileStream:
    def __init__(self, hbm, stage, sems, sem_row, tiles_per_layer, tile_view):
        self.hbm, self.stage, self.sems, self.sem_row = hbm, stage, sems, sem_row
        self.tiles_per_layer, self.tile_view = tiles_per_layer, tile_view
        self.n_tiles = hbm.shape[0] * tiles_per_layer
        self.n_slots = stage.shape[0]
        self.lookahead = self.n_slots - 1

    def _copy(self, n):
        slot = n % self.n_slots
        src = self.tile_view(self.hbm, n // self.tiles_per_layer, n % self.tiles_per_layer)
        return pltpu.make_async_copy(src, self.stage.at[slot], self.sems.at[self.sem_row, slot])

    def prime(self):
        for n in range(self.lookahead):
            self._copy(n).start()

    def take(self, n):
        @pl.when(n + self.lookahead < self.n_tiles)
        def _():
            self._copy(n + self.lookahead).start()

        self._copy(n).wait()
        return self.stage.at[n % self.n_slots]


def _tile_start(t):
    return t * W_TILE if isinstance(t, int) else pl.multiple_of(t * W_TILE, W_TILE)


def _win_tile(hbm, layer, t):
    return hbm.at[layer, :, pl.ds(_tile_start(t), W_TILE)]


def _wout_tile(hbm, layer, t):
    return hbm.at[layer, pl.ds(_tile_start(t), W_TILE), :]


def _state_group(hbm, layer, g):
    return hbm.at[layer, pl.ds(g * SAMPLE_GROUP, SAMPLE_GROUP)]


def _sample_kernel(x_hbm, s_in_hbm, cos_ref, sin_ref, gn_ref, win_hbm, gsgu_ref, ws_ref, bs_ref,
                   dec_ref, rdec_ref, cdec_ref, sdec_ref, wout_hbm, gfin_ref,
                   y_ref, s_out_ref, vn_ref, win_bf_ref, wout_bf_ref,
                   h_ref, p_ref, mix_ref, win_stage, wout_stage, state_stage, sems, x_sem, *, seq):
    layer = pl.program_id(0)
    step = pl.program_id(1)
    group = SAMPLE_GROUP
    n_groups = y_ref.shape[0] // (group * seq)
    tiles_per_block = D_A // W_TILE
    win_stream = _TileStream(win_hbm, win_stage, sems, 0, N_IN_TILES, _win_tile)
    wout_stream = _TileStream(wout_hbm, wout_stage, sems, 1, N_OUT_TILES, _wout_tile)
    state_stream = _TileStream(s_in_hbm, state_stage, sems, 2, n_groups, _state_group)

    @pl.when((layer == 0) & (step == 0))
    def _():
        win_stream.prime()
        wout_stream.prime()
        state_stream.prime()
        x_copy = pltpu.make_async_copy(x_hbm, y_ref, x_sem)
        x_copy.start()
        x_copy.wait()

    @pl.when(step == 0)
    def _():
        h_ref[...] = _bf(_rms(y_ref[...]) * gn_ref[...])

    @pl.when(step < N_IN_TILES)
    def _():
        w = _bf(win_stream.take(layer * N_IN_TILES + step)[...])
        win_bf_ref[...] = w
        p_ref[step] = _dot(h_ref[...], w)

    for t in range(tiles_per_block):
        @pl.when(step == 2 * tiles_per_block + t)
        def _(t=t):
            cols = slice(t * W_TILE, (t + 1) * W_TILE)
            u = p_ref[t]
            v = p_ref[tiles_per_block + t]
            za = p_ref[2 * tiles_per_block + t]
            vn = _head_norm(v, gsgu_ref[:, cols])
            vn_ref[...] = vn
            mixed = _spatial_gate(_bf(vn), ws_ref, t * (W_TILE // A_DH), bs_ref[:, cols])
            mix_ref[:, cols] = _bf(u * mixed * _silu(za))

    @pl.when((step >= N_IN_TILES) & (step < N_IN_TILES + n_groups))
    def _():
        r0 = pl.multiple_of((step - N_IN_TILES) * (group * seq), group * seq)
        rows = pl.ds(r0, group * seq)
        cos = cos_ref[...]
        sin = sin_ref[...]
        s_in = state_stream.take(layer * n_groups + step - N_IN_TILES)
        heads_per_tile = W_TILE // B_DK
        for hb in range(B_HEADS):
            t, c0 = hb // heads_per_tile, (hb % heads_per_tile) * B_DK
            blocks = [p_ref[(3 + i) * tiles_per_block + t, rows, c0:c0 + B_DK] for i in range(4)]
            q = _rotary(blocks[0], cos, sin)
            k = _rotary(blocks[1], cos, sin) * (B_DK ** -0.5)
            states = [s_in[i, hb] for i in range(group)]
            front = _retention_front(q, k, blocks[2], states, cdec_ref[hb])
            o, new_states = _retention_back(front, states, dec_ref[hb], rdec_ref[hb], sdec_ref[hb])
            for i in range(group):
                s_out_ref[i, hb] = new_states[i]
            mix_ref[rows, D_A + hb * B_DV:D_A + (hb + 1) * B_DV] = _bf(_rms(o) * _silu(blocks[3]))

    @pl.when(step >= N_IN_TILES + n_groups)
    def _():
        w = _bf(wout_stream.take(layer * N_OUT_TILES + step - (N_IN_TILES + n_groups))[...])
        wout_bf_ref[...] = w
        for t in range(N_OUT_TILES):
            @pl.when(step == N_IN_TILES + n_groups + t)
            def _(t=t):
                y_ref[...] += _dot(mix_ref[:, t * W_TILE:(t + 1) * W_TILE], w)

    @pl.when((layer == pl.num_programs(0) - 1) & (step == pl.num_programs(1) - 1))
    def _():
        y_ref[...] = _rms(y_ref[...]) * gfin_ref[...]


def _const_spec(shape):
    zeros = (0,) * len(shape)
    return pl.BlockSpec(shape, lambda *_: zeros, pipeline_mode=pl.Buffered(1))


def _layer_spec(shape, layer):
    zeros = (0,) * len(shape)
    return pl.BlockSpec((None,) + shape, lambda i: (layer,) + zeros, pipeline_mode=pl.Buffered(1))


def _decay_tables(length, n_seq=1):
    lg = jnp.log(1.0 - 2.0 ** (-5.0 - jnp.arange(B_HEADS, dtype=jnp.float32)))
    idx = jnp.arange(length, dtype=jnp.float32)
    diff = idx[:, None] - idx[None, :]
    causal = diff >= 0
    dec = jnp.where(causal[None], jnp.exp(lg[:, None, None] * jnp.where(causal, diff, 0.0)[None]), 0.0)
    rdec = jnp.exp(lg[:, None] * (idx[None, :] + 1.0))
    cdec = jnp.exp(lg[:, None] * (length - 1.0 - idx[None, :]))
    sdec = jnp.exp(lg * length)
    dec = jnp.einsum("ab,hij->haibj", jnp.eye(n_seq, dtype=dec.dtype), dec)
    dec = dec.reshape(B_HEADS, n_seq * length, n_seq * length)
    rdec = jnp.broadcast_to(jnp.tile(rdec, (1, n_seq))[:, :, None], (B_HEADS, n_seq * length, LANES))
    cdec = jnp.broadcast_to(jnp.tile(cdec, (1, n_seq))[:, :, None], (B_HEADS, n_seq * length, LANES))
    sdec = jnp.broadcast_to(sdec[:, None, None], (B_HEADS, 1, B_DV))
    return dec, rdec, cdec, sdec


def _rope_tables(start, length):
    inv = ROPE_BASE ** (-jnp.arange(HALF, dtype=jnp.float32) / HALF)
    ang = (start + jnp.arange(length)).astype(jnp.float32)[:, None] * inv[None, :]
    return jnp.cos(ang), jnp.sin(ang)


def _rope_split_tables(length):
    assert length % ROPE_SPLIT == 0
    inv = ROPE_BASE ** (-jnp.arange(HALF, dtype=jnp.float32) / HALF)
    ang_hi = (ROPE_SPLIT * jnp.arange(length // ROPE_SPLIT)).astype(jnp.float32)[:, None] * inv[None, :]
    ang_lo = jnp.arange(ROPE_SPLIT).astype(jnp.float32)[:, None] * inv[None, :]
    return (jnp.stack([jnp.cos(ang_hi), jnp.sin(ang_hi)], axis=0),
            jnp.stack([jnp.cos(ang_lo), jnp.sin(ang_lo)], axis=0))


def _gate_weights(w_s, b_s, c, n_seq=1):
    i = jnp.arange(c)
    mask = (i[None, :] // CHUNK) <= (i[:, None] // CHUNK)
    w = jnp.where(mask, w_s[..., :c, :c], 0.0)
    w = jnp.einsum("ab,...ij->...aibj", jnp.eye(n_seq, dtype=w.dtype), w)
    w = _bf(w.reshape(w.shape[:-4] + (n_seq * c, n_seq * c)))
    bias = jnp.swapaxes(b_s[..., :c], -1, -2)
    bias = jnp.broadcast_to(bias[..., None], bias.shape + (A_DH,)).reshape(bias.shape[:-1] + (D_A,))
    bias = jnp.tile(bias, (1,) * (bias.ndim - 2) + (n_seq, 1))
    return w, bias


def _prompt_layer(layer, x, rope_hi, rope_lo, gn, win, gsgu, ws, bs, decs, wout, gfin, final_norm, in_place):
    seq = x.shape[0]
    tile = PROMPT_TILE
    dec, rdec, cdec, sdec = decs
    row = lambda i: (i, 0)
    kern = functools.partial(_prompt_layer_kernel, final_norm=final_norm)
    return pl.pallas_call(
        kern,
        grid=(seq // tile,),
        in_specs=[
            pl.BlockSpec((tile, D_MODEL), row),
            _const_spec(rope_hi.shape),
            _const_spec(rope_lo.shape),
            _const_spec((1, D_MODEL)),
            _layer_spec((D_MODEL, D_PROJ), layer),
            _const_spec((1, D_A)),
            _const_spec((A_HEADS // 2, A_CHUNK, 2 * A_CHUNK)),
            _const_spec((A_CHUNK, D_A)),
            _const_spec((B_HEADS, RET_BLOCK, RET_BLOCK)),
            _const_spec((B_HEADS, RET_BLOCK, LANES)),
            _const_spec((B_HEADS, RET_BLOCK, LANES)),
            _const_spec((B_HEADS, 1, B_DV)),
            _layer_spec((D_MIX, D_MODEL), layer),
            _const_spec((1, D_MODEL)),
        ],
        out_specs=[
            pl.BlockSpec((tile, D_MODEL), row),
            pl.BlockSpec((B_HEADS, B_DK, B_DV), lambda i: (0, 0, 0)),
        ],
        out_shape=[
            jax.ShapeDtypeStruct((seq, D_MODEL), jnp.float32),
            jax.ShapeDtypeStruct((B_HEADS, B_DK, B_DV), jnp.float32),
        ],
        scratch_shapes=[pltpu.VMEM((tile, D_MIX), jnp.bfloat16)],
        input_output_aliases={0: 0} if in_place else {},
        compiler_params=pltpu.CompilerParams(
            dimension_semantics=("arbitrary",), vmem_limit_bytes=VMEM_LIMIT_BYTES),
        name="prompt_layer",
    )(x, rope_hi, rope_lo, gn, win, gsgu, ws, bs, dec, rdec, cdec, sdec, wout, gfin)


def _sample_trunk(x, state, cos, sin, gn, w_in, gsgu, ws, bs, decs, w_out, gfin, seq):
    rows = x.shape[0]
    nb = rows // seq
    group = SAMPLE_GROUP
    n_groups = nb // group
    n_steps = N_IN_TILES + n_groups + N_OUT_TILES
    dec, rdec, cdec, sdec = decs
    gate_rows = ws.shape[-1]
    tiles_per_block = D_A // W_TILE

    def per_layer(shape):
        zeros = (0,) * len(shape)
        return pl.BlockSpec((None,) + shape, lambda l, s: (l,) + zeros)

    def state_map(l, s):
        return (l, jnp.clip(s - N_IN_TILES, 0, n_groups - 1), 0, 0, 0)

    state_spec = pl.BlockSpec((None, group, B_HEADS, B_DK, B_DV), state_map)
    kern = functools.partial(_sample_kernel, seq=seq)
    return pl.pallas_call(
        kern,
        grid=(DEPTH, n_steps),
        in_specs=[
            pl.BlockSpec(memory_space=pl.ANY),
            pl.BlockSpec(memory_space=pl.ANY),
            _const_spec((group * seq, HALF)),
            _const_spec((group * seq, HALF)),
            per_layer((1, D_MODEL)),
            pl.BlockSpec(memory_space=pl.ANY),
            per_layer((1, D_A)),
            per_layer((A_HEADS, gate_rows, gate_rows)),
            per_layer((gate_rows, D_A)),
            _const_spec((B_HEADS, group * seq, group * seq)),
            _const_spec((B_HEADS, group * seq, LANES)),
            _const_spec((B_HEADS, group * seq, LANES)),
            _const_spec((B_HEADS, 1, B_DV)),
            pl.BlockSpec(memory_space=pl.ANY),
            _const_spec((1, D_MODEL)),
        ],
        out_specs=[
            pl.BlockSpec((rows, D_MODEL), lambda l, s: (0, 0)),
            state_spec,
            pl.BlockSpec((None, rows, W_TILE),
                         lambda l, s: (l, 0, jnp.clip(s - 2 * tiles_per_block, 0, tiles_per_block - 1))),
            pl.BlockSpec((None, D_MODEL, W_TILE), lambda l, s: (l, 0, jnp.minimum(s, N_IN_TILES - 1))),
            pl.BlockSpec((None, W_TILE, D_MODEL),
                         lambda l, s: (l, jnp.clip(s - N_IN_TILES - n_groups, 0, N_OUT_TILES - 1), 0)),
        ],
        out_shape=[
            jax.ShapeDtypeStruct((rows, D_MODEL), jnp.float32),
            jax.ShapeDtypeStruct(state.shape, jnp.float32),
            jax.ShapeDtypeStruct((DEPTH, rows, D_A), jnp.float32),
            jax.ShapeDtypeStruct(w_in.shape, jnp.bfloat16),
            jax.ShapeDtypeStruct(w_out.shape, jnp.bfloat16),
        ],
        scratch_shapes=[
            pltpu.VMEM((rows, D_MODEL), jnp.bfloat16),
            pltpu.VMEM((N_IN_TILES, rows, W_TILE), jnp.float32),
            pltpu.VMEM((rows, D_MIX), jnp.bfloat16),
            pltpu.VMEM((W_IN_STREAM_SLOTS, D_MODEL, W_TILE), jnp.float32),
            pltpu.VMEM((W_OUT_STREAM_SLOTS, W_TILE, D_MODEL), jnp.float32),
            pltpu.VMEM((STATE_STREAM_SLOTS, group, B_HEADS, B_DK, B_DV), jnp.float32),
            pltpu.SemaphoreType.DMA((3, max(W_IN_STREAM_SLOTS, W_OUT_STREAM_SLOTS, STATE_STREAM_SLOTS))),
            pltpu.SemaphoreType.DMA(()),
        ],
        compiler_params=pltpu.CompilerParams(
            dimension_semantics=("arbitrary", "arbitrary"), vmem_limit_bytes=VMEM_LIMIT_BYTES),
        name="sample_trunk",
    )(x, state, cos, sin, gn, w_in, gsgu, ws, bs, dec, rdec, cdec, sdec, w_out, gfin)


def kernel(x_prompt, x_sample, state_ret, g_norm, w_in, g_sgu, w_s, b_s, w_out, g_final):
    batch, seq, _ = x_prompt.shape
    dec_batch, dec_seq, _ = x_sample.shape
    assert batch == 1 and seq % PROMPT_TILE == 0
    assert PROMPT_TILE % RET_BLOCK == 0 and RET_BLOCK % A_CHUNK == 0 and RET_BLOCK % ROPE_SPLIT == 0
    assert dec_seq <= CHUNK and dec_batch % SAMPLE_GROUP == 0
    assert (dec_batch * dec_seq) % (SAMPLE_GATE_GROUP * dec_seq) == 0
    assert D_A % W_TILE == 0 and W_TILE % B_DK == 0 and W_TILE % A_DH == 0

    gfin = g_final.reshape(1, D_MODEL)
    gn = g_norm.reshape(DEPTH, 1, D_MODEL)
    gsgu = g_sgu.reshape(DEPTH, 1, D_A)

    cos_s, sin_s = _rope_tables(PAST_LEN, dec_seq)
    cos_s, sin_s = jnp.tile(cos_s, (SAMPLE_GROUP, 1)), jnp.tile(sin_s, (SAMPLE_GROUP, 1))
    ws_s, bs_s = _gate_weights(w_s, b_s, dec_seq, SAMPLE_GATE_GROUP)
    hs, new_ret_sample, vn, win_bf, wout_bf = _sample_trunk(
        x_sample.reshape(dec_batch * dec_seq, D_MODEL), state_ret, cos_s, sin_s, gn, w_in, gsgu, ws_s, bs_s,
        _decay_tables(dec_seq, SAMPLE_GROUP), w_out, gfin, dec_seq)

    rope_hi, rope_lo = _rope_split_tables(seq)
    decs_p = _decay_tables(RET_BLOCK)
    ws_p, bs_p = _gate_weights(w_s, b_s, A_CHUNK)
    ws_p = ws_p.reshape(DEPTH, A_HEADS // 2, 2, A_CHUNK, A_CHUNK).transpose(0, 1, 3, 2, 4)
    ws_p = ws_p.reshape(DEPTH, A_HEADS // 2, A_CHUNK, 2 * A_CHUNK)
    hp = x_prompt.reshape(seq, D_MODEL)
    ret_p = []
    for l in range(DEPTH):
        hp, sp = _prompt_layer(l, hp, rope_hi, rope_lo, gn[l], win_bf, gsgu[l], ws_p[l], bs_p[l], decs_p, wout_bf,
                               gfin, l == DEPTH - 1, l > 0)
        ret_p.append(sp.reshape(batch, B_HEADS, B_DK, B_DV))

    y_prompt = hp.reshape(batch, seq, D_MODEL)
    y_sample = hs.reshape(dec_batch, dec_seq, D_MODEL)
    new_chunk_v = vn.reshape(DEPTH, dec_batch, dec_seq, A_HEADS, A_DH)
    return (y_prompt, y_sample, jnp.stack(ret_p, axis=0), new_ret_sample, new_chunk_v)
```

```python
import functools

import jax
import jax.numpy as jnp
from jax import lax
from jax.experimental import pallas as pl
from jax.experimental.pallas import tpu as pltpu

D_MODEL = 1024
DEPTH = 4
PAST_LEN = 4096
CHUNK = 64
D_MIX = 2 * D_MODEL
D_A = D_MIX // 2
A_HEADS = 8
A_DH = D_A // A_HEADS
A_CHUNK = 128
D_B = D_MIX - D_A
B_HEADS = 4
B_DK = D_B // B_HEADS
B_DV = D_B // B_HEADS
D_PROJ = 3 * D_A + 4 * D_B
ROPE_BASE = 10000.0
EPS = 1e-6
HALF = B_DK // 2
LANES = 128
ROPE_SPLIT = 128

OFF_U, OFF_V, OFF_ZA = 0, D_A, 2 * D_A
OFF_Q, OFF_K, OFF_VB, OFF_ZB = 3 * D_A, 3 * D_A + D_B, 3 * D_A + 2 * D_B, 3 * D_A + 3 * D_B

RET_BLOCK = 256
PROMPT_TILE = 1024

SAMPLE_GROUP = 2
SAMPLE_GATE_GROUP = 4
W_TILE = 512
N_IN_TILES = D_PROJ // W_TILE
N_OUT_TILES = D_MIX // W_TILE
W_IN_STREAM_SLOTS = 4
W_OUT_STREAM_SLOTS = 3
STATE_STREAM_SLOTS = 3

VMEM_LIMIT_BYTES = 56 * 1024 * 1024


def _bf(x):
    return x.astype(jnp.bfloat16)


def _dot(a, b):
    return jnp.dot(a, b, preferred_element_type=jnp.float32)


def _dot_nt(a, b):
    return lax.dot_general(a, b, (((1,), (1,)), ((), ())), preferred_element_type=jnp.float32)


def _dot_tn(a, b):
    return lax.dot_general(a, b, (((0,), (0,)), ((), ())), preferred_element_type=jnp.float32)


def _rms(x):
    return x * lax.rsqrt(jnp.mean(x * x, axis=-1, keepdims=True) + EPS)


def _silu(z):
    return z * (1.0 / (1.0 + jnp.exp(-z)))


def _rotary(x, cos, sin):
    x1, x2 = x[:, :HALF], x[:, HALF:]
    return jnp.concatenate([x1 * cos - x2 * sin, x1 * sin + x2 * cos], axis=-1)


def _wide(t):
    return jnp.concatenate([t, t], axis=-1)


def _head_norm(v, gsgu):
    parts = [_rms(v[:, h * A_DH:(h + 1) * A_DH]) for h in range(v.shape[1] // A_DH)]
    return jnp.concatenate(parts, axis=-1) * gsgu


def _spatial_gate(vn, ws_ref, head0, bias):
    c = ws_ref.shape[-1]
    rows = []
    for r in range(vn.shape[0] // c):
        cols = [_dot(ws_ref[head0 + h], vn[r * c:(r + 1) * c, h * A_DH:(h + 1) * A_DH])
                for h in range(vn.shape[1] // A_DH)]
        rows.append(jnp.concatenate(cols, axis=-1) + bias)
    return jnp.concatenate(rows, axis=0)


def _spatial_gate_paired(vn, wpair_ref, bias):
    c = wpair_ref.shape[1]
    n_chunks = vn.shape[0] // c
    zeros = jnp.zeros((c, A_DH), vn.dtype)
    outs = []
    for j in range(wpair_ref.shape[0]):
        top, bottom = [], []
        for r in range(n_chunks):
            top += [vn[r * c:(r + 1) * c, (2 * j) * A_DH:(2 * j + 1) * A_DH], zeros]
            bottom += [zeros, vn[r * c:(r + 1) * c, (2 * j + 1) * A_DH:(2 * j + 2) * A_DH]]
        rhs = jnp.concatenate([jnp.concatenate(top, axis=1), jnp.concatenate(bottom, axis=1)], axis=0)
        outs.append(_dot(wpair_ref[j], rhs))
    rows = [jnp.concatenate([o[:, r * 2 * A_DH:(r + 1) * 2 * A_DH] for o in outs], axis=1) + bias
            for r in range(n_chunks)]
    return jnp.concatenate(rows, axis=0)


def _retention_front(q, k, v, states, cdec):
    n = len(states)
    seq = q.shape[0] // n
    qb, kb, vb = _bf(q), _bf(k), _bf(v)
    kw = _bf(k * _wide(cdec))
    scores = _dot_nt(qb, kb)
    cross = [_dot(qb[i * seq:(i + 1) * seq], _bf(states[i])) for i in range(n)]
    return scores, cross, kw, vb


def _retention_back(front, states, dec, rdec, sdec):
    scores, cross, kw, vb = front
    n = len(states)
    seq = vb.shape[0] // n
    updates = [_dot_tn(kw[i * seq:(i + 1) * seq], vb[i * seq:(i + 1) * seq]) for i in range(n)]
    intra = _dot(_bf(scores * dec), vb)
    cross = (cross[0] if n == 1 else jnp.concatenate(cross, axis=0)) * _wide(rdec)
    new_states = [sdec * states[i] + updates[i] for i in range(n)]
    return intra + cross, new_states


def _rope_rows(rope_hi_ref, rope_lo_ref, block):
    ch = rope_hi_ref[0, pl.ds(block, 1), :]
    sh = rope_hi_ref[1, pl.ds(block, 1), :]
    cl = rope_lo_ref[0]
    sl = rope_lo_ref[1]
    return ch * cl - sh * sl, sh * cl + ch * sl


def _prompt_layer_kernel(x_ref, rope_hi_ref, rope_lo_ref, gn_ref, win_ref, gsgu_ref, ws_ref, bs_ref,
                         dec_ref, rdec_ref, cdec_ref, sdec_ref, wout_ref, gfin_ref,
                         y_ref, s_ref, mix_ref, *, final_norm):
    tile = x_ref.shape[0]
    block0 = pl.program_id(0) * (tile // ROPE_SPLIT)

    @pl.when(pl.program_id(0) == 0)
    def _():
        s_ref[...] = jnp.zeros_like(s_ref)

    def start_block(blk):
        r = slice(blk * RET_BLOCK, (blk + 1) * RET_BLOCK)
        x = x_ref[r, :]
        h = _bf(_rms(x) * gn_ref[...])
        v = _dot(h, win_ref[:, OFF_V:OFF_V + D_A])
        u = _dot(h, win_ref[:, OFF_U:OFF_U + D_A])
        za = _dot(h, win_ref[:, OFF_ZA:OFF_ZA + D_A])
        return r, x, h, u, v, za

    n_blocks = tile // RET_BLOCK
    rows_a = D_A + (B_HEADS - 2) * B_DV
    rows_b = rows_a + B_DV
    started = start_block(0)
    for blk in range(n_blocks):
        r, x, h, u, v, za = started

        vn = _bf(_head_norm(v, gsgu_ref[...]))
        mixed = _spatial_gate_paired(vn, ws_ref, bs_ref[...])
        mix_ref[r, 0:D_A] = _bf(u * mixed * _silu(za))

        parts = [_rope_rows(rope_hi_ref, rope_lo_ref, block0 + blk * (RET_BLOCK // ROPE_SPLIT) + j)
                 for j in range(RET_BLOCK // ROPE_SPLIT)]
        cos = jnp.concatenate([p[0] for p in parts], axis=0)
        sin = jnp.concatenate([p[1] for p in parts], axis=0)

        def project(hb):
            c0 = hb * B_DK
            q = _rotary(_dot(h, win_ref[:, OFF_Q + c0:OFF_Q + c0 + B_DK]), cos, sin)
            k = _rotary(_dot(h, win_ref[:, OFF_K + c0:OFF_K + c0 + B_DK]), cos, sin) * (B_DK ** -0.5)
            vb = _dot(h, win_ref[:, OFF_VB + c0:OFF_VB + c0 + B_DV])
            zb = _dot(h, win_ref[:, OFF_ZB + c0:OFF_ZB + c0 + B_DV])
            return q, k, vb, zb

        proj = project(0)
        for hb in range(B_HEADS):
            q, k, vb, zb = proj
            state = [s_ref[hb]]
            front = _retention_front(q, k, vb, state, cdec_ref[hb])
            if hb + 1 < B_HEADS:
                proj = project(hb + 1)
            else:
                y = x + _dot(mix_ref[r, 0:rows_a], wout_ref[0:rows_a, :])
            o, (s_new,) = _retention_back(front, state, dec_ref[hb], rdec_ref[hb], sdec_ref[hb])
            s_ref[hb] = s_new
            mix_ref[r, D_A + hb * B_DV:D_A + (hb + 1) * B_DV] = _bf(_rms(o) * _silu(zb))

        y = y + _dot(mix_ref[r, rows_a:rows_b], wout_ref[rows_a:rows_b, :])
        if blk + 1 < n_blocks:
            started = start_block(blk + 1)
        y = y + _dot(mix_ref[r, rows_b:D_MIX], wout_ref[rows_b:D_MIX, :])
        if final_norm:
            y = _rms(y) * gfin_ref[...]
        y_ref[r, :] = y


class _TileStream:
    def __init__(self, hbm, stage, sems, sem_row, tiles_per_layer, tile_view):
        self.hbm, self.stage, self.sems, self.sem_row = hbm, stage, sems, sem_row
        self.tiles_per_layer, self.tile_view = tiles_per_layer, tile_view
        self.n_tiles = hbm.shape[0] * tiles_per_layer
        self.n_slots = stage.shape[0]
        self.lookahead = self.n_slots - 1

    def _copy(self, n):
        slot = n % self.n_slots
        src = self.tile_view(self.hbm, n // self.tiles_per_layer, n % self.tiles_per_layer)
        return pltpu.make_async_copy(src, self.stage.at[slot], self.sems.at[self.sem_row, slot])

    def prime(self):
        for n in range(self.lookahead):
            self._copy(n).start()

    def take(self, n):
        @pl.when(n + self.lookahead < self.n_tiles)
        def _():
            self._copy(n + self.lookahead).start()

        self._copy(n).wait()
        return self.stage.at[n % self.n_slots]


def _tile_start(t):
    return t * W_TILE if isinstance(t, int) else pl.multiple_of(t * W_TILE, W_TILE)


def _win_tile(hbm, layer, t):
    return hbm.at[layer, :, pl.ds(_tile_start(t), W_TILE)]


def _wout_tile(hbm, layer, t):
    return hbm.at[layer, pl.ds(_tile_start(t), W_TILE), :]


def _state_group(hbm, layer, g):
    return hbm.at[layer, pl.ds(g * SAMPLE_GROUP, SAMPLE_GROUP)]


def _sample_kernel(x_hbm, s_in_hbm, cos_ref, sin_ref, gn_ref, win_hbm, gsgu_ref, ws_ref, bs_ref,
                   dec_ref, rdec_ref, cdec_ref, sdec_ref, wout_hbm, gfin_ref,
                   y_ref, s_out_ref, vn_ref, win_bf_ref, wout_bf_ref,
                   h_ref, p_ref, mix_ref, win_stage, wout_stage, state_stage, sems, x_sem, *, seq):
    layer = pl.program_id(0)
    step = pl.program_id(1)
    group = SAMPLE_GROUP
    n_groups = y_ref.shape[0] // (group * seq)
    tiles_per_block = D_A // W_TILE
    win_stream = _TileStream(win_hbm, win_stage, sems, 0, N_IN_TILES, _win_tile)
    wout_stream = _TileStream(wout_hbm, wout_stage, sems, 1, N_OUT_TILES, _wout_tile)
    state_stream = _TileStream(s_in_hbm, state_stage, sems, 2, n_groups, _state_group)

    @pl.when((layer == 0) & (step == 0))
    def _():
        win_stream.prime()
        wout_stream.prime()
        state_stream.prime()
        x_copy = pltpu.make_async_copy(x_hbm, y_ref, x_sem)
        x_copy.start()
        x_copy.wait()

    @pl.when(step == 0)
    def _():
        h_ref[...] = _bf(_rms(y_ref[...]) * gn_ref[...])

    @pl.when(step < N_IN_TILES)
    def _():
        w = _bf(win_stream.take(layer * N_IN_TILES + step)[...])
        win_bf_ref[...] = w
        p_ref[step] = _dot(h_ref[...], w)

    for t in range(tiles_per_block):
        @pl.when(step == 2 * tiles_per_block + t)
        def _(t=t):
            cols = slice(t * W_TILE, (t + 1) * W_TILE)
            u = p_ref[t]
            v = p_ref[tiles_per_block + t]
            za = p_ref[2 * tiles_per_block + t]
            vn = _head_norm(v, gsgu_ref[:, cols])
            vn_ref[...] = vn
            mixed = _spatial_gate(_bf(vn), ws_ref, t * (W_TILE // A_DH), bs_ref[:, cols])
            mix_ref[:, cols] = _bf(u * mixed * _silu(za))

    @pl.when((step >= N_IN_TILES) & (step < N_IN_TILES + n_groups))
    def _():
        r0 = pl.multiple_of((step - N_IN_TILES) * (group * seq), group * seq)
        rows = pl.ds(r0, group * seq)
        cos = cos_ref[...]
        sin = sin_ref[...]
        s_in = state_stream.take(layer * n_groups + step - N_IN_TILES)
        heads_per_tile = W_TILE // B_DK
        for hb in range(B_HEADS):
            t, c0 = hb // heads_per_tile, (hb % heads_per_tile) * B_DK
            blocks = [p_ref[(3 + i) * tiles_per_block + t, rows, c0:c0 + B_DK] for i in range(4)]
            q = _rotary(blocks[0], cos, sin)
            k = _rotary(blocks[1], cos, sin) * (B_DK ** -0.5)
            states = [s_in[i, hb] for i in range(group)]
            front = _retention_front(q, k, blocks[2], states, cdec_ref[hb])
            o, new_states = _retention_back(front, states, dec_ref[hb], rdec_ref[hb], sdec_ref[hb])
            for i in range(group):
                s_out_ref[i, hb] = new_states[i]
            mix_ref[rows, D_A + hb * B_DV:D_A + (hb + 1) * B_DV] = _bf(_rms(o) * _silu(blocks[3]))

    @pl.when(step >= N_IN_TILES + n_groups)
    def _():
        w = _bf(wout_stream.take(layer * N_OUT_TILES + step - (N_IN_TILES + n_groups))[...])
        wout_bf_ref[...] = w
        for t in range(N_OUT_TILES):
            @pl.when(step == N_IN_TILES + n_groups + t)
            def _(t=t):
                y_ref[...] += _dot(mix_ref[:, t * W_TILE:(t + 1) * W_TILE], w)

    @pl.when((layer == pl.num_programs(0) - 1) & (step == pl.num_programs(1) - 1))
    def _():
        y_ref[...] = _rms(y_ref[...]) * gfin_ref[...]


def _const_spec(shape):
    zeros = (0,) * len(shape)
    return pl.BlockSpec(shape, lambda *_: zeros, pipeline_mode=pl.Buffered(1))


def _layer_spec(shape, layer):
    zeros = (0,) * len(shape)
    return pl.BlockSpec((None,) + shape, lambda i: (layer,) + zeros, pipeline_mode=pl.Buffered(1))


def _decay_tables(length, n_seq=1):
    lg = jnp.log(1.0 - 2.0 ** (-5.0 - jnp.arange(B_HEADS, dtype=jnp.float32)))
    idx = jnp.arange(length, dtype=jnp.float32)
    diff = idx[:, None] - idx[None, :]
    causal = diff >= 0
    dec = jnp.where(causal[None], jnp.exp(lg[:, None, None] * jnp.where(causal, diff, 0.0)[None]), 0.0)
    rdec = jnp.exp(lg[:, None] * (idx[None, :] + 1.0))
    cdec = jnp.exp(lg[:, None] * (length - 1.0 - idx[None, :]))
    sdec = jnp.exp(lg * length)
    dec = jnp.einsum("ab,hij->haibj", jnp.eye(n_seq, dtype=dec.dtype), dec)
    dec = dec.reshape(B_HEADS, n_seq * length, n_seq * length)
    rdec = jnp.broadcast_to(jnp.tile(rdec, (1, n_seq))[:, :, None], (B_HEADS, n_seq * length, LANES))
    cdec = jnp.broadcast_to(jnp.tile(cdec, (1, n_seq))[:, :, None], (B_HEADS, n_seq * length, LANES))
    sdec = jnp.broadcast_to(sdec[:, None, None], (B_HEADS, 1, B_DV))
    return dec, rdec, cdec, sdec


def _rope_tables(start, length):
    inv = ROPE_BASE ** (-jnp.arange(HALF, dtype=jnp.float32) / HALF)
    ang = (start + jnp.arange(length)).astype(jnp.float32)[:, None] * inv[None, :]
    return jnp.cos(ang), jnp.sin(ang)


def _rope_split_tables(length):
    assert length % ROPE_SPLIT == 0
    inv = ROPE_BASE ** (-jnp.arange(HALF, dtype=jnp.float32) / HALF)
    ang_hi = (ROPE_SPLIT * jnp.arange(length // ROPE_SPLIT)).astype(jnp.float32)[:, None] * inv[None, :]
    ang_lo = jnp.arange(ROPE_SPLIT).astype(jnp.float32)[:, None] * inv[None, :]
    return (jnp.stack([jnp.cos(ang_hi), jnp.sin(ang_hi)], axis=0),
            jnp.stack([jnp.cos(ang_lo), jnp.sin(ang_lo)], axis=0))


def _gate_weights(w_s, b_s, c, n_seq=1):
    i = jnp.arange(c)
    mask = (i[None, :] // CHUNK) <= (i[:, None] // CHUNK)
    w = jnp.where(mask, w_s[..., :c, :c], 0.0)
    w = jnp.einsum("ab,...ij->...aibj", jnp.eye(n_seq, dtype=w.dtype), w)
    w = _bf(w.reshape(w.shape[:-4] + (n_seq * c, n_seq * c)))
    bias = jnp.swapaxes(b_s[..., :c], -1, -2)
    bias = jnp.broadcast_to(bias[..., None], bias.shape + (A_DH,)).reshape(bias.shape[:-1] + (D_A,))
    bias = jnp.tile(bias, (1,) * (bias.ndim - 2) + (n_seq, 1))
    return w, bias


def _prompt_layer(layer, x, rope_hi, rope_lo, gn, win, gsgu, ws, bs, decs, wout, gfin, final_norm, in_place):
    seq = x.shape[0]
    tile = PROMPT_TILE
    dec, rdec, cdec, sdec = decs
    row = lambda i: (i, 0)
    kern = functools.partial(_prompt_layer_kernel, final_norm=final_norm)
    return pl.pallas_call(
        kern,
        grid=(seq // tile,),
        in_specs=[
            pl.BlockSpec((tile, D_MODEL), row),
            _const_spec(rope_hi.shape),
            _const_spec(rope_lo.shape),
            _const_spec((1, D_MODEL)),
            _layer_spec((D_MODEL, D_PROJ), layer),
            _const_spec((1, D_A)),
            _const_spec((A_HEADS // 2, A_CHUNK, 2 * A_CHUNK)),
            _const_spec((A_CHUNK, D_A)),
            _const_spec((B_HEADS, RET_BLOCK, RET_BLOCK)),
            _const_spec((B_HEADS, RET_BLOCK, LANES)),
            _const_spec((B_HEADS, RET_BLOCK, LANES)),
            _const_spec((B_HEADS, 1, B_DV)),
            _layer_spec((D_MIX, D_MODEL), layer),
            _const_spec((1, D_MODEL)),
        ],
        out_specs=[
            pl.BlockSpec((tile, D_MODEL), row),
            pl.BlockSpec((B_HEADS, B_DK, B_DV), lambda i: (0, 0, 0)),
        ],
        out_shape=[
            jax.ShapeDtypeStruct((seq, D_MODEL), jnp.float32),
            jax.ShapeDtypeStruct((B_HEADS, B_DK, B_DV), jnp.float32),
        ],
        scratch_shapes=[pltpu.VMEM((tile, D_MIX), jnp.bfloat16)],
        input_output_aliases={0: 0} if in_place else {},
        compiler_params=pltpu.CompilerParams(
            dimension_semantics=("arbitrary",), vmem_limit_bytes=VMEM_LIMIT_BYTES),
        name="prompt_layer",
    )(x, rope_hi, rope_lo, gn, win, gsgu, ws, bs, dec, rdec, cdec, sdec, wout, gfin)


def _sample_trunk(x, state, cos, sin, gn, w_in, gsgu, ws, bs, decs, w_out, gfin, seq):
    rows = x.shape[0]
    nb = rows // seq
    group = SAMPLE_GROUP
    n_groups = nb // group
    n_steps = N_IN_TILES + n_groups + N_OUT_TILES
    dec, rdec, cdec, sdec = decs
    gate_rows = ws.shape[-1]
    tiles_per_block = D_A // W_TILE

    def per_layer(shape):
        zeros = (0,) * len(shape)
        return pl.BlockSpec((None,) + shape, lambda l, s: (l,) + zeros)

    def state_map(l, s):
        return (l, jnp.clip(s - N_IN_TILES, 0, n_groups - 1), 0, 0, 0)

    state_spec = pl.BlockSpec((None, group, B_HEADS, B_DK, B_DV), state_map)
    kern = functools.partial(_sample_kernel, seq=seq)
    return pl.pallas_call(
        kern,
        grid=(DEPTH, n_steps),
        in_specs=[
            pl.BlockSpec(memory_space=pl.ANY),
            pl.BlockSpec(memory_space=pl.ANY),
            _const_spec((group * seq, HALF)),
            _const_spec((group * seq, HALF)),
            per_layer((1, D_MODEL)),
            pl.BlockSpec(memory_space=pl.ANY),
            per_layer((1, D_A)),
            per_layer((A_HEADS, gate_rows, gate_rows)),
            per_layer((gate_rows, D_A)),
            _const_spec((B_HEADS, group * seq, group * seq)),
            _const_spec((B_HEADS, group * seq, LANES)),
            _const_spec((B_HEADS, group * seq, LANES)),
            _const_spec((B_HEADS, 1, B_DV)),
            pl.BlockSpec(memory_space=pl.ANY),
            _const_spec((1, D_MODEL)),
        ],
        out_specs=[
            pl.BlockSpec((rows, D_MODEL), lambda l, s: (0, 0)),
            state_spec,
            pl.BlockSpec((None, rows, W_TILE),
                         lambda l, s: (l, 0, jnp.clip(s - 2 * tiles_per_block, 0, tiles_per_block - 1))),
            pl.BlockSpec((None, D_MODEL, W_TILE), lambda l, s: (l, 0, jnp.minimum(s, N_IN_TILES - 1))),
            pl.BlockSpec((None, W_TILE, D_MODEL),
                         lambda l, s: (l, jnp.clip(s - N_IN_TILES - n_groups, 0, N_OUT_TILES - 1), 0)),
        ],
        out_shape=[
            jax.ShapeDtypeStruct((rows, D_MODEL), jnp.float32),
            jax.ShapeDtypeStruct(state.shape, jnp.float32),
            jax.ShapeDtypeStruct((DEPTH, rows, D_A), jnp.float32),
            jax.ShapeDtypeStruct(w_in.shape, jnp.bfloat16),
            jax.ShapeDtypeStruct(w_out.shape, jnp.bfloat16),
        ],
        scratch_shapes=[
            pltpu.VMEM((rows, D_MODEL), jnp.bfloat16),
            pltpu.VMEM((N_IN_TILES, rows, W_TILE), jnp.float32),
            pltpu.VMEM((rows, D_MIX), jnp.bfloat16),
            pltpu.VMEM((W_IN_STREAM_SLOTS, D_MODEL, W_TILE), jnp.float32),
            pltpu.VMEM((W_OUT_STREAM_SLOTS, W_TILE, D_MODEL), jnp.float32),
            pltpu.VMEM((STATE_STREAM_SLOTS, group, B_HEADS, B_DK, B_DV), jnp.float32),
            pltpu.SemaphoreType.DMA((3, max(W_IN_STREAM_SLOTS, W_OUT_STREAM_SLOTS, STATE_STREAM_SLOTS))),
            pltpu.SemaphoreType.DMA(()),
        ],
        compiler_params=pltpu.CompilerParams(
            dimension_semantics=("arbitrary", "arbitrary"), vmem_limit_bytes=VMEM_LIMIT_BYTES),
        name="sample_trunk",
    )(x, state, cos, sin, gn, w_in, gsgu, ws, bs, dec, rdec, cdec, sdec, w_out, gfin)


def kernel(x_prompt, x_sample, state_ret, g_norm, w_in, g_sgu, w_s, b_s, w_out, g_final):
    batch, seq, _ = x_prompt.shape
    dec_batch, dec_seq, _ = x_sample.shape
    assert batch == 1 and seq % PROMPT_TILE == 0
    assert PROMPT_TILE % RET_BLOCK == 0 and RET_BLOCK % A_CHUNK == 0 and RET_BLOCK % ROPE_SPLIT == 0
    assert dec_seq <= CHUNK and dec_batch % SAMPLE_GROUP == 0
    assert (dec_batch * dec_seq) % (SAMPLE_GATE_GROUP * dec_seq) == 0
    assert D_A % W_TILE == 0 and W_TILE % B_DK == 0 and W_TILE % A_DH == 0

    gfin = g_final.reshape(1, D_MODEL)
    gn = g_norm.reshape(DEPTH, 1, D_MODEL)
    gsgu = g_sgu.reshape(DEPTH, 1, D_A)

    cos_s, sin_s = _rope_tables(PAST_LEN, dec_seq)
    cos_s, sin_s = jnp.tile(cos_s, (SAMPLE_GROUP, 1)), jnp.tile(sin_s, (SAMPLE_GROUP, 1))
    ws_s, bs_s = _gate_weights(w_s, b_s, dec_seq, SAMPLE_GATE_GROUP)
    hs, new_ret_sample, vn, win_bf, wout_bf = _sample_trunk(
        x_sample.reshape(dec_batch * dec_seq, D_MODEL), state_ret, cos_s, sin_s, gn, w_in, gsgu, ws_s, bs_s,
        _decay_tables(dec_seq, SAMPLE_GROUP), w_out, gfin, dec_seq)

    rope_hi, rope_lo = _rope_split_tables(seq)
    decs_p = _decay_tables(RET_BLOCK)
    ws_p, bs_p = _gate_weights(w_s, b_s, A_CHUNK)
    ws_p = ws_p.reshape(DEPTH, A_HEADS // 2, 2, A_CHUNK, A_CHUNK).transpose(0, 1, 3, 2, 4)
    ws_p = ws_p.reshape(DEPTH, A_HEADS // 2, A_CHUNK, 2 * A_CHUNK)
    hp = x_prompt.reshape(seq, D_MODEL)
    ret_p = []
    for l in range(DEPTH):
        hp, sp = _prompt_layer(l, hp, rope_hi, rope_lo, gn[l], win_bf, gsgu[l], ws_p[l], bs_p[l], decs_p, wout_bf,
                               gfin, l == DEPTH - 1, l > 0)
        ret_p.append(sp.reshape(batch, B_HEADS, B_DK, B_DV))

    y_prompt = hp.reshape(batch, seq, D_MODEL)
    y_sample = hs.reshape(dec_batch, dec_seq, D_MODEL)
    new_chunk_v = vn.reshape(DEPTH, dec_batch, dec_seq, A_HEADS, A_DH)
    return (y_prompt, y_sample, jnp.stack(ret_p, axis=0), new_ret_sample, new_chunk_v)
```

```python
import functools

import jax
import jax.numpy as jnp
from jax import lax
from jax.experimental import pallas as pl
from jax.experimental.pallas import tpu as pltpu

D_MODEL = 1024
DEPTH = 4
PAST_LEN = 4096
CHUNK = 64
D_MIX = 2 * D_MODEL
D_A = D_MIX // 2
A_HEADS = 8
A_DH = D_A // A_HEADS
A_CHUNK = 128
D_B = D_MIX - D_A
B_HEADS = 4
B_DK = D_B // B_HEADS
B_DV = D_B // B_HEADS
D_PROJ = 3 * D_A + 4 * D_B
ROPE_BASE = 10000.0
EPS = 1e-6
HALF = B_DK // 2
LANES = 128
ROPE_SPLIT = 128

OFF_U, OFF_V, OFF_ZA = 0, D_A, 2 * D_A
OFF_Q, OFF_K, OFF_VB, OFF_ZB = 3 * D_A, 3 * D_A + D_B, 3 * D_A + 2 * D_B, 3 * D_A + 3 * D_B

RET_BLOCK = 256
PROMPT_TILE = 512

SAMPLE_GROUP = 2
SAMPLE_GATE_GROUP = 4
W_TILE = 512
N_IN_TILES = D_PROJ // W_TILE
N_OUT_TILES = D_MIX // W_TILE
W_IN_STREAM_SLOTS = 4
W_OUT_STREAM_SLOTS = 3
STATE_STREAM_SLOTS = 3

VMEM_LIMIT_BYTES = 56 * 1024 * 1024


def _bf(x):
    return x.astype(jnp.bfloat16)


def _dot(a, b):
    return jnp.dot(a, b, preferred_element_type=jnp.float32)


def _dot_nt(a, b):
    return lax.dot_general(a, b, (((1,), (1,)), ((), ())), preferred_element_type=jnp.float32)


def _dot_tn(a, b):
    return lax.dot_general(a, b, (((0,), (0,)), ((), ())), preferred_element_type=jnp.float32)


def _rms(x):
    return x * lax.rsqrt(jnp.mean(x * x, axis=-1, keepdims=True) + EPS)


def _silu(z):
    return z * (1.0 / (1.0 + jnp.exp(-z)))


def _rotary(x, cos, sin):
    x1, x2 = x[:, :HALF], x[:, HALF:]
    return jnp.concatenate([x1 * cos - x2 * sin, x1 * sin + x2 * cos], axis=-1)


def _wide(t):
    return jnp.concatenate([t, t], axis=-1)


def _head_norm(v, gsgu):
    parts = [_rms(v[:, h * A_DH:(h + 1) * A_DH]) for h in range(v.shape[1] // A_DH)]
    return jnp.concatenate(parts, axis=-1) * gsgu


def _spatial_gate(vn, ws_ref, head0, bias):
    c = ws_ref.shape[-1]
    rows = []
    for r in range(vn.shape[0] // c):
        cols = [_dot(ws_ref[head0 + h], vn[r * c:(r + 1) * c, h * A_DH:(h + 1) * A_DH])
                for h in range(vn.shape[1] // A_DH)]
        rows.append(jnp.concatenate(cols, axis=-1) + bias)
    return jnp.concatenate(rows, axis=0)


def _spatial_gate_paired(vn, wpair_ref, bias):
    c = wpair_ref.shape[1]
    n_chunks = vn.shape[0] // c
    zeros = jnp.zeros((c, A_DH), vn.dtype)
    outs = []
    for j in range(wpair_ref.shape[0]):
        top, bottom = [], []
        for r in range(n_chunks):
            top += [vn[r * c:(r + 1) * c, (2 * j) * A_DH:(2 * j + 1) * A_DH], zeros]
            bottom += [zeros, vn[r * c:(r + 1) * c, (2 * j + 1) * A_DH:(2 * j + 2) * A_DH]]
        rhs = jnp.concatenate([jnp.concatenate(top, axis=1), jnp.concatenate(bottom, axis=1)], axis=0)
        outs.append(_dot(wpair_ref[j], rhs))
    rows = [jnp.concatenate([o[:, r * 2 * A_DH:(r + 1) * 2 * A_DH] for o in outs], axis=1) + bias
            for r in range(n_chunks)]
    return jnp.concatenate(rows, axis=0)


def _retention_front(q, k, v, states, cdec):
    n = len(states)
    seq = q.shape[0] // n
    qb, kb, vb = _bf(q), _bf(k), _bf(v)
    kw = _bf(k * _wide(cdec))
    scores = _dot_nt(qb, kb)
    cross = [_dot(qb[i * seq:(i + 1) * seq], _bf(states[i])) for i in range(n)]
    return scores, cross, kw, vb


def _retention_back(front, states, dec, rdec, sdec):
    scores, cross, kw, vb = front
    n = len(states)
    seq = vb.shape[0] // n
    updates = [_dot_tn(kw[i * seq:(i + 1) * seq], vb[i * seq:(i + 1) * seq]) for i in range(n)]
    intra = _dot(_bf(scores * dec), vb)
    cross = (cross[0] if n == 1 else jnp.concatenate(cross, axis=0)) * _wide(rdec)
    new_states = [sdec * states[i] + updates[i] for i in range(n)]
    return intra + cross, new_states


def _rope_rows(rope_hi_ref, rope_lo_ref, block):
    ch = rope_hi_ref[0, pl.ds(block, 1), :]
    sh = rope_hi_ref[1, pl.ds(block, 1), :]
    cl = rope_lo_ref[0]
    sl = rope_lo_ref[1]
    return ch * cl - sh * sl, sh * cl + ch * sl


def _prompt_layer_kernel(x_ref, rope_hi_ref, rope_lo_ref, gn_ref, win_ref, gsgu_ref, ws_ref, bs_ref,
                         dec_ref, rdec_ref, cdec_ref, sdec_ref, wout_ref, gfin_ref,
                         y_ref, s_ref, mix_ref, *, final_norm):
    tile = x_ref.shape[0]
    block0 = pl.program_id(0) * (tile // ROPE_SPLIT)

    @pl.when(pl.program_id(0) == 0)
    def _():
        s_ref[...] = jnp.zeros_like(s_ref)

    def start_block(blk):
        r = slice(blk * RET_BLOCK, (blk + 1) * RET_BLOCK)
        x = x_ref[r, :]
        h = _bf(_rms(x) * gn_ref[...])
        v = _dot(h, win_ref[:, OFF_V:OFF_V + D_A])
        u = _dot(h, win_ref[:, OFF_U:OFF_U + D_A])
        return r, x, h, u, v

    n_blocks = tile // RET_BLOCK
    rows_a = D_A + (B_HEADS - 2) * B_DV
    rows_b = rows_a + B_DV
    started = start_block(0)
    for blk in range(n_blocks):
        r, x, h, u, v = started

        vn = _bf(_head_norm(v, gsgu_ref[...]))
        mixed = _spatial_gate_paired(vn, ws_ref, bs_ref[...])
        za = _dot(h, win_ref[:, OFF_ZA:OFF_ZA + D_A])
        mix_ref[r, 0:D_A] = _bf(u * mixed * _silu(za))

        parts = [_rope_rows(rope_hi_ref, rope_lo_ref, block0 + blk * (RET_BLOCK // ROPE_SPLIT) + j)
                 for j in range(RET_BLOCK // ROPE_SPLIT)]
        cos = jnp.concatenate([p[0] for p in parts], axis=0)
        sin = jnp.concatenate([p[1] for p in parts], axis=0)

        def project(hb):
            c0 = hb * B_DK
            q = _rotary(_dot(h, win_ref[:, OFF_Q + c0:OFF_Q + c0 + B_DK]), cos, sin)
            k = _rotary(_dot(h, win_ref[:, OFF_K + c0:OFF_K + c0 + B_DK]), cos, sin) * (B_DK ** -0.5)
            vb = _dot(h, win_ref[:, OFF_VB + c0:OFF_VB + c0 + B_DV])
            zb = _dot(h, win_ref[:, OFF_ZB + c0:OFF_ZB + c0 + B_DV])
            return q, k, vb, zb

        proj = project(0)
        for hb in range(B_HEADS):
            q, k, vb, zb = proj
            state = [s_ref[hb]]
            front = _retention_front(q, k, vb, state, cdec_ref[hb])
            if hb + 1 < B_HEADS:
                proj = project(hb + 1)
            else:
                y = x + _dot(mix_ref[r, 0:rows_a], wout_ref[0:rows_a, :])
            o, (s_new,) = _retention_back(front, state, dec_ref[hb], rdec_ref[hb], sdec_ref[hb])
            s_ref[hb] = s_new
            mix_ref[r, D_A + hb * B_DV:D_A + (hb + 1) * B_DV] = _bf(_rms(o) * _silu(zb))

        y = y + _dot(mix_ref[r, rows_a:rows_b], wout_ref[rows_a:rows_b, :])
        if blk + 1 < n_blocks:
            started = start_block(blk + 1)
        y = y + _dot(mix_ref[r, rows_b:D_MIX], wout_ref[rows_b:D_MIX, :])
        if final_norm:
            y = _rms(y) * gfin_ref[...]
        y_ref[r, :] = y


class _TileStream:
    def __init__(self, hbm, stage, sems, sem_row, tiles_per_layer, tile_view):
        self.hbm, self.stage, self.sems, self.sem_row = hbm, stage, sems, sem_row
        self.tiles_per_layer, self.tile_view = tiles_per_layer, tile_view
        self.n_tiles = hbm.shape[0] * tiles_per_layer
        self.n_slots = stage.shape[0]
        self.lookahead = self.n_slots - 1

    def _copy(self, n):
        slot = n % self.n_slots
        src = self.tile_view(self.hbm, n // self.tiles_per_layer, n % self.tiles_per_layer)
        return pltpu.make_async_copy(src, self.stage.at[slot], self.sems.at[self.sem_row, slot])

    def prime(self):
        for n in range(self.lookahead):
            self._copy(n).start()

    def take(self, n):
        @pl.when(n + self.lookahead < self.n_tiles)
        def _():
            self._copy(n + self.lookahead).start()

        self._copy(n).wait()
        return self.stage.at[n % self.n_slots]


def _tile_start(t):
    return t * W_TILE if isinstance(t, int) else pl.multiple_of(t * W_TILE, W_TILE)


def _win_tile(hbm, layer, t):
    return hbm.at[layer, :, pl.ds(_tile_start(t), W_TILE)]


def _wout_tile(hbm, layer, t):
    return hbm.at[layer, pl.ds(_tile_start(t), W_TILE), :]


def _state_group(hbm, layer, g):
    return hbm.at[layer, pl.ds(g * SAMPLE_GROUP, SAMPLE_GROUP)]


def _sample_kernel(x_hbm, s_in_hbm, cos_ref, sin_ref, gn_ref, win_hbm, gsgu_ref, ws_ref, bs_ref,
                   dec_ref, rdec_ref, cdec_ref, sdec_ref, wout_hbm, gfin_ref,
                   y_ref, s_out_ref, vn_ref, win_bf_ref, wout_bf_ref,
                   h_ref, p_ref, mix_ref, win_stage, wout_stage, state_stage, sems, x_sem, *, seq):
    layer = pl.program_id(0)
    step = pl.program_id(1)
    group = SAMPLE_GROUP
    n_groups = y_ref.shape[0] // (group * seq)
    tiles_per_block = D_A // W_TILE
    win_stream = _TileStream(win_hbm, win_stage, sems, 0, N_IN_TILES, _win_tile)
    wout_stream = _TileStream(wout_hbm, wout_stage, sems, 1, N_OUT_TILES, _wout_tile)
    state_stream = _TileStream(s_in_hbm, state_stage, sems, 2, n_groups, _state_group)

    @pl.when((layer == 0) & (step == 0))
    def _():
        win_stream.prime()
        wout_stream.prime()
        state_stream.prime()
        x_copy = pltpu.make_async_copy(x_hbm, y_ref, x_sem)
        x_copy.start()
        x_copy.wait()

    @pl.when(step == 0)
    def _():
        h_ref[...] = _bf(_rms(y_ref[...]) * gn_ref[...])

    @pl.when(step < N_IN_TILES)
    def _():
        w = _bf(win_stream.take(layer * N_IN_TILES + step)[...])
        win_bf_ref[...] = w
        p_ref[step] = _dot(h_ref[...], w)

    for t in range(tiles_per_block):
        @pl.when(step == 2 * tiles_per_block + t)
        def _(t=t):
            cols = slice(t * W_TILE, (t + 1) * W_TILE)
            u = p_ref[t]
            v = p_ref[tiles_per_block + t]
            za = p_ref[2 * tiles_per_block + t]
            vn = _head_norm(v, gsgu_ref[:, cols])
            vn_ref[...] = vn
            mixed = _spatial_gate(_bf(vn), ws_ref, t * (W_TILE // A_DH), bs_ref[:, cols])
            mix_ref[:, cols] = _bf(u * mixed * _silu(za))

    @pl.when((step >= N_IN_TILES) & (step < N_IN_TILES + n_groups))
    def _():
        r0 = pl.multiple_of((step - N_IN_TILES) * (group * seq), group * seq)
        rows = pl.ds(r0, group * seq)
        cos = cos_ref[...]
        sin = sin_ref[...]
        s_in = state_stream.take(layer * n_groups + step - N_IN_TILES)
        heads_per_tile = W_TILE // B_DK
        for hb in range(B_HEADS):
            t, c0 = hb // heads_per_tile, (hb % heads_per_tile) * B_DK
            blocks = [p_ref[(3 + i) * tiles_per_block + t, rows, c0:c0 + B_DK] for i in range(4)]
            q = _rotary(blocks[0], cos, sin)
            k = _rotary(blocks[1], cos, sin) * (B_DK ** -0.5)
            states = [s_in[i, hb] for i in range(group)]
            front = _retention_front(q, k, blocks[2], states, cdec_ref[hb])
            o, new_states = _retention_back(front, states, dec_ref[hb], rdec_ref[hb], sdec_ref[hb])
            for i in range(group):
                s_out_ref[i, hb] = new_states[i]
            mix_ref[rows, D_A + hb * B_DV:D_A + (hb + 1) * B_DV] = _bf(_rms(o) * _silu(blocks[3]))

    @pl.when(step >= N_IN_TILES + n_groups)
    def _():
        w = _bf(wout_stream.take(layer * N_OUT_TILES + step - (N_IN_TILES + n_groups))[...])
        wout_bf_ref[...] = w
        for t in range(N_OUT_TILES):
            @pl.when(step == N_IN_TILES + n_groups + t)
            def _(t=t):
                y_ref[...] += _dot(mix_ref[:, t * W_TILE:(t + 1) * W_TILE], w)

    @pl.when((layer == pl.num_programs(0) - 1) & (step == pl.num_programs(1) - 1))
    def _():
        y_ref[...] = _rms(y_ref[...]) * gfin_ref[...]


def _const_spec(shape):
    zeros = (0,) * len(shape)
    return pl.BlockSpec(shape, lambda *_: zeros, pipeline_mode=pl.Buffered(1))


def _layer_spec(shape, layer):
    zeros = (0,) * len(shape)
    return pl.BlockSpec((None,) + shape, lambda i: (layer,) + zeros, pipeline_mode=pl.Buffered(1))


def _decay_tables(length, n_seq=1):
    lg = jnp.log(1.0 - 2.0 ** (-5.0 - jnp.arange(B_HEADS, dtype=jnp.float32)))
    idx = jnp.arange(length, dtype=jnp.float32)
    diff = idx[:, None] - idx[None, :]
    causal = diff >= 0
    dec = jnp.where(causal[None], jnp.exp(lg[:, None, None] * jnp.where(causal, diff, 0.0)[None]), 0.0)
    rdec = jnp.exp(lg[:, None] * (idx[None, :] + 1.0))
    cdec = jnp.exp(lg[:, None] * (length - 1.0 - idx[None, :]))
    sdec = jnp.exp(lg * length)
    dec = jnp.einsum("ab,hij->haibj", jnp.eye(n_seq, dtype=dec.dtype), dec)
    dec = dec.reshape(B_HEADS, n_seq * length, n_seq * length)
    rdec = jnp.broadcast_to(jnp.tile(rdec, (1, n_seq))[:, :, None], (B_HEADS, n_seq * length, LANES))
    cdec = jnp.broadcast_to(jnp.tile(cdec, (1, n_seq))[:, :, None], (B_HEADS, n_seq * length, LANES))
    sdec = jnp.broadcast_to(sdec[:, None, None], (B_HEADS, 1, B_DV))
    return dec, rdec, cdec, sdec


def _rope_tables(start, length):
    inv = ROPE_BASE ** (-jnp.arange(HALF, dtype=jnp.float32) / HALF)
    ang = (start + jnp.arange(length)).astype(jnp.float32)[:, None] * inv[None, :]
    return jnp.cos(ang), jnp.sin(ang)


def _rope_split_tables(length):
    assert length % ROPE_SPLIT == 0
    inv = ROPE_BASE ** (-jnp.arange(HALF, dtype=jnp.float32) / HALF)
    ang_hi = (ROPE_SPLIT * jnp.arange(length // ROPE_SPLIT)).astype(jnp.float32)[:, None] * inv[None, :]
    ang_lo = jnp.arange(ROPE_SPLIT).astype(jnp.float32)[:, None] * inv[None, :]
    return (jnp.stack([jnp.cos(ang_hi), jnp.sin(ang_hi)], axis=0),
            jnp.stack([jnp.cos(ang_lo), jnp.sin(ang_lo)], axis=0))


def _gate_weights(w_s, b_s, c, n_seq=1):
    i = jnp.arange(c)
    mask = (i[None, :] // CHUNK) <= (i[:, None] // CHUNK)
    w = jnp.where(mask, w_s[..., :c, :c], 0.0)
    w = jnp.einsum("ab,...ij->...aibj", jnp.eye(n_seq, dtype=w.dtype), w)
    w = _bf(w.reshape(w.shape[:-4] + (n_seq * c, n_seq * c)))
    bias = jnp.swapaxes(b_s[..., :c], -1, -2)
    bias = jnp.broadcast_to(bias[..., None], bias.shape + (A_DH,)).reshape(bias.shape[:-1] + (D_A,))
    bias = jnp.tile(bias, (1,) * (bias.ndim - 2) + (n_seq, 1))
    return w, bias


def _prompt_layer(layer, x, rope_hi, rope_lo, gn, win, gsgu, ws, bs, decs, wout, gfin, final_norm, in_place):
    seq = x.shape[0]
    tile = PROMPT_TILE
    dec, rdec, cdec, sdec = decs
    row = lambda i: (i, 0)
    kern = functools.partial(_prompt_layer_kernel, final_norm=final_norm)
    return pl.pallas_call(
        kern,
        grid=(seq // tile,),
        in_specs=[
            pl.BlockSpec((tile, D_MODEL), row),
            _const_spec(rope_hi.shape),
            _const_spec(rope_lo.shape),
            _const_spec((1, D_MODEL)),
            _layer_spec((D_MODEL, D_PROJ), layer),
            _const_spec((1, D_A)),
            _const_spec((A_HEADS // 2, A_CHUNK, 2 * A_CHUNK)),
            _const_spec((A_CHUNK, D_A)),
            _const_spec((B_HEADS, RET_BLOCK, RET_BLOCK)),
            _const_spec((B_HEADS, RET_BLOCK, LANES)),
            _const_spec((B_HEADS, RET_BLOCK, LANES)),
            _const_spec((B_HEADS, 1, B_DV)),
            _layer_spec((D_MIX, D_MODEL), layer),
            _const_spec((1, D_MODEL)),
        ],
        out_specs=[
            pl.BlockSpec((tile, D_MODEL), row),
            pl.BlockSpec((B_HEADS, B_DK, B_DV), lambda i: (0, 0, 0)),
        ],
        out_shape=[
            jax.ShapeDtypeStruct((seq, D_MODEL), jnp.float32),
            jax.ShapeDtypeStruct((B_HEADS, B_DK, B_DV), jnp.float32),
        ],
        scratch_shapes=[pltpu.VMEM((tile, D_MIX), jnp.bfloat16)],
        input_output_aliases={0: 0} if in_place else {},
        compiler_params=pltpu.CompilerParams(
            dimension_semantics=("arbitrary",), vmem_limit_bytes=VMEM_LIMIT_BYTES),
        name="prompt_layer",
    )(x, rope_hi, rope_lo, gn, win, gsgu, ws, bs, dec, rdec, cdec, sdec, wout, gfin)


def _sample_trunk(x, state, cos, sin, gn, w_in, gsgu, ws, bs, decs, w_out, gfin, seq):
    rows = x.shape[0]
    nb = rows // seq
    group = SAMPLE_GROUP
    n_groups = nb // group
    n_steps = N_IN_TILES + n_groups + N_OUT_TILES
    dec, rdec, cdec, sdec = decs
    gate_rows = ws.shape[-1]
    tiles_per_block = D_A // W_TILE

    def per_layer(shape):
        zeros = (0,) * len(shape)
        return pl.BlockSpec((None,) + shape, lambda l, s: (l,) + zeros)

    def state_map(l, s):
        return (l, jnp.clip(s - N_IN_TILES, 0, n_groups - 1), 0, 0, 0)

    state_spec = pl.BlockSpec((None, group, B_HEADS, B_DK, B_DV), state_map)
    kern = functools.partial(_sample_kernel, seq=seq)
    return pl.pallas_call(
        kern,
        grid=(DEPTH, n_steps),
        in_specs=[
            pl.BlockSpec(memory_space=pl.ANY),
            pl.BlockSpec(memory_space=pl.ANY),
            _const_spec((group * seq, HALF)),
            _const_spec((group * seq, HALF)),
            per_layer((1, D_MODEL)),
            pl.BlockSpec(memory_space=pl.ANY),
            per_layer((1, D_A)),
            per_layer((A_HEADS, gate_rows, gate_rows)),
            per_layer((gate_rows, D_A)),
            _const_spec((B_HEADS, group * seq, group * seq)),
            _const_spec((B_HEADS, group * seq, LANES)),
            _const_spec((B_HEADS, group * seq, LANES)),
            _const_spec((B_HEADS, 1, B_DV)),
            pl.BlockSpec(memory_space=pl.ANY),
            _const_spec((1, D_MODEL)),
        ],
        out_specs=[
            pl.BlockSpec((rows, D_MODEL), lambda l, s: (0, 0)),
            state_spec,
            pl.BlockSpec((None, rows, W_TILE),
                         lambda l, s: (l, 0, jnp.clip(s - 2 * tiles_per_block, 0, tiles_per_block - 1))),
            pl.BlockSpec((None, D_MODEL, W_TILE), lambda l, s: (l, 0, jnp.minimum(s, N_IN_TILES - 1))),
            pl.BlockSpec((None, W_TILE, D_MODEL),
                         lambda l, s: (l, jnp.clip(s - N_IN_TILES - n_groups, 0, N_OUT_TILES - 1), 0)),
        ],
        out_shape=[
            jax.ShapeDtypeStruct((rows, D_MODEL), jnp.float32),
            jax.ShapeDtypeStruct(state.shape, jnp.float32),
            jax.ShapeDtypeStruct((DEPTH, rows, D_A), jnp.float32),
            jax.ShapeDtypeStruct(w_in.shape, jnp.bfloat16),
            jax.ShapeDtypeStruct(w_out.shape, jnp.bfloat16),
        ],
        scratch_shapes=[
            pltpu.VMEM((rows, D_MODEL), jnp.bfloat16),
            pltpu.VMEM((N_IN_TILES, rows, W_TILE), jnp.float32),
            pltpu.VMEM((rows, D_MIX), jnp.bfloat16),
            pltpu.VMEM((W_IN_STREAM_SLOTS, D_MODEL, W_TILE), jnp.float32),
            pltpu.VMEM((W_OUT_STREAM_SLOTS, W_TILE, D_MODEL), jnp.float32),
            pltpu.VMEM((STATE_STREAM_SLOTS, group, B_HEADS, B_DK, B_DV), jnp.float32),
            pltpu.SemaphoreType.DMA((3, max(W_IN_STREAM_SLOTS, W_OUT_STREAM_SLOTS, STATE_STREAM_SLOTS))),
            pltpu.SemaphoreType.DMA(()),
        ],
        compiler_params=pltpu.CompilerParams(
            dimension_semantics=("arbitrary", "arbitrary"), vmem_limit_bytes=VMEM_LIMIT_BYTES),
        name="sample_trunk",
    )(x, state, cos, sin, gn, w_in, gsgu, ws, bs, dec, rdec, cdec, sdec, w_out, gfin)


def kernel(x_prompt, x_sample, state_ret, g_norm, w_in, g_sgu, w_s, b_s, w_out, g_final):
    batch, seq, _ = x_prompt.shape
    dec_batch, dec_seq, _ = x_sample.shape
    assert batch == 1 and seq % PROMPT_TILE == 0
    assert PROMPT_TILE % RET_BLOCK == 0 and RET_BLOCK % A_CHUNK == 0 and RET_BLOCK % ROPE_SPLIT == 0
    assert dec_seq <= CHUNK and dec_batch % SAMPLE_GROUP == 0
    assert (dec_batch * dec_seq) % (SAMPLE_GATE_GROUP * dec_seq) == 0
    assert D_A % W_TILE == 0 and W_TILE % B_DK == 0 and W_TILE % A_DH == 0

    gfin = g_final.reshape(1, D_MODEL)
    gn = g_norm.reshape(DEPTH, 1, D_MODEL)
    gsgu = g_sgu.reshape(DEPTH, 1, D_A)

    cos_s, sin_s = _rope_tables(PAST_LEN, dec_seq)
    cos_s, sin_s = jnp.tile(cos_s, (SAMPLE_GROUP, 1)), jnp.tile(sin_s, (SAMPLE_GROUP, 1))
    ws_s, bs_s = _gate_weights(w_s, b_s, dec_seq, SAMPLE_GATE_GROUP)
    hs, new_ret_sample, vn, win_bf, wout_bf = _sample_trunk(
        x_sample.reshape(dec_batch * dec_seq, D_MODEL), state_ret, cos_s, sin_s, gn, w_in, gsgu, ws_s, bs_s,
        _decay_tables(dec_seq, SAMPLE_GROUP), w_out, gfin, dec_seq)

    rope_hi, rope_lo = _rope_split_tables(seq)
    decs_p = _decay_tables(RET_BLOCK)
    ws_p, bs_p = _gate_weights(w_s, b_s, A_CHUNK)
    ws_p = ws_p.reshape(DEPTH, A_HEADS // 2, 2, A_CHUNK, A_CHUNK).transpose(0, 1, 3, 2, 4)
    ws_p = ws_p.reshape(DEPTH, A_HEADS // 2, A_CHUNK, 2 * A_CHUNK)
    hp = x_prompt.reshape(seq, D_MODEL)
    ret_p = []
    for l in range(DEPTH):
        hp, sp = _prompt_layer(l, hp, rope_hi, rope_lo, gn[l], win_bf, gsgu[l], ws_p[l], bs_p[l], decs_p, wout_bf,
                               gfin, l == DEPTH - 1, l > 0)
        ret_p.append(sp.reshape(batch, B_HEADS, B_DK, B_DV))

    y_prompt = hp.reshape(batch, seq, D_MODEL)
    y_sample = hs.reshape(dec_batch, dec_seq, D_MODEL)
    new_chunk_v = vn.reshape(DEPTH, dec_batch, dec_seq, A_HEADS, A_DH)
    return (y_prompt, y_sample, jnp.stack(ret_p, axis=0), new_ret_sample, new_chunk_v)
```

```python
import functools

import jax
import jax.numpy as jnp
from jax import lax
from jax.experimental import pallas as pl
from jax.experimental.pallas import tpu as pltpu

D_MODEL = 1024
DEPTH = 4
PAST_LEN = 4096
CHUNK = 64
D_MIX = 2 * D_MODEL
D_A = D_MIX // 2
A_HEADS = 8
A_DH = D_A // A_HEADS
A_CHUNK = 128
D_B = D_MIX - D_A
B_HEADS = 4
B_DK = D_B // B_HEADS
B_DV = D_B // B_HEADS
D_PROJ = 3 * D_A + 4 * D_B
ROPE_BASE = 10000.0
EPS = 1e-6
HALF = B_DK // 2
LANES = 128
ROPE_SPLIT = 128

OFF_U, OFF_V, OFF_ZA = 0, D_A, 2 * D_A
OFF_Q, OFF_K, OFF_VB, OFF_ZB = 3 * D_A, 3 * D_A + D_B, 3 * D_A + 2 * D_B, 3 * D_A + 3 * D_B

RET_BLOCK = 256
PROMPT_TILE = 512

SAMPLE_GROUP = 2
SAMPLE_GATE_GROUP = 4
W_TILE = 512
N_IN_TILES = D_PROJ // W_TILE
N_OUT_TILES = D_MIX // W_TILE
W_IN_STREAM_SLOTS = 4
W_OUT_STREAM_SLOTS = 3
STATE_STREAM_SLOTS = 3

VMEM_LIMIT_BYTES = 56 * 1024 * 1024


def _bf(x):
    return x.astype(jnp.bfloat16)


def _dot(a, b):
    return jnp.dot(a, b, preferred_element_type=jnp.float32)


def _dot_nt(a, b):
    return lax.dot_general(a, b, (((1,), (1,)), ((), ())), preferred_element_type=jnp.float32)


def _dot_tn(a, b):
    return lax.dot_general(a, b, (((0,), (0,)), ((), ())), preferred_element_type=jnp.float32)


def _rms(x):
    return x * lax.rsqrt(jnp.mean(x * x, axis=-1, keepdims=True) + EPS)


def _silu(z):
    return z * (0.5 * jnp.tanh(0.5 * z) + 0.5)


def _rotary(x, cos, sin):
    x1, x2 = x[:, :HALF], x[:, HALF:]
    return jnp.concatenate([x1 * cos - x2 * sin, x1 * sin + x2 * cos], axis=-1)


def _wide(t):
    return jnp.concatenate([t, t], axis=-1)


def _head_norm(v, gsgu):
    parts = [_rms(v[:, h * A_DH:(h + 1) * A_DH]) for h in range(v.shape[1] // A_DH)]
    return jnp.concatenate(parts, axis=-1) * gsgu


def _spatial_gate(vn, ws_ref, head0, bias):
    c = ws_ref.shape[-1]
    rows = []
    for r in range(vn.shape[0] // c):
        cols = [_dot(ws_ref[head0 + h], vn[r * c:(r + 1) * c, h * A_DH:(h + 1) * A_DH])
                for h in range(vn.shape[1] // A_DH)]
        rows.append(jnp.concatenate(cols, axis=-1) + bias)
    return jnp.concatenate(rows, axis=0)


def _spatial_gate_paired(vn, wpair_ref, bias):
    c = wpair_ref.shape[1]
    n_chunks = vn.shape[0] // c
    zeros = jnp.zeros((c, A_DH), vn.dtype)
    outs = []
    for j in range(wpair_ref.shape[0]):
        top, bottom = [], []
        for r in range(n_chunks):
            top += [vn[r * c:(r + 1) * c, (2 * j) * A_DH:(2 * j + 1) * A_DH], zeros]
            bottom += [zeros, vn[r * c:(r + 1) * c, (2 * j + 1) * A_DH:(2 * j + 2) * A_DH]]
        rhs = jnp.concatenate([jnp.concatenate(top, axis=1), jnp.concatenate(bottom, axis=1)], axis=0)
        outs.append(_dot(wpair_ref[j], rhs))
    rows = [jnp.concatenate([o[:, r * 2 * A_DH:(r + 1) * 2 * A_DH] for o in outs], axis=1) + bias
            for r in range(n_chunks)]
    return jnp.concatenate(rows, axis=0)


def _retention_front(q, k, v, states, cdec):
    n = len(states)
    seq = q.shape[0] // n
    qb, kb, vb = _bf(q), _bf(k), _bf(v)
    kw = _bf(k * _wide(cdec))
    scores = _dot_nt(qb, kb)
    cross = [_dot(qb[i * seq:(i + 1) * seq], _bf(states[i])) for i in range(n)]
    return scores, cross, kw, vb


def _retention_back(front, states, dec, rdec, sdec):
    scores, cross, kw, vb = front
    n = len(states)
    seq = vb.shape[0] // n
    updates = [_dot_tn(kw[i * seq:(i + 1) * seq], vb[i * seq:(i + 1) * seq]) for i in range(n)]
    intra = _dot(_bf(scores * dec), vb)
    cross = (cross[0] if n == 1 else jnp.concatenate(cross, axis=0)) * _wide(rdec)
    new_states = [sdec * states[i] + updates[i] for i in range(n)]
    return intra + cross, new_states


def _rope_rows(rope_hi_ref, rope_lo_ref, block):
    ch = rope_hi_ref[0, pl.ds(block, 1), :]
    sh = rope_hi_ref[1, pl.ds(block, 1), :]
    cl = rope_lo_ref[0]
    sl = rope_lo_ref[1]
    return ch * cl - sh * sl, sh * cl + ch * sl


def _prompt_layer_kernel(x_ref, rope_hi_ref, rope_lo_ref, gn_ref, win_ref, gsgu_ref, ws_ref, bs_ref,
                         dec_ref, rdec_ref, cdec_ref, sdec_ref, wout_ref, gfin_ref,
                         y_ref, s_ref, mix_ref, *, final_norm):
    tile = x_ref.shape[0]
    block0 = pl.program_id(0) * (tile // ROPE_SPLIT)

    @pl.when(pl.program_id(0) == 0)
    def _():
        s_ref[...] = jnp.zeros_like(s_ref)

    def start_block(blk):
        r = slice(blk * RET_BLOCK, (blk + 1) * RET_BLOCK)
        x = x_ref[r, :]
        h = _bf(_rms(x) * gn_ref[...])
        v = _dot(h, win_ref[:, OFF_V:OFF_V + D_A])
        u = _dot(h, win_ref[:, OFF_U:OFF_U + D_A])
        return r, x, h, u, v

    n_blocks = tile // RET_BLOCK
    rows_a = D_A + (B_HEADS - 2) * B_DV
    rows_b = rows_a + B_DV
    started = start_block(0)
    for blk in range(n_blocks):
        r, x, h, u, v = started

        vn = _bf(_head_norm(v, gsgu_ref[...]))
        mixed = _spatial_gate_paired(vn, ws_ref, bs_ref[...])
        za = _dot(h, win_ref[:, OFF_ZA:OFF_ZA + D_A])
        mix_ref[r, 0:D_A] = _bf(u * mixed * _silu(za))

        parts = [_rope_rows(rope_hi_ref, rope_lo_ref, block0 + blk * (RET_BLOCK // ROPE_SPLIT) + j)
                 for j in range(RET_BLOCK // ROPE_SPLIT)]
        cos = jnp.concatenate([p[0] for p in parts], axis=0)
        sin = jnp.concatenate([p[1] for p in parts], axis=0)

        def project(hb):
            c0 = hb * B_DK
            k = _rotary(_dot(h, win_ref[:, OFF_K + c0:OFF_K + c0 + B_DK]), cos, sin) * (B_DK ** -0.5)
            q = _rotary(_dot(h, win_ref[:, OFF_Q + c0:OFF_Q + c0 + B_DK]), cos, sin)
            vb = _dot(h, win_ref[:, OFF_VB + c0:OFF_VB + c0 + B_DV])
            zb = _dot(h, win_ref[:, OFF_ZB + c0:OFF_ZB + c0 + B_DV])
            return q, k, vb, zb

        proj = project(0)
        for hb in range(B_HEADS):
            q, k, vb, zb = proj
            state = [s_ref[hb]]
            front = _retention_front(q, k, vb, state, cdec_ref[hb])
            if hb + 1 < B_HEADS:
                proj = project(hb + 1)
            else:
                y = x + _dot(mix_ref[r, 0:rows_a], wout_ref[0:rows_a, :])
            o, (s_new,) = _retention_back(front, state, dec_ref[hb], rdec_ref[hb], sdec_ref[hb])
            s_ref[hb] = s_new
            mix_ref[r, D_A + hb * B_DV:D_A + (hb + 1) * B_DV] = _bf(_rms(o) * _silu(zb))

        y = y + _dot(mix_ref[r, rows_a:rows_b], wout_ref[rows_a:rows_b, :])
        if blk + 1 < n_blocks:
            started = start_block(blk + 1)
        y = y + _dot(mix_ref[r, rows_b:D_MIX], wout_ref[rows_b:D_MIX, :])
        if final_norm:
            y = _rms(y) * gfin_ref[...]
        y_ref[r, :] = y


class _TileStream:
    def __init__(self, hbm, stage, sems, sem_row, tiles_per_layer, tile_view):
        self.hbm, self.stage, self.sems, self.sem_row = hbm, stage, sems, sem_row
        self.tiles_per_layer, self.tile_view = tiles_per_layer, tile_view
        self.n_tiles = hbm.shape[0] * tiles_per_layer
        self.n_slots = stage.shape[0]
        self.lookahead = self.n_slots - 1

    def _copy(self, n):
        slot = n % self.n_slots
        src = self.tile_view(self.hbm, n // self.tiles_per_layer, n % self.tiles_per_layer)
        return pltpu.make_async_copy(src, self.stage.at[slot], self.sems.at[self.sem_row, slot])

    def prime(self):
        for n in range(self.lookahead):
            self._copy(n).start()

    def take(self, n):
        @pl.when(n + self.lookahead < self.n_tiles)
        def _():
            self._copy(n + self.lookahead).start()

        self._copy(n).wait()
        return self.stage.at[n % self.n_slots]


def _tile_start(t):
    return t * W_TILE if isinstance(t, int) else pl.multiple_of(t * W_TILE, W_TILE)


def _win_tile(hbm, layer, t):
    return hbm.at[layer, :, pl.ds(_tile_start(t), W_TILE)]


def _wout_tile(hbm, layer, t):
    return hbm.at[layer, pl.ds(_tile_start(t), W_TILE), :]


def _state_group(hbm, layer, g):
    return hbm.at[layer, pl.ds(g * SAMPLE_GROUP, SAMPLE_GROUP)]


def _sample_kernel(x_hbm, s_in_hbm, cos_ref, sin_ref, gn_ref, win_hbm, gsgu_ref, ws_ref, bs_ref,
                   dec_ref, rdec_ref, cdec_ref, sdec_ref, wout_hbm, gfin_ref,
                   y_ref, s_out_ref, vn_ref, win_bf_ref, wout_bf_ref,
                   h_ref, p_ref, mix_ref, win_stage, wout_stage, state_stage, sems, x_sem, *, seq):
    layer = pl.program_id(0)
    step = pl.program_id(1)
    group = SAMPLE_GROUP
    n_groups = y_ref.shape[0] // (group * seq)
    tiles_per_block = D_A // W_TILE
    win_stream = _TileStream(win_hbm, win_stage, sems, 0, N_IN_TILES, _win_tile)
    wout_stream = _TileStream(wout_hbm, wout_stage, sems, 1, N_OUT_TILES, _wout_tile)
    state_stream = _TileStream(s_in_hbm, state_stage, sems, 2, n_groups, _state_group)

    @pl.when((layer == 0) & (step == 0))
    def _():
        win_stream.prime()
        wout_stream.prime()
        state_stream.prime()
        x_copy = pltpu.make_async_copy(x_hbm, y_ref, x_sem)
        x_copy.start()
        x_copy.wait()

    @pl.when(step == 0)
    def _():
        h_ref[...] = _bf(_rms(y_ref[...]) * gn_ref[...])

    @pl.when(step < N_IN_TILES)
    def _():
        w = _bf(win_stream.take(layer * N_IN_TILES + step)[...])
        win_bf_ref[...] = w
        p_ref[step] = _dot(h_ref[...], w)

    for t in range(tiles_per_block):
        @pl.when(step == 2 * tiles_per_block + t)
        def _(t=t):
            cols = slice(t * W_TILE, (t + 1) * W_TILE)
            u = p_ref[t]
            v = p_ref[tiles_per_block + t]
            za = p_ref[2 * tiles_per_block + t]
            vn = _head_norm(v, gsgu_ref[:, cols])
            vn_ref[...] = vn
            mixed = _spatial_gate(_bf(vn), ws_ref, t * (W_TILE // A_DH), bs_ref[:, cols])
            mix_ref[:, cols] = _bf(u * mixed * _silu(za))

    @pl.when((step >= N_IN_TILES) & (step < N_IN_TILES + n_groups))
    def _():
        r0 = pl.multiple_of((step - N_IN_TILES) * (group * seq), group * seq)
        rows = pl.ds(r0, group * seq)
        cos = cos_ref[...]
        sin = sin_ref[...]
        s_in = state_stream.take(layer * n_groups + step - N_IN_TILES)
        heads_per_tile = W_TILE // B_DK
        for hb in range(B_HEADS):
            t, c0 = hb // heads_per_tile, (hb % heads_per_tile) * B_DK
            blocks = [p_ref[(3 + i) * tiles_per_block + t, rows, c0:c0 + B_DK] for i in range(4)]
            q = _rotary(blocks[0], cos, sin)
            k = _rotary(blocks[1], cos, sin) * (B_DK ** -0.5)
            states = [s_in[i, hb] for i in range(group)]
            front = _retention_front(q, k, blocks[2], states, cdec_ref[hb])
            o, new_states = _retention_back(front, states, dec_ref[hb], rdec_ref[hb], sdec_ref[hb])
            for i in range(group):
                s_out_ref[i, hb] = new_states[i]
            mix_ref[rows, D_A + hb * B_DV:D_A + (hb + 1) * B_DV] = _bf(_rms(o) * _silu(blocks[3]))

    @pl.when(step >= N_IN_TILES + n_groups)
    def _():
        w = _bf(wout_stream.take(layer * N_OUT_TILES + step - (N_IN_TILES + n_groups))[...])
        wout_bf_ref[...] = w
        for t in range(N_OUT_TILES):
            @pl.when(step == N_IN_TILES + n_groups + t)
            def _(t=t):
                y_ref[...] += _dot(mix_ref[:, t * W_TILE:(t + 1) * W_TILE], w)

    @pl.when((layer == pl.num_programs(0) - 1) & (step == pl.num_programs(1) - 1))
    def _():
        y_ref[...] = _rms(y_ref[...]) * gfin_ref[...]


def _const_spec(shape):
    zeros = (0,) * len(shape)
    return pl.BlockSpec(shape, lambda *_: zeros, pipeline_mode=pl.Buffered(1))


def _layer_spec(shape, layer):
    zeros = (0,) * len(shape)
    return pl.BlockSpec((None,) + shape, lambda i: (layer,) + zeros, pipeline_mode=pl.Buffered(1))


def _decay_tables(length, n_seq=1):
    lg = jnp.log(1.0 - 2.0 ** (-5.0 - jnp.arange(B_HEADS, dtype=jnp.float32)))
    idx = jnp.arange(length, dtype=jnp.float32)
    diff = idx[:, None] - idx[None, :]
    causal = diff >= 0
    dec = jnp.where(causal[None], jnp.exp(lg[:, None, None] * jnp.where(causal, diff, 0.0)[None]), 0.0)
    rdec = jnp.exp(lg[:, None] * (idx[None, :] + 1.0))
    cdec = jnp.exp(lg[:, None] * (length - 1.0 - idx[None, :]))
    sdec = jnp.exp(lg * length)
    dec = jnp.einsum("ab,hij->haibj", jnp.eye(n_seq, dtype=dec.dtype), dec)
    dec = dec.reshape(B_HEADS, n_seq * length, n_seq * length)
    rdec = jnp.broadcast_to(jnp.tile(rdec, (1, n_seq))[:, :, None], (B_HEADS, n_seq * length, LANES))
    cdec = jnp.broadcast_to(jnp.tile(cdec, (1, n_seq))[:, :, None], (B_HEADS, n_seq * length, LANES))
    sdec = jnp.broadcast_to(sdec[:, None, None], (B_HEADS, 1, B_DV))
    return dec, rdec, cdec, sdec


def _rope_tables(start, length):
    inv = ROPE_BASE ** (-jnp.arange(HALF, dtype=jnp.float32) / HALF)
    ang = (start + jnp.arange(length)).astype(jnp.float32)[:, None] * inv[None, :]
    return jnp.cos(ang), jnp.sin(ang)


def _rope_split_tables(length):
    assert length % ROPE_SPLIT == 0
    inv = ROPE_BASE ** (-jnp.arange(HALF, dtype=jnp.float32) / HALF)
    ang_hi = (ROPE_SPLIT * jnp.arange(length // ROPE_SPLIT)).astype(jnp.float32)[:, None] * inv[None, :]
    ang_lo = jnp.arange(ROPE_SPLIT).astype(jnp.float32)[:, None] * inv[None, :]
    return (jnp.stack([jnp.cos(ang_hi), jnp.sin(ang_hi)], axis=0),
            jnp.stack([jnp.cos(ang_lo), jnp.sin(ang_lo)], axis=0))


def _gate_weights(w_s, b_s, c, n_seq=1):
    i = jnp.arange(c)
    mask = (i[None, :] // CHUNK) <= (i[:, None] // CHUNK)
    w = jnp.where(mask, w_s[..., :c, :c], 0.0)
    w = jnp.einsum("ab,...ij->...aibj", jnp.eye(n_seq, dtype=w.dtype), w)
    w = _bf(w.reshape(w.shape[:-4] + (n_seq * c, n_seq * c)))
    bias = jnp.swapaxes(b_s[..., :c], -1, -2)
    bias = jnp.broadcast_to(bias[..., None], bias.shape + (A_DH,)).reshape(bias.shape[:-1] + (D_A,))
    bias = jnp.tile(bias, (1,) * (bias.ndim - 2) + (n_seq, 1))
    return w, bias


def _prompt_layer(layer, x, rope_hi, rope_lo, gn, win, gsgu, ws, bs, decs, wout, gfin, final_norm, in_place):
    seq = x.shape[0]
    tile = PROMPT_TILE
    dec, rdec, cdec, sdec = decs
    row = lambda i: (i, 0)
    kern = functools.partial(_prompt_layer_kernel, final_norm=final_norm)
    return pl.pallas_call(
        kern,
        grid=(seq // tile,),
        in_specs=[
            pl.BlockSpec((tile, D_MODEL), row),
            _const_spec(rope_hi.shape),
            _const_spec(rope_lo.shape),
            _const_spec((1, D_MODEL)),
            _layer_spec((D_MODEL, D_PROJ), layer),
            _const_spec((1, D_A)),
            _const_spec((A_HEADS // 2, A_CHUNK, 2 * A_CHUNK)),
            _const_spec((A_CHUNK, D_A)),
            _const_spec((B_HEADS, RET_BLOCK, RET_BLOCK)),
            _const_spec((B_HEADS, RET_BLOCK, LANES)),
            _const_spec((B_HEADS, RET_BLOCK, LANES)),
            _const_spec((B_HEADS, 1, B_DV)),
            _layer_spec((D_MIX, D_MODEL), layer),
            _const_spec((1, D_MODEL)),
        ],
        out_specs=[
            pl.BlockSpec((tile, D_MODEL), row),
            pl.BlockSpec((B_HEADS, B_DK, B_DV), lambda i: (0, 0, 0)),
        ],
        out_shape=[
            jax.ShapeDtypeStruct((seq, D_MODEL), jnp.float32),
            jax.ShapeDtypeStruct((B_HEADS, B_DK, B_DV), jnp.float32),
        ],
        scratch_shapes=[pltpu.VMEM((tile, D_MIX), jnp.bfloat16)],
        input_output_aliases={0: 0} if in_place else {},
        compiler_params=pltpu.CompilerParams(
            dimension_semantics=("arbitrary",), vmem_limit_bytes=VMEM_LIMIT_BYTES),
        name="prompt_layer",
    )(x, rope_hi, rope_lo, gn, win, gsgu, ws, bs, dec, rdec, cdec, sdec, wout, gfin)


def _sample_trunk(x, state, cos, sin, gn, w_in, gsgu, ws, bs, decs, w_out, gfin, seq):
    rows = x.shape[0]
    nb = rows // seq
    group = SAMPLE_GROUP
    n_groups = nb // group
    n_steps = N_IN_TILES + n_groups + N_OUT_TILES
    dec, rdec, cdec, sdec = decs
    gate_rows = ws.shape[-1]
    tiles_per_block = D_A // W_TILE

    def per_layer(shape):
        zeros = (0,) * len(shape)
        return pl.BlockSpec((None,) + shape, lambda l, s: (l,) + zeros)

    def state_map(l, s):
        return (l, jnp.clip(s - N_IN_TILES, 0, n_groups - 1), 0, 0, 0)

    state_spec = pl.BlockSpec((None, group, B_HEADS, B_DK, B_DV), state_map)
    kern = functools.partial(_sample_kernel, seq=seq)
    return pl.pallas_call(
        kern,
        grid=(DEPTH, n_steps),
        in_specs=[
            pl.BlockSpec(memory_space=pl.ANY),
            pl.BlockSpec(memory_space=pl.ANY),
            _const_spec((group * seq, HALF)),
            _const_spec((group * seq, HALF)),
            per_layer((1, D_MODEL)),
            pl.BlockSpec(memory_space=pl.ANY),
            per_layer((1, D_A)),
            per_layer((A_HEADS, gate_rows, gate_rows)),
            per_layer((gate_rows, D_A)),
            _const_spec((B_HEADS, group * seq, group * seq)),
            _const_spec((B_HEADS, group * seq, LANES)),
            _const_spec((B_HEADS, group * seq, LANES)),
            _const_spec((B_HEADS, 1, B_DV)),
            pl.BlockSpec(memory_space=pl.ANY),
            _const_spec((1, D_MODEL)),
        ],
        out_specs=[
            pl.BlockSpec((rows, D_MODEL), lambda l, s: (0, 0)),
            state_spec,
            pl.BlockSpec((None, rows, W_TILE),
                         lambda l, s: (l, 0, jnp.clip(s - 2 * tiles_per_block, 0, tiles_per_block - 1))),
            pl.BlockSpec((None, D_MODEL, W_TILE), lambda l, s: (l, 0, jnp.minimum(s, N_IN_TILES - 1))),
            pl.BlockSpec((None, W_TILE, D_MODEL),
                         lambda l, s: (l, jnp.clip(s - N_IN_TILES - n_groups, 0, N_OUT_TILES - 1), 0)),
        ],
        out_shape=[
            jax.ShapeDtypeStruct((rows, D_MODEL), jnp.float32),
            jax.ShapeDtypeStruct(state.shape, jnp.float32),
            jax.ShapeDtypeStruct((DEPTH, rows, D_A), jnp.float32),
            jax.ShapeDtypeStruct(w_in.shape, jnp.bfloat16),
            jax.ShapeDtypeStruct(w_out.shape, jnp.bfloat16),
        ],
        scratch_shapes=[
            pltpu.VMEM((rows, D_MODEL), jnp.bfloat16),
            pltpu.VMEM((N_IN_TILES, rows, W_TILE), jnp.float32),
            pltpu.VMEM((rows, D_MIX), jnp.bfloat16),
            pltpu.VMEM((W_IN_STREAM_SLOTS, D_MODEL, W_TILE), jnp.float32),
            pltpu.VMEM((W_OUT_STREAM_SLOTS, W_TILE, D_MODEL), jnp.float32),
            pltpu.VMEM((STATE_STREAM_SLOTS, group, B_HEADS, B_DK, B_DV), jnp.float32),
            pltpu.SemaphoreType.DMA((3, max(W_IN_STREAM_SLOTS, W_OUT_STREAM_SLOTS, STATE_STREAM_SLOTS))),
            pltpu.SemaphoreType.DMA(()),
        ],
        compiler_params=pltpu.CompilerParams(
            dimension_semantics=("arbitrary", "arbitrary"), vmem_limit_bytes=VMEM_LIMIT_BYTES),
        name="sample_trunk",
    )(x, state, cos, sin, gn, w_in, gsgu, ws, bs, dec, rdec, cdec, sdec, w_out, gfin)


def kernel(x_prompt, x_sample, state_ret, g_norm, w_in, g_sgu, w_s, b_s, w_out, g_final):
    batch, seq, _ = x_prompt.shape
    dec_batch, dec_seq, _ = x_sample.shape
    assert batch == 1 and seq % PROMPT_TILE == 0
    assert PROMPT_TILE % RET_BLOCK == 0 and RET_BLOCK % A_CHUNK == 0 and RET_BLOCK % ROPE_SPLIT == 0
    assert dec_seq <= CHUNK and dec_batch % SAMPLE_GROUP == 0
    assert (dec_batch * dec_seq) % (SAMPLE_GATE_GROUP * dec_seq) == 0
    assert D_A % W_TILE == 0 and W_TILE % B_DK == 0 and W_TILE % A_DH == 0

    gfin = g_final.reshape(1, D_MODEL)
    gn = g_norm.reshape(DEPTH, 1, D_MODEL)
    gsgu = g_sgu.reshape(DEPTH, 1, D_A)

    cos_s, sin_s = _rope_tables(PAST_LEN, dec_seq)
    cos_s, sin_s = jnp.tile(cos_s, (SAMPLE_GROUP, 1)), jnp.tile(sin_s, (SAMPLE_GROUP, 1))
    ws_s, bs_s = _gate_weights(w_s, b_s, dec_seq, SAMPLE_GATE_GROUP)
    hs, new_ret_sample, vn, win_bf, wout_bf = _sample_trunk(
        x_sample.reshape(dec_batch * dec_seq, D_MODEL), state_ret, cos_s, sin_s, gn, w_in, gsgu, ws_s, bs_s,
        _decay_tables(dec_seq, SAMPLE_GROUP), w_out, gfin, dec_seq)

    rope_hi, rope_lo = _rope_split_tables(seq)
    decs_p = _decay_tables(RET_BLOCK)
    ws_p, bs_p = _gate_weights(w_s, b_s, A_CHUNK)
    ws_p = ws_p.reshape(DEPTH, A_HEADS // 2, 2, A_CHUNK, A_CHUNK).transpose(0, 1, 3, 2, 4)
    ws_p = ws_p.reshape(DEPTH, A_HEADS // 2, A_CHUNK, 2 * A_CHUNK)
    hp = x_prompt.reshape(seq, D_MODEL)
    ret_p = []
    for l in range(DEPTH):
        hp, sp = _prompt_layer(l, hp, rope_hi, rope_lo, gn[l], win_bf, gsgu[l], ws_p[l], bs_p[l], decs_p, wout_bf,
                               gfin, l == DEPTH - 1, l > 0)
        ret_p.append(sp.reshape(batch, B_HEADS, B_DK, B_DV))

    y_prompt = hp.reshape(batch, seq, D_MODEL)
    y_sample = hs.reshape(dec_batch, dec_seq, D_MODEL)
    new_chunk_v = vn.reshape(DEPTH, dec_batch, dec_seq, A_HEADS, A_DH)
    return (y_prompt, y_sample, jnp.stack(ret_p, axis=0), new_ret_sample, new_chunk_v)
```

```python
import functools

import jax
import jax.numpy as jnp
from jax import lax
from jax.experimental import pallas as pl
from jax.experimental.pallas import tpu as pltpu

D_MODEL = 1024
DEPTH = 4
PAST_LEN = 4096
CHUNK = 64
D_MIX = 2 * D_MODEL
D_A = D_MIX // 2
A_HEADS = 8
A_DH = D_A // A_HEADS
A_CHUNK = 128
D_B = D_MIX - D_A
B_HEADS = 4
B_DK = D_B // B_HEADS
B_DV = D_B // B_HEADS
D_PROJ = 3 * D_A + 4 * D_B
ROPE_BASE = 10000.0
EPS = 1e-6
HALF = B_DK // 2
LANES = 128
ROPE_SPLIT = 128

OFF_U, OFF_V, OFF_ZA = 0, D_A, 2 * D_A
OFF_Q, OFF_K, OFF_VB, OFF_ZB = 3 * D_A, 3 * D_A + D_B, 3 * D_A + 2 * D_B, 3 * D_A + 3 * D_B

RET_BLOCK = 256
PROMPT_TILE = 512

SAMPLE_GROUP = 2
SAMPLE_GATE_GROUP = 4
W_TILE = 512
N_IN_TILES = D_PROJ // W_TILE
N_OUT_TILES = D_MIX // W_TILE
W_IN_STREAM_SLOTS = 4
W_OUT_STREAM_SLOTS = 3
STATE_STREAM_SLOTS = 3

VMEM_LIMIT_BYTES = 56 * 1024 * 1024


def _bf(x):
    return x.astype(jnp.bfloat16)


def _dot(a, b):
    return jnp.dot(a, b, preferred_element_type=jnp.float32)


def _dot_nt(a, b):
    return lax.dot_general(a, b, (((1,), (1,)), ((), ())), preferred_element_type=jnp.float32)


def _dot_tn(a, b):
    return lax.dot_general(a, b, (((0,), (0,)), ((), ())), preferred_element_type=jnp.float32)


def _rms(x):
    return x * lax.rsqrt(jnp.mean(x * x, axis=-1, keepdims=True) + EPS)


def _silu(z):
    half = 0.5 * z
    return half * jnp.tanh(half) + half


def _rotary(x, cos, sin):
    x1, x2 = x[:, :HALF], x[:, HALF:]
    return jnp.concatenate([x1 * cos - x2 * sin, x1 * sin + x2 * cos], axis=-1)


def _wide(t):
    return jnp.concatenate([t, t], axis=-1)


def _head_norm(v, gsgu):
    parts = [_rms(v[:, h * A_DH:(h + 1) * A_DH]) for h in range(v.shape[1] // A_DH)]
    return jnp.concatenate(parts, axis=-1) * gsgu


def _spatial_gate(vn, ws_ref, head0, bias):
    c = ws_ref.shape[-1]
    rows = []
    for r in range(vn.shape[0] // c):
        cols = [_dot(ws_ref[head0 + h], vn[r * c:(r + 1) * c, h * A_DH:(h + 1) * A_DH])
                for h in range(vn.shape[1] // A_DH)]
        rows.append(jnp.concatenate(cols, axis=-1) + bias)
    return jnp.concatenate(rows, axis=0)


def _spatial_gate_paired(vn, wpair_ref, bias):
    c = wpair_ref.shape[1]
    n_chunks = vn.shape[0] // c
    zeros = jnp.zeros((c, A_DH), vn.dtype)
    outs = []
    for j in range(wpair_ref.shape[0]):
        top, bottom = [], []
        for r in range(n_chunks):
            top += [vn[r * c:(r + 1) * c, (2 * j) * A_DH:(2 * j + 1) * A_DH], zeros]
            bottom += [zeros, vn[r * c:(r + 1) * c, (2 * j + 1) * A_DH:(2 * j + 2) * A_DH]]
        rhs = jnp.concatenate([jnp.concatenate(top, axis=1), jnp.concatenate(bottom, axis=1)], axis=0)
        outs.append(_dot(wpair_ref[j], rhs))
    rows = [jnp.concatenate([o[:, r * 2 * A_DH:(r + 1) * 2 * A_DH] for o in outs], axis=1) + bias
            for r in range(n_chunks)]
    return jnp.concatenate(rows, axis=0)


def _retention_front(q, k, v, states, cdec):
    n = len(states)
    seq = q.shape[0] // n
    qb, kb, vb = _bf(q), _bf(k), _bf(v)
    kw = _bf(k * _wide(cdec))
    scores = _dot_nt(qb, kb)
    cross = [_dot(qb[i * seq:(i + 1) * seq], _bf(states[i])) for i in range(n)]
    return scores, cross, kw, vb


def _retention_back(front, states, dec, rdec, sdec):
    scores, cross, kw, vb = front
    n = len(states)
    seq = vb.shape[0] // n
    updates = [_dot_tn(kw[i * seq:(i + 1) * seq], vb[i * seq:(i + 1) * seq]) for i in range(n)]
    intra = _dot(_bf(scores * dec), vb)
    cross = (cross[0] if n == 1 else jnp.concatenate(cross, axis=0)) * _wide(rdec)
    new_states = [sdec * states[i] + updates[i] for i in range(n)]
    return intra + cross, new_states


def _rope_rows(rope_hi_ref, rope_lo_ref, block):
    ch = rope_hi_ref[0, pl.ds(block, 1), :]
    sh = rope_hi_ref[1, pl.ds(block, 1), :]
    cl = rope_lo_ref[0]
    sl = rope_lo_ref[1]
    return ch * cl - sh * sl, sh * cl + ch * sl


def _prompt_layer_kernel(x_ref, rope_hi_ref, rope_lo_ref, gn_ref, win_ref, gsgu_ref, ws_ref, bs_ref,
                         dec_ref, rdec_ref, cdec_ref, sdec_ref, wout_ref, gfin_ref,
                         y_ref, s_ref, mix_ref, *, final_norm):
    tile = x_ref.shape[0]
    block0 = pl.program_id(0) * (tile // ROPE_SPLIT)

    @pl.when(pl.program_id(0) == 0)
    def _():
        s_ref[...] = jnp.zeros_like(s_ref)

    def start_block(blk):
        r = slice(blk * RET_BLOCK, (blk + 1) * RET_BLOCK)
        x = x_ref[r, :]
        h = _bf(_rms(x) * gn_ref[...])
        v = _dot(h, win_ref[:, OFF_V:OFF_V + D_A])
        u = _dot(h, win_ref[:, OFF_U:OFF_U + D_A])
        return r, x, h, u, v

    n_blocks = tile // RET_BLOCK
    rows_a = D_A + (B_HEADS - 2) * B_DV
    rows_b = rows_a + B_DV
    started = start_block(0)
    for blk in range(n_blocks):
        r, x, h, u, v = started

        vn = _bf(_head_norm(v, gsgu_ref[...]))
        mixed = _spatial_gate_paired(vn, ws_ref, bs_ref[...])
        za = _dot(h, win_ref[:, OFF_ZA:OFF_ZA + D_A])
        mix_ref[r, 0:D_A] = _bf(u * mixed * _silu(za))

        parts = [_rope_rows(rope_hi_ref, rope_lo_ref, block0 + blk * (RET_BLOCK // ROPE_SPLIT) + j)
                 for j in range(RET_BLOCK // ROPE_SPLIT)]
        cos = jnp.concatenate([p[0] for p in parts], axis=0)
        sin = jnp.concatenate([p[1] for p in parts], axis=0)

        def project(hb):
            c0 = hb * B_DK
            k = _rotary(_dot(h, win_ref[:, OFF_K + c0:OFF_K + c0 + B_DK]), cos, sin)
            q = _rotary(_dot(h, win_ref[:, OFF_Q + c0:OFF_Q + c0 + B_DK]), cos, sin)
            vb = _dot(h, win_ref[:, OFF_VB + c0:OFF_VB + c0 + B_DV])
            zb = _dot(h, win_ref[:, OFF_ZB + c0:OFF_ZB + c0 + B_DV])
            return q, k, vb, zb

        proj = project(0)
        for hb in range(B_HEADS):
            q, k, vb, zb = proj
            state = [s_ref[hb]]
            front = _retention_front(q, k, vb, state, cdec_ref[hb])
            if hb + 1 < B_HEADS:
                proj = project(hb + 1)
            else:
                y = x + _dot(mix_ref[r, 0:rows_a], wout_ref[0:rows_a, :])
            o, (s_new,) = _retention_back(front, state, dec_ref[hb], rdec_ref[hb], sdec_ref[hb])
            s_ref[hb] = s_new
            mix_ref[r, D_A + hb * B_DV:D_A + (hb + 1) * B_DV] = _bf(_rms(o) * _silu(zb))

        y = y + _dot(mix_ref[r, rows_a:rows_b], wout_ref[rows_a:rows_b, :])
        if blk + 1 < n_blocks:
            started = start_block(blk + 1)
        y = y + _dot(mix_ref[r, rows_b:D_MIX], wout_ref[rows_b:D_MIX, :])
        if final_norm:
            y = _rms(y) * gfin_ref[...]
        y_ref[r, :] = y


class _TileStream:
    def __init__(self, hbm, stage, sems, sem_row, tiles_per_layer, tile_view):
        self.hbm, self.stage, self.sems, self.sem_row = hbm, stage, sems, sem_row
        self.tiles_per_layer, self.tile_view = tiles_per_layer, tile_view
        self.n_tiles = hbm.shape[0] * tiles_per_layer
        self.n_slots = stage.shape[0]
        self.lookahead = self.n_slots - 1

    def _copy(self, n):
        slot = n % self.n_slots
        src = self.tile_view(self.hbm, n // self.tiles_per_layer, n % self.tiles_per_layer)
        return pltpu.make_async_copy(src, self.stage.at[slot], self.sems.at[self.sem_row, slot])

    def prime(self):
        for n in range(self.lookahead):
            self._copy(n).start()

    def take(self, n):
        @pl.when(n + self.lookahead < self.n_tiles)
        def _():
            self._copy(n + self.lookahead).start()

        self._copy(n).wait()
        return self.stage.at[n % self.n_slots]


def _tile_start(t):
    return t * W_TILE if isinstance(t, int) else pl.multiple_of(t * W_TILE, W_TILE)


def _win_tile(hbm, layer, t):
    return hbm.at[layer, :, pl.ds(_tile_start(t), W_TILE)]


def _wout_tile(hbm, layer, t):
    return hbm.at[layer, pl.ds(_tile_start(t), W_TILE), :]


def _state_group(hbm, layer, g):
    return hbm.at[layer, pl.ds(g * SAMPLE_GROUP, SAMPLE_GROUP)]


def _sample_kernel(x_hbm, s_in_hbm, cos_ref, sin_ref, gn_ref, win_hbm, gsgu_ref, ws_ref, bs_ref,
                   dec_ref, rdec_ref, cdec_ref, sdec_ref, wout_hbm, gfin_ref,
                   y_ref, s_out_ref, vn_ref, win_bf_ref, wout_bf_ref,
                   h_ref, p_ref, mix_ref, win_stage, wout_stage, state_stage, sems, x_sem, *, seq):
    layer = pl.program_id(0)
    step = pl.program_id(1)
    group = SAMPLE_GROUP
    n_groups = y_ref.shape[0] // (group * seq)
    tiles_per_block = D_A // W_TILE
    win_stream = _TileStream(win_hbm, win_stage, sems, 0, N_IN_TILES, _win_tile)
    wout_stream = _TileStream(wout_hbm, wout_stage, sems, 1, N_OUT_TILES, _wout_tile)
    state_stream = _TileStream(s_in_hbm, state_stage, sems, 2, n_groups, _state_group)

    @pl.when((layer == 0) & (step == 0))
    def _():
        win_stream.prime()
        wout_stream.prime()
        state_stream.prime()
        x_copy = pltpu.make_async_copy(x_hbm, y_ref, x_sem)
        x_copy.start()
        x_copy.wait()

    @pl.when(step == 0)
    def _():
        h_ref[...] = _bf(_rms(y_ref[...]) * gn_ref[...])

    @pl.when(step < N_IN_TILES)
    def _():
        w = _bf(win_stream.take(layer * N_IN_TILES + step)[...])
        win_bf_ref[...] = w
        p_ref[step] = _dot(h_ref[...], w)

    for t in range(tiles_per_block):
        @pl.when(step == 2 * tiles_per_block + t)
        def _(t=t):
            cols = slice(t * W_TILE, (t + 1) * W_TILE)
            u = p_ref[t]
            v = p_ref[tiles_per_block + t]
            za = p_ref[2 * tiles_per_block + t]
            vn = _head_norm(v, gsgu_ref[:, cols])
            vn_ref[...] = vn
            mixed = _spatial_gate(_bf(vn), ws_ref, t * (W_TILE // A_DH), bs_ref[:, cols])
            mix_ref[:, cols] = _bf(u * mixed * _silu(za))

    @pl.when((step >= N_IN_TILES) & (step < N_IN_TILES + n_groups))
    def _():
        r0 = pl.multiple_of((step - N_IN_TILES) * (group * seq), group * seq)
        rows = pl.ds(r0, group * seq)
        cos = cos_ref[...]
        sin = sin_ref[...]
        s_in = state_stream.take(layer * n_groups + step - N_IN_TILES)
        heads_per_tile = W_TILE // B_DK
        for hb in range(B_HEADS):
            t, c0 = hb // heads_per_tile, (hb % heads_per_tile) * B_DK
            blocks = [p_ref[(3 + i) * tiles_per_block + t, rows, c0:c0 + B_DK] for i in range(4)]
            q = _rotary(blocks[0], cos, sin)
            k = _rotary(blocks[1], cos, sin)
            states = [s_in[i, hb] for i in range(group)]
            front = _retention_front(q, k, blocks[2], states, cdec_ref[hb])
            o, new_states = _retention_back(front, states, dec_ref[hb], rdec_ref[hb], sdec_ref[hb])
            for i in range(group):
                s_out_ref[i, hb] = new_states[i]
            mix_ref[rows, D_A + hb * B_DV:D_A + (hb + 1) * B_DV] = _bf(_rms(o) * _silu(blocks[3]))

    @pl.when(step >= N_IN_TILES + n_groups)
    def _():
        w = _bf(wout_stream.take(layer * N_OUT_TILES + step - (N_IN_TILES + n_groups))[...])
        wout_bf_ref[...] = w
        for t in range(N_OUT_TILES):
            @pl.when(step == N_IN_TILES + n_groups + t)
            def _(t=t):
                y_ref[...] += _dot(mix_ref[:, t * W_TILE:(t + 1) * W_TILE], w)

    @pl.when((layer == pl.num_programs(0) - 1) & (step == pl.num_programs(1) - 1))
    def _():
        y_ref[...] = _rms(y_ref[...]) * gfin_ref[...]


def _const_spec(shape):
    zeros = (0,) * len(shape)
    return pl.BlockSpec(shape, lambda *_: zeros, pipeline_mode=pl.Buffered(1))


def _layer_spec(shape, layer):
    zeros = (0,) * len(shape)
    return pl.BlockSpec((None,) + shape, lambda i: (layer,) + zeros, pipeline_mode=pl.Buffered(1))


def _decay_tables(length, n_seq=1):
    lg = jnp.log(1.0 - 2.0 ** (-5.0 - jnp.arange(B_HEADS, dtype=jnp.float32)))
    idx = jnp.arange(length, dtype=jnp.float32)
    diff = idx[:, None] - idx[None, :]
    causal = diff >= 0
    dec = jnp.where(causal[None], jnp.exp(lg[:, None, None] * jnp.where(causal, diff, 0.0)[None]), 0.0)
    rdec = jnp.exp(lg[:, None] * (idx[None, :] + 1.0))
    cdec = jnp.exp(lg[:, None] * (length - 1.0 - idx[None, :]))
    sdec = jnp.exp(lg * length)
    dec = dec * (B_DK ** -0.5)
    cdec = cdec * (B_DK ** -0.5)
    dec = jnp.einsum("ab,hij->haibj", jnp.eye(n_seq, dtype=dec.dtype), dec)
    dec = dec.reshape(B_HEADS, n_seq * length, n_seq * length)
    rdec = jnp.broadcast_to(jnp.tile(rdec, (1, n_seq))[:, :, None], (B_HEADS, n_seq * length, LANES))
    cdec = jnp.broadcast_to(jnp.tile(cdec, (1, n_seq))[:, :, None], (B_HEADS, n_seq * length, LANES))
    sdec = jnp.broadcast_to(sdec[:, None, None], (B_HEADS, 1, B_DV))
    return dec, rdec, cdec, sdec


def _rope_tables(start, length):
    inv = ROPE_BASE ** (-jnp.arange(HALF, dtype=jnp.float32) / HALF)
    ang = (start + jnp.arange(length)).astype(jnp.float32)[:, None] * inv[None, :]
    return jnp.cos(ang), jnp.sin(ang)


def _rope_split_tables(length):
    assert length % ROPE_SPLIT == 0
    inv = ROPE_BASE ** (-jnp.arange(HALF, dtype=jnp.float32) / HALF)
    ang_hi = (ROPE_SPLIT * jnp.arange(length // ROPE_SPLIT)).astype(jnp.float32)[:, None] * inv[None, :]
    ang_lo = jnp.arange(ROPE_SPLIT).astype(jnp.float32)[:, None] * inv[None, :]
    return (jnp.stack([jnp.cos(ang_hi), jnp.sin(ang_hi)], axis=0),
            jnp.stack([jnp.cos(ang_lo), jnp.sin(ang_lo)], axis=0))


def _gate_weights(w_s, b_s, c, n_seq=1):
    i = jnp.arange(c)
    mask = (i[None, :] // CHUNK) <= (i[:, None] // CHUNK)
    w = jnp.where(mask, w_s[..., :c, :c], 0.0)
    w = jnp.einsum("ab,...ij->...aibj", jnp.eye(n_seq, dtype=w.dtype), w)
    w = _bf(w.reshape(w.shape[:-4] + (n_seq * c, n_seq * c)))
    bias = jnp.swapaxes(b_s[..., :c], -1, -2)
    bias = jnp.broadcast_to(bias[..., None], bias.shape + (A_DH,)).reshape(bias.shape[:-1] + (D_A,))
    bias = jnp.tile(bias, (1,) * (bias.ndim - 2) + (n_seq, 1))
    return w, bias


def _prompt_layer(layer, x, rope_hi, rope_lo, gn, win, gsgu, ws, bs, decs, wout, gfin, final_norm, in_place):
    seq = x.shape[0]
    tile = PROMPT_TILE
    dec, rdec, cdec, sdec = decs
    row = lambda i: (i, 0)
    kern = functools.partial(_prompt_layer_kernel, final_norm=final_norm)
    return pl.pallas_call(
        kern,
        grid=(seq // tile,),
        in_specs=[
            pl.BlockSpec((tile, D_MODEL), row),
            _const_spec(rope_hi.shape),
            _const_spec(rope_lo.shape),
            _const_spec((1, D_MODEL)),
            _layer_spec((D_MODEL, D_PROJ), layer),
            _const_spec((1, D_A)),
            _const_spec((A_HEADS // 2, A_CHUNK, 2 * A_CHUNK)),
            _const_spec((A_CHUNK, D_A)),
            _const_spec((B_HEADS, RET_BLOCK, RET_BLOCK)),
            _const_spec((B_HEADS, RET_BLOCK, LANES)),
            _const_spec((B_HEADS, RET_BLOCK, LANES)),
            _const_spec((B_HEADS, 1, B_DV)),
            _layer_spec((D_MIX, D_MODEL), layer),
            _const_spec((1, D_MODEL)),
        ],
        out_specs=[
            pl.BlockSpec((tile, D_MODEL), row),
            pl.BlockSpec((B_HEADS, B_DK, B_DV), lambda i: (0, 0, 0)),
        ],
        out_shape=[
            jax.ShapeDtypeStruct((seq, D_MODEL), jnp.float32),
            jax.ShapeDtypeStruct((B_HEADS, B_DK, B_DV), jnp.float32),
        ],
        scratch_shapes=[pltpu.VMEM((tile, D_MIX), jnp.bfloat16)],
        input_output_aliases={0: 0} if in_place else {},
        compiler_params=pltpu.CompilerParams(
            dimension_semantics=("arbitrary",), vmem_limit_bytes=VMEM_LIMIT_BYTES),
        name="prompt_layer",
    )(x, rope_hi, rope_lo, gn, win, gsgu, ws, bs, dec, rdec, cdec, sdec, wout, gfin)


def _sample_trunk(x, state, cos, sin, gn, w_in, gsgu, ws, bs, decs, w_out, gfin, seq):
    rows = x.shape[0]
    nb = rows // seq
    group = SAMPLE_GROUP
    n_groups = nb // group
    n_steps = N_IN_TILES + n_groups + N_OUT_TILES
    dec, rdec, cdec, sdec = decs
    gate_rows = ws.shape[-1]
    tiles_per_block = D_A // W_TILE

    def per_layer(shape):
        zeros = (0,) * len(shape)
        return pl.BlockSpec((None,) + shape, lambda l, s: (l,) + zeros)

    def state_map(l, s):
        return (l, jnp.clip(s - N_IN_TILES, 0, n_groups - 1), 0, 0, 0)

    state_spec = pl.BlockSpec((None, group, B_HEADS, B_DK, B_DV), state_map)
    kern = functools.partial(_sample_kernel, seq=seq)
    return pl.pallas_call(
        kern,
        grid=(DEPTH, n_steps),
        in_specs=[
            pl.BlockSpec(memory_space=pl.ANY),
            pl.BlockSpec(memory_space=pl.ANY),
            _const_spec((group * seq, HALF)),
            _const_spec((group * seq, HALF)),
            per_layer((1, D_MODEL)),
            pl.BlockSpec(memory_space=pl.ANY),
            per_layer((1, D_A)),
            per_layer((A_HEADS, gate_rows, gate_rows)),
            per_layer((gate_rows, D_A)),
            _const_spec((B_HEADS, group * seq, group * seq)),
            _const_spec((B_HEADS, group * seq, LANES)),
            _const_spec((B_HEADS, group * seq, LANES)),
            _const_spec((B_HEADS, 1, B_DV)),
            pl.BlockSpec(memory_space=pl.ANY),
            _const_spec((1, D_MODEL)),
        ],
        out_specs=[
            pl.BlockSpec((rows, D_MODEL), lambda l, s: (0, 0)),
            state_spec,
            pl.BlockSpec((None, rows, W_TILE),
                         lambda l, s: (l, 0, jnp.clip(s - 2 * tiles_per_block, 0, tiles_per_block - 1))),
            pl.BlockSpec((None, D_MODEL, W_TILE), lambda l, s: (l, 0, jnp.minimum(s, N_IN_TILES - 1))),
            pl.BlockSpec((None, W_TILE, D_MODEL),
                         lambda l, s: (l, jnp.clip(s - N_IN_TILES - n_groups, 0, N_OUT_TILES - 1), 0)),
        ],
        out_shape=[
            jax.ShapeDtypeStruct((rows, D_MODEL), jnp.float32),
            jax.ShapeDtypeStruct(state.shape, jnp.float32),
            jax.ShapeDtypeStruct((DEPTH, rows, D_A), jnp.float32),
            jax.ShapeDtypeStruct(w_in.shape, jnp.bfloat16),
            jax.ShapeDtypeStruct(w_out.shape, jnp.bfloat16),
        ],
        scratch_shapes=[
            pltpu.VMEM((rows, D_MODEL), jnp.bfloat16),
            pltpu.VMEM((N_IN_TILES, rows, W_TILE), jnp.float32),
            pltpu.VMEM((rows, D_MIX), jnp.bfloat16),
            pltpu.VMEM((W_IN_STREAM_SLOTS, D_MODEL, W_TILE), jnp.float32),
            pltpu.VMEM((W_OUT_STREAM_SLOTS, W_TILE, D_MODEL), jnp.float32),
            pltpu.VMEM((STATE_STREAM_SLOTS, group, B_HEADS, B_DK, B_DV), jnp.float32),
            pltpu.SemaphoreType.DMA((3, max(W_IN_STREAM_SLOTS, W_OUT_STREAM_SLOTS, STATE_STREAM_SLOTS))),
            pltpu.SemaphoreType.DMA(()),
        ],
        compiler_params=pltpu.CompilerParams(
            dimension_semantics=("arbitrary", "arbitrary"), vmem_limit_bytes=VMEM_LIMIT_BYTES),
        name="sample_trunk",
    )(x, state, cos, sin, gn, w_in, gsgu, ws, bs, dec, rdec, cdec, sdec, w_out, gfin)


def kernel(x_prompt, x_sample, state_ret, g_norm, w_in, g_sgu, w_s, b_s, w_out, g_final):
    batch, seq, _ = x_prompt.shape
    dec_batch, dec_seq, _ = x_sample.shape
    assert batch == 1 and seq % PROMPT_TILE == 0
    assert PROMPT_TILE % RET_BLOCK == 0 and RET_BLOCK % A_CHUNK == 0 and RET_BLOCK % ROPE_SPLIT == 0
    assert dec_seq <= CHUNK and dec_batch % SAMPLE_GROUP == 0
    assert (dec_batch * dec_seq) % (SAMPLE_GATE_GROUP * dec_seq) == 0
    assert D_A % W_TILE == 0 and W_TILE % B_DK == 0 and W_TILE % A_DH == 0

    gfin = g_final.reshape(1, D_MODEL)
    gn = g_norm.reshape(DEPTH, 1, D_MODEL)
    gsgu = g_sgu.reshape(DEPTH, 1, D_A)

    cos_s, sin_s = _rope_tables(PAST_LEN, dec_seq)
    cos_s, sin_s = jnp.tile(cos_s, (SAMPLE_GROUP, 1)), jnp.tile(sin_s, (SAMPLE_GROUP, 1))
    ws_s, bs_s = _gate_weights(w_s, b_s, dec_seq, SAMPLE_GATE_GROUP)
    hs, new_ret_sample, vn, win_bf, wout_bf = _sample_trunk(
        x_sample.reshape(dec_batch * dec_seq, D_MODEL), state_ret, cos_s, sin_s, gn, w_in, gsgu, ws_s, bs_s,
        _decay_tables(dec_seq, SAMPLE_GROUP), w_out, gfin, dec_seq)

    rope_hi, rope_lo = _rope_split_tables(seq)
    decs_p = _decay_tables(RET_BLOCK)
    ws_p, bs_p = _gate_weights(w_s, b_s, A_CHUNK)
    ws_p = ws_p.reshape(DEPTH, A_HEADS // 2, 2, A_CHUNK, A_CHUNK).transpose(0, 1, 3, 2, 4)
    ws_p = ws_p.reshape(DEPTH, A_HEADS // 2, A_CHUNK, 2 * A_CHUNK)
    hp = x_prompt.reshape(seq, D_MODEL)
    ret_p = []
    for l in range(DEPTH):
        hp, sp = _prompt_layer(l, hp, rope_hi, rope_lo, gn[l], win_bf, gsgu[l], ws_p[l], bs_p[l], decs_p, wout_bf,
                               gfin, l == DEPTH - 1, l > 0)
        ret_p.append(sp.reshape(batch, B_HEADS, B_DK, B_DV))

    y_prompt = hp.reshape(batch, seq, D_MODEL)
    y_sample = hs.reshape(dec_batch, dec_seq, D_MODEL)
    new_chunk_v = vn.reshape(DEPTH, dec_batch, dec_seq, A_HEADS, A_DH)
    return (y_prompt, y_sample, jnp.stack(ret_p, axis=0), new_ret_sample, new_chunk_v)
```

```python
import functools

import jax
import jax.numpy as jnp
from jax import lax
from jax.experimental import pallas as pl
from jax.experimental.pallas import tpu as pltpu

D_MODEL = 1024
DEPTH = 4
PAST_LEN = 4096
CHUNK = 64
D_MIX = 2 * D_MODEL
D_A = D_MIX // 2
A_HEADS = 8
A_DH = D_A // A_HEADS
A_CHUNK = 128
D_B = D_MIX - D_A
B_HEADS = 4
B_DK = D_B // B_HEADS
B_DV = D_B // B_HEADS
D_PROJ = 3 * D_A + 4 * D_B
ROPE_BASE = 10000.0
EPS = 1e-6
HALF = B_DK // 2
LANES = 128
ROPE_SPLIT = 128

OFF_U, OFF_V, OFF_ZA = 0, D_A, 2 * D_A
OFF_Q, OFF_K, OFF_VB, OFF_ZB = 3 * D_A, 3 * D_A + D_B, 3 * D_A + 2 * D_B, 3 * D_A + 3 * D_B

RET_BLOCK = 256
PROMPT_TILE = 512

SAMPLE_GROUP = 2
SAMPLE_GATE_GROUP = 4
W_TILE = 512
N_IN_TILES = D_PROJ // W_TILE
N_OUT_TILES = D_MIX // W_TILE
W_IN_STREAM_SLOTS = 4
W_OUT_STREAM_SLOTS = 3
STATE_STREAM_SLOTS = 3

VMEM_LIMIT_BYTES = 56 * 1024 * 1024


def _bf(x):
    return x.astype(jnp.bfloat16)


def _dot(a, b):
    return jnp.dot(a, b, preferred_element_type=jnp.float32)


def _dot_nt(a, b):
    return lax.dot_general(a, b, (((1,), (1,)), ((), ())), preferred_element_type=jnp.float32)


def _dot_tn(a, b):
    return lax.dot_general(a, b, (((0,), (0,)), ((), ())), preferred_element_type=jnp.float32)


def _rms(x):
    return x * lax.rsqrt(jnp.mean(x * x, axis=-1, keepdims=True) + EPS)


def _silu(z):
    half = 0.5 * z
    return half * jnp.tanh(half) + half


def _rotary(x, cos, sin):
    x1, x2 = x[:, :HALF], x[:, HALF:]
    return jnp.concatenate([x1 * cos - x2 * sin, x1 * sin + x2 * cos], axis=-1)


def _wide(t):
    return jnp.concatenate([t, t], axis=-1)


def _head_norm(v, gsgu):
    parts = [_rms(v[:, h * A_DH:(h + 1) * A_DH]) for h in range(v.shape[1] // A_DH)]
    return jnp.concatenate(parts, axis=-1) * gsgu


def _spatial_gate(vn, ws_ref, head0, bias):
    c = ws_ref.shape[-1]
    rows = []
    for r in range(vn.shape[0] // c):
        cols = [_dot(ws_ref[head0 + h], vn[r * c:(r + 1) * c, h * A_DH:(h + 1) * A_DH])
                for h in range(vn.shape[1] // A_DH)]
        rows.append(jnp.concatenate(cols, axis=-1) + bias)
    return jnp.concatenate(rows, axis=0)


def _spatial_gate_paired(vn, wpair_ref, bias):
    c = wpair_ref.shape[1]
    n_chunks = vn.shape[0] // c
    zeros = jnp.zeros((c, A_DH), vn.dtype)
    outs = []
    for j in range(wpair_ref.shape[0]):
        top, bottom = [], []
        for r in range(n_chunks):
            top += [vn[r * c:(r + 1) * c, (2 * j) * A_DH:(2 * j + 1) * A_DH], zeros]
            bottom += [zeros, vn[r * c:(r + 1) * c, (2 * j + 1) * A_DH:(2 * j + 2) * A_DH]]
        rhs = jnp.concatenate([jnp.concatenate(top, axis=1), jnp.concatenate(bottom, axis=1)], axis=0)
        outs.append(_dot(wpair_ref[j], rhs))
    rows = [jnp.concatenate([o[:, r * 2 * A_DH:(r + 1) * 2 * A_DH] for o in outs], axis=1) + bias
            for r in range(n_chunks)]
    return jnp.concatenate(rows, axis=0)


def _retention_front(q, k, v, states, cdec):
    n = len(states)
    seq = q.shape[0] // n
    qb, kb, vb = _bf(q), _bf(k), _bf(v)
    kw = _bf(k * _wide(cdec))
    scores = _dot_nt(qb, kb)
    cross = [_dot(qb[i * seq:(i + 1) * seq], _bf(states[i])) for i in range(n)]
    return scores, cross, kw, vb


def _retention_back(front, states, dec, rdec, sdec):
    scores, cross, kw, vb = front
    n = len(states)
    seq = vb.shape[0] // n
    updates = [_dot_tn(kw[i * seq:(i + 1) * seq], vb[i * seq:(i + 1) * seq]) for i in range(n)]
    intra = _dot(_bf(scores * dec), vb)
    cross = (cross[0] if n == 1 else jnp.concatenate(cross, axis=0)) * _wide(rdec)
    new_states = [sdec * states[i] + updates[i] for i in range(n)]
    return intra + cross, new_states


def _rope_rows(rope_hi_ref, rope_lo_ref, block):
    ch = rope_hi_ref[0, pl.ds(block, 1), :]
    sh = rope_hi_ref[1, pl.ds(block, 1), :]
    cl = rope_lo_ref[0]
    sl = rope_lo_ref[1]
    return ch * cl - sh * sl, sh * cl + ch * sl


def _prompt_layer_kernel(x_ref, rope_hi_ref, rope_lo_ref, gn_ref, win_ref, gsgu_ref, ws_ref, bs_ref,
                         dec_ref, rdec_ref, cdec_ref, sdec_ref, wout_ref, gfin_ref,
                         y_ref, s_ref, mix_ref, *, final_norm):
    tile = x_ref.shape[0]
    block0 = pl.program_id(0) * (tile // ROPE_SPLIT)

    @pl.when(pl.program_id(0) == 0)
    def _():
        s_ref[...] = jnp.zeros_like(s_ref)

    def start_block(blk):
        r = slice(blk * RET_BLOCK, (blk + 1) * RET_BLOCK)
        x = x_ref[r, :]
        h = _bf(_rms(x) * gn_ref[...])
        v = _dot(h, win_ref[:, OFF_V:OFF_V + D_A])
        u = _dot(h, win_ref[:, OFF_U:OFF_U + D_A])
        return r, x, h, u, v

    n_blocks = tile // RET_BLOCK
    rows_a = D_A + (B_HEADS - 2) * B_DV
    rows_b = rows_a + B_DV
    started = start_block(0)
    for blk in range(n_blocks):
        r, x, h, u, v = started

        vn = _bf(_head_norm(v, gsgu_ref[...]))
        mixed = _spatial_gate_paired(vn, ws_ref, bs_ref[...])
        za = _dot(h, win_ref[:, OFF_ZA:OFF_ZA + D_A])
        mix_ref[r, 0:D_A] = _bf(u * mixed * _silu(za))

        parts = [_rope_rows(rope_hi_ref, rope_lo_ref, block0 + blk * (RET_BLOCK // ROPE_SPLIT) + j)
                 for j in range(RET_BLOCK // ROPE_SPLIT)]
        cos = jnp.concatenate([p[0] for p in parts], axis=0)
        sin = jnp.concatenate([p[1] for p in parts], axis=0)

        def project(hb):
            c0 = hb * B_DK
            k = _rotary(_dot(h, win_ref[:, OFF_K + c0:OFF_K + c0 + B_DK]), cos, sin)
            q = _rotary(_dot(h, win_ref[:, OFF_Q + c0:OFF_Q + c0 + B_DK]), cos, sin)
            vb = _dot(h, win_ref[:, OFF_VB + c0:OFF_VB + c0 + B_DV])
            zb = _dot(h, win_ref[:, OFF_ZB + c0:OFF_ZB + c0 + B_DV])
            return q, k, vb, zb

        proj = project(0)
        for hb in range(B_HEADS):
            q, k, vb, zb = proj
            state = [s_ref[hb]]
            front = _retention_front(q, k, vb, state, cdec_ref[hb])
            if hb + 1 < B_HEADS:
                proj = project(hb + 1)
            else:
                y = x + _dot(mix_ref[r, 0:rows_a], wout_ref[0:rows_a, :])
            o, (s_new,) = _retention_back(front, state, dec_ref[hb], rdec_ref[hb], sdec_ref[hb])
            s_ref[hb] = s_new
            mix_ref[r, D_A + hb * B_DV:D_A + (hb + 1) * B_DV] = _bf(_rms(o) * _silu(zb))

        y = y + _dot(mix_ref[r, rows_a:rows_b], wout_ref[rows_a:rows_b, :])
        if blk + 1 < n_blocks:
            started = start_block(blk + 1)
        y = y + _dot(mix_ref[r, rows_b:D_MIX], wout_ref[rows_b:D_MIX, :])
        if final_norm:
            y = _rms(y) * gfin_ref[...]
        y_ref[r, :] = y


class _TileStream:
    def __init__(self, hbm, stage, sems, sem_row, tiles_per_layer, tile_view):
        self.hbm, self.stage, self.sems, self.sem_row = hbm, stage, sems, sem_row
        self.tiles_per_layer, self.tile_view = tiles_per_layer, tile_view
        self.n_tiles = hbm.shape[0] * tiles_per_layer
        self.n_slots = stage.shape[0]
        self.lookahead = self.n_slots - 1

    def _copy(self, n):
        slot = n % self.n_slots
        src = self.tile_view(self.hbm, n // self.tiles_per_layer, n % self.tiles_per_layer)
        return pltpu.make_async_copy(src, self.stage.at[slot], self.sems.at[self.sem_row, slot])

    def prime(self):
        for n in range(self.lookahead):
            self._copy(n).start()

    def take(self, n):
        @pl.when(n + self.lookahead < self.n_tiles)
        def _():
            self._copy(n + self.lookahead).start()

        self._copy(n).wait()
        return self.stage.at[n % self.n_slots]


def _tile_start(t):
    return t * W_TILE if isinstance(t, int) else pl.multiple_of(t * W_TILE, W_TILE)


def _win_tile(hbm, layer, t):
    return hbm.at[layer, :, pl.ds(_tile_start(t), W_TILE)]


def _wout_tile(hbm, layer, t):
    return hbm.at[layer, pl.ds(_tile_start(t), W_TILE), :]


def _state_group(hbm, layer, g):
    return hbm.at[layer, pl.ds(g * SAMPLE_GROUP, SAMPLE_GROUP)]


def _sample_kernel(x_hbm, s_in_hbm, cos_ref, sin_ref, gn_ref, win_hbm, gsgu_ref, ws_ref, bs_ref,
                   dec_ref, rdec_ref, cdec_ref, sdec_ref, wout_hbm, gfin_ref,
                   y_ref, s_out_ref, vn_ref, win_bf_ref, wout_bf_ref,
                   h_ref, p_ref, mix_ref, win_stage, wout_stage, state_stage, sems, x_sem, *, seq):
    layer = pl.program_id(0)
    step = pl.program_id(1)
    group = SAMPLE_GROUP
    n_groups = y_ref.shape[0] // (group * seq)
    tiles_per_block = D_A // W_TILE
    win_stream = _TileStream(win_hbm, win_stage, sems, 0, N_IN_TILES, _win_tile)
    wout_stream = _TileStream(wout_hbm, wout_stage, sems, 1, N_OUT_TILES, _wout_tile)
    state_stream = _TileStream(s_in_hbm, state_stage, sems, 2, n_groups, _state_group)

    @pl.when((layer == 0) & (step == 0))
    def _():
        win_stream.prime()
        wout_stream.prime()
        state_stream.prime()
        x_copy = pltpu.make_async_copy(x_hbm, y_ref, x_sem)
        x_copy.start()
        x_copy.wait()

    @pl.when(step == 0)
    def _():
        h_ref[...] = _bf(_rms(y_ref[...]) * gn_ref[...])

    @pl.when(step < N_IN_TILES)
    def _():
        w = _bf(win_stream.take(layer * N_IN_TILES + step)[...])
        win_bf_ref[...] = w
        p_ref[step] = _dot(h_ref[...], w)

    for t in range(tiles_per_block):
        @pl.when(step == 2 * tiles_per_block + t)
        def _(t=t):
            cols = slice(t * W_TILE, (t + 1) * W_TILE)
            half = p_ref.shape[1] // 2
            for rows in (slice(0, half), slice(half, 2 * half)):
                u = p_ref[t, rows, :]
                v = p_ref[tiles_per_block + t, rows, :]
                za = p_ref[2 * tiles_per_block + t, rows, :]
                vn = _head_norm(v, gsgu_ref[:, cols])
                vn_ref[rows, :] = vn
                mixed = _spatial_gate(_bf(vn), ws_ref, t * (W_TILE // A_DH), bs_ref[:, cols])
                mix_ref[rows, cols] = _bf(u * mixed * _silu(za))

    @pl.when((step >= N_IN_TILES) & (step < N_IN_TILES + n_groups))
    def _():
        r0 = pl.multiple_of((step - N_IN_TILES) * (group * seq), group * seq)
        rows = pl.ds(r0, group * seq)
        cos = cos_ref[...]
        sin = sin_ref[...]
        s_in = state_stream.take(layer * n_groups + step - N_IN_TILES)
        heads_per_tile = W_TILE // B_DK
        for hb in range(B_HEADS):
            t, c0 = hb // heads_per_tile, (hb % heads_per_tile) * B_DK
            blocks = [p_ref[(3 + i) * tiles_per_block + t, rows, c0:c0 + B_DK] for i in range(4)]
            q = _rotary(blocks[0], cos, sin)
            k = _rotary(blocks[1], cos, sin)
            states = [s_in[i, hb] for i in range(group)]
            front = _retention_front(q, k, blocks[2], states, cdec_ref[hb])
            o, new_states = _retention_back(front, states, dec_ref[hb], rdec_ref[hb], sdec_ref[hb])
            for i in range(group):
                s_out_ref[i, hb] = new_states[i]
            mix_ref[rows, D_A + hb * B_DV:D_A + (hb + 1) * B_DV] = _bf(_rms(o) * _silu(blocks[3]))

    @pl.when(step >= N_IN_TILES + n_groups)
    def _():
        w = _bf(wout_stream.take(layer * N_OUT_TILES + step - (N_IN_TILES + n_groups))[...])
        wout_bf_ref[...] = w
        for t in range(N_OUT_TILES):
            @pl.when(step == N_IN_TILES + n_groups + t)
            def _(t=t):
                y_ref[...] += _dot(mix_ref[:, t * W_TILE:(t + 1) * W_TILE], w)

    @pl.when((layer == pl.num_programs(0) - 1) & (step == pl.num_programs(1) - 1))
    def _():
        y_ref[...] = _rms(y_ref[...]) * gfin_ref[...]


def _const_spec(shape):
    zeros = (0,) * len(shape)
    return pl.BlockSpec(shape, lambda *_: zeros, pipeline_mode=pl.Buffered(1))


def _layer_spec(shape, layer):
    zeros = (0,) * len(shape)
    return pl.BlockSpec((None,) + shape, lambda i: (layer,) + zeros, pipeline_mode=pl.Buffered(1))


def _decay_tables(length, n_seq=1):
    lg = jnp.log(1.0 - 2.0 ** (-5.0 - jnp.arange(B_HEADS, dtype=jnp.float32)))
    idx = jnp.arange(length, dtype=jnp.float32)
    diff = idx[:, None] - idx[None, :]
    causal = diff >= 0
    dec = jnp.where(causal[None], jnp.exp(lg[:, None, None] * jnp.where(causal, diff, 0.0)[None]), 0.0)
    rdec = jnp.exp(lg[:, None] * (idx[None, :] + 1.0))
    cdec = jnp.exp(lg[:, None] * (length - 1.0 - idx[None, :]))
    sdec = jnp.exp(lg * length)
    dec = dec * (B_DK ** -0.5)
    cdec = cdec * (B_DK ** -0.5)
    dec = jnp.einsum("ab,hij->haibj", jnp.eye(n_seq, dtype=dec.dtype), dec)
    dec = dec.reshape(B_HEADS, n_seq * length, n_seq * length)
    rdec = jnp.broadcast_to(jnp.tile(rdec, (1, n_seq))[:, :, None], (B_HEADS, n_seq * length, LANES))
    cdec = jnp.broadcast_to(jnp.tile(cdec, (1, n_seq))[:, :, None], (B_HEADS, n_seq * length, LANES))
    sdec = jnp.broadcast_to(sdec[:, None, None], (B_HEADS, 1, B_DV))
    return dec, rdec, cdec, sdec


def _rope_tables(start, length):
    inv = ROPE_BASE ** (-jnp.arange(HALF, dtype=jnp.float32) / HALF)
    ang = (start + jnp.arange(length)).astype(jnp.float32)[:, None] * inv[None, :]
    return jnp.cos(ang), jnp.sin(ang)


def _rope_split_tables(length):
    assert length % ROPE_SPLIT == 0
    inv = ROPE_BASE ** (-jnp.arange(HALF, dtype=jnp.float32) / HALF)
    ang_hi = (ROPE_SPLIT * jnp.arange(length // ROPE_SPLIT)).astype(jnp.float32)[:, None] * inv[None, :]
    ang_lo = jnp.arange(ROPE_SPLIT).astype(jnp.float32)[:, None] * inv[None, :]
    return (jnp.stack([jnp.cos(ang_hi), jnp.sin(ang_hi)], axis=0),
            jnp.stack([jnp.cos(ang_lo), jnp.sin(ang_lo)], axis=0))


def _gate_weights(w_s, b_s, c, n_seq=1):
    i = jnp.arange(c)
    mask = (i[None, :] // CHUNK) <= (i[:, None] // CHUNK)
    w = jnp.where(mask, w_s[..., :c, :c], 0.0)
    w = jnp.einsum("ab,...ij->...aibj", jnp.eye(n_seq, dtype=w.dtype), w)
    w = _bf(w.reshape(w.shape[:-4] + (n_seq * c, n_seq * c)))
    bias = jnp.swapaxes(b_s[..., :c], -1, -2)
    bias = jnp.broadcast_to(bias[..., None], bias.shape + (A_DH,)).reshape(bias.shape[:-1] + (D_A,))
    bias = jnp.tile(bias, (1,) * (bias.ndim - 2) + (n_seq, 1))
    return w, bias


def _prompt_layer(layer, x, rope_hi, rope_lo, gn, win, gsgu, ws, bs, decs, wout, gfin, final_norm, in_place):
    seq = x.shape[0]
    tile = PROMPT_TILE
    dec, rdec, cdec, sdec = decs
    row = lambda i: (i, 0)
    kern = functools.partial(_prompt_layer_kernel, final_norm=final_norm)
    return pl.pallas_call(
        kern,
        grid=(seq // tile,),
        in_specs=[
            pl.BlockSpec((tile, D_MODEL), row),
            _const_spec(rope_hi.shape),
            _const_spec(rope_lo.shape),
            _const_spec((1, D_MODEL)),
            _layer_spec((D_MODEL, D_PROJ), layer),
            _const_spec((1, D_A)),
            _const_spec((A_HEADS // 2, A_CHUNK, 2 * A_CHUNK)),
            _const_spec((A_CHUNK, D_A)),
            _const_spec((B_HEADS, RET_BLOCK, RET_BLOCK)),
            _const_spec((B_HEADS, RET_BLOCK, LANES)),
            _const_spec((B_HEADS, RET_BLOCK, LANES)),
            _const_spec((B_HEADS, 1, B_DV)),
            _layer_spec((D_MIX, D_MODEL), layer),
            _const_spec((1, D_MODEL)),
        ],
        out_specs=[
            pl.BlockSpec((tile, D_MODEL), row),
            pl.BlockSpec((B_HEADS, B_DK, B_DV), lambda i: (0, 0, 0)),
        ],
        out_shape=[
            jax.ShapeDtypeStruct((seq, D_MODEL), jnp.float32),
            jax.ShapeDtypeStruct((B_HEADS, B_DK, B_DV), jnp.float32),
        ],
        scratch_shapes=[pltpu.VMEM((tile, D_MIX), jnp.bfloat16)],
        input_output_aliases={0: 0} if in_place else {},
        compiler_params=pltpu.CompilerParams(
            dimension_semantics=("arbitrary",), vmem_limit_bytes=VMEM_LIMIT_BYTES),
        name="prompt_layer",
    )(x, rope_hi, rope_lo, gn, win, gsgu, ws, bs, dec, rdec, cdec, sdec, wout, gfin)


def _sample_trunk(x, state, cos, sin, gn, w_in, gsgu, ws, bs, decs, w_out, gfin, seq):
    rows = x.shape[0]
    nb = rows // seq
    group = SAMPLE_GROUP
    n_groups = nb // group
    n_steps = N_IN_TILES + n_groups + N_OUT_TILES
    dec, rdec, cdec, sdec = decs
    gate_rows = ws.shape[-1]
    tiles_per_block = D_A // W_TILE

    def per_layer(shape):
        zeros = (0,) * len(shape)
        return pl.BlockSpec((None,) + shape, lambda l, s: (l,) + zeros)

    def state_map(l, s):
        return (l, jnp.clip(s - N_IN_TILES, 0, n_groups - 1), 0, 0, 0)

    state_spec = pl.BlockSpec((None, group, B_HEADS, B_DK, B_DV), state_map)
    kern = functools.partial(_sample_kernel, seq=seq)
    return pl.pallas_call(
        kern,
        grid=(DEPTH, n_steps),
        in_specs=[
            pl.BlockSpec(memory_space=pl.ANY),
            pl.BlockSpec(memory_space=pl.ANY),
            _const_spec((group * seq, HALF)),
            _const_spec((group * seq, HALF)),
            per_layer((1, D_MODEL)),
            pl.BlockSpec(memory_space=pl.ANY),
            per_layer((1, D_A)),
            per_layer((A_HEADS, gate_rows, gate_rows)),
            per_layer((gate_rows, D_A)),
            _const_spec((B_HEADS, group * seq, group * seq)),
            _const_spec((B_HEADS, group * seq, LANES)),
            _const_spec((B_HEADS, group * seq, LANES)),
            _const_spec((B_HEADS, 1, B_DV)),
            pl.BlockSpec(memory_space=pl.ANY),
            _const_spec((1, D_MODEL)),
        ],
        out_specs=[
            pl.BlockSpec((rows, D_MODEL), lambda l, s: (0, 0)),
            state_spec,
            pl.BlockSpec((None, rows, W_TILE),
                         lambda l, s: (l, 0, jnp.clip(s - 2 * tiles_per_block, 0, tiles_per_block - 1))),
            pl.BlockSpec((None, D_MODEL, W_TILE), lambda l, s: (l, 0, jnp.minimum(s, N_IN_TILES - 1))),
            pl.BlockSpec((None, W_TILE, D_MODEL),
                         lambda l, s: (l, jnp.clip(s - N_IN_TILES - n_groups, 0, N_OUT_TILES - 1), 0)),
        ],
        out_shape=[
            jax.ShapeDtypeStruct((rows, D_MODEL), jnp.float32),
            jax.ShapeDtypeStruct(state.shape, jnp.float32),
            jax.ShapeDtypeStruct((DEPTH, rows, D_A), jnp.float32),
            jax.ShapeDtypeStruct(w_in.shape, jnp.bfloat16),
            jax.ShapeDtypeStruct(w_out.shape, jnp.bfloat16),
        ],
        scratch_shapes=[
            pltpu.VMEM((rows, D_MODEL), jnp.bfloat16),
            pltpu.VMEM((N_IN_TILES, rows, W_TILE), jnp.float32),
            pltpu.VMEM((rows, D_MIX), jnp.bfloat16),
            pltpu.VMEM((W_IN_STREAM_SLOTS, D_MODEL, W_TILE), jnp.float32),
            pltpu.VMEM((W_OUT_STREAM_SLOTS, W_TILE, D_MODEL), jnp.float32),
            pltpu.VMEM((STATE_STREAM_SLOTS, group, B_HEADS, B_DK, B_DV), jnp.float32),
            pltpu.SemaphoreType.DMA((3, max(W_IN_STREAM_SLOTS, W_OUT_STREAM_SLOTS, STATE_STREAM_SLOTS))),
            pltpu.SemaphoreType.DMA(()),
        ],
        compiler_params=pltpu.CompilerParams(
            dimension_semantics=("arbitrary", "arbitrary"), vmem_limit_bytes=VMEM_LIMIT_BYTES),
        name="sample_trunk",
    )(x, state, cos, sin, gn, w_in, gsgu, ws, bs, dec, rdec, cdec, sdec, w_out, gfin)


def kernel(x_prompt, x_sample, state_ret, g_norm, w_in, g_sgu, w_s, b_s, w_out, g_final):
    batch, seq, _ = x_prompt.shape
    dec_batch, dec_seq, _ = x_sample.shape
    assert batch == 1 and seq % PROMPT_TILE == 0
    assert PROMPT_TILE % RET_BLOCK == 0 and RET_BLOCK % A_CHUNK == 0 and RET_BLOCK % ROPE_SPLIT == 0
    assert dec_seq <= CHUNK and dec_batch % SAMPLE_GROUP == 0
    assert (dec_batch * dec_seq) % (SAMPLE_GATE_GROUP * dec_seq) == 0
    assert D_A % W_TILE == 0 and W_TILE % B_DK == 0 and W_TILE % A_DH == 0

    gfin = g_final.reshape(1, D_MODEL)
    gn = g_norm.reshape(DEPTH, 1, D_MODEL)
    gsgu = g_sgu.reshape(DEPTH, 1, D_A)

    cos_s, sin_s = _rope_tables(PAST_LEN, dec_seq)
    cos_s, sin_s = jnp.tile(cos_s, (SAMPLE_GROUP, 1)), jnp.tile(sin_s, (SAMPLE_GROUP, 1))
    ws_s, bs_s = _gate_weights(w_s, b_s, dec_seq, SAMPLE_GATE_GROUP)
    hs, new_ret_sample, vn, win_bf, wout_bf = _sample_trunk(
        x_sample.reshape(dec_batch * dec_seq, D_MODEL), state_ret, cos_s, sin_s, gn, w_in, gsgu, ws_s, bs_s,
        _decay_tables(dec_seq, SAMPLE_GROUP), w_out, gfin, dec_seq)

    rope_hi, rope_lo = _rope_split_tables(seq)
    decs_p = _decay_tables(RET_BLOCK)
    ws_p, bs_p = _gate_weights(w_s, b_s, A_CHUNK)
    ws_p = ws_p.reshape(DEPTH, A_HEADS // 2, 2, A_CHUNK, A_CHUNK).transpose(0, 1, 3, 2, 4)
    ws_p = ws_p.reshape(DEPTH, A_HEADS // 2, A_CHUNK, 2 * A_CHUNK)
    hp = x_prompt.reshape(seq, D_MODEL)
    ret_p = []
    for l in range(DEPTH):
        hp, sp = _prompt_layer(l, hp, rope_hi, rope_lo, gn[l], win_bf, gsgu[l], ws_p[l], bs_p[l], decs_p, wout_bf,
                               gfin, l == DEPTH - 1, l > 0)
        ret_p.append(sp.reshape(batch, B_HEADS, B_DK, B_DV))

    y_prompt = hp.reshape(batch, seq, D_MODEL)
    y_sample = hs.reshape(dec_batch, dec_seq, D_MODEL)
    new_chunk_v = vn.reshape(DEPTH, dec_batch, dec_seq, A_HEADS, A_DH)
    return (y_prompt, y_sample, jnp.stack(ret_p, axis=0), new_ret_sample, new_chunk_v)
```

```python
import functools

import jax
import jax.numpy as jnp
from jax import lax
from jax.experimental import pallas as pl
from jax.experimental.pallas import tpu as pltpu

D_MODEL = 1024
DEPTH = 4
PAST_LEN = 4096
CHUNK = 64
D_MIX = 2 * D_MODEL
D_A = D_MIX // 2
A_HEADS = 8
A_DH = D_A // A_HEADS
A_CHUNK = 128
D_B = D_MIX - D_A
B_HEADS = 4
B_DK = D_B // B_HEADS
B_DV = D_B // B_HEADS
D_PROJ = 3 * D_A + 4 * D_B
ROPE_BASE = 10000.0
EPS = 1e-6
HALF = B_DK // 2
LANES = 128
ROPE_SPLIT = 128

OFF_U, OFF_V, OFF_ZA = 0, D_A, 2 * D_A
OFF_Q, OFF_K, OFF_VB, OFF_ZB = 3 * D_A, 3 * D_A + D_B, 3 * D_A + 2 * D_B, 3 * D_A + 3 * D_B

RET_BLOCK = 256
PROMPT_TILE = 512

SAMPLE_GROUP = 2
SAMPLE_GATE_GROUP = 4
W_TILE = 512
N_IN_TILES = D_PROJ // W_TILE
N_OUT_TILES = D_MIX // W_TILE
W_IN_STREAM_SLOTS = 4
W_OUT_STREAM_SLOTS = 3
STATE_STREAM_SLOTS = 3

VMEM_LIMIT_BYTES = 56 * 1024 * 1024


def _bf(x):
    return x.astype(jnp.bfloat16)


def _dot(a, b):
    return jnp.dot(a, b, preferred_element_type=jnp.float32)


def _dot_nt(a, b):
    return lax.dot_general(a, b, (((1,), (1,)), ((), ())), preferred_element_type=jnp.float32)


def _dot_tn(a, b):
    return lax.dot_general(a, b, (((0,), (0,)), ((), ())), preferred_element_type=jnp.float32)


def _rms(x):
    return x * lax.rsqrt(jnp.mean(x * x, axis=-1, keepdims=True) + EPS)


def _silu(z):
    half = 0.5 * z
    return half * jnp.tanh(half) + half


def _rotary(x, cos, sin):
    x1, x2 = x[:, :HALF], x[:, HALF:]
    return jnp.concatenate([x1 * cos - x2 * sin, x1 * sin + x2 * cos], axis=-1)


def _wide(t):
    return jnp.concatenate([t, t], axis=-1)


def _head_norm(v, gsgu):
    parts = [_rms(v[:, h * A_DH:(h + 1) * A_DH]) for h in range(v.shape[1] // A_DH)]
    return jnp.concatenate(parts, axis=-1) * gsgu


def _spatial_gate(vn, ws_ref, head0, bias):
    c = ws_ref.shape[-1]
    rows = []
    for r in range(vn.shape[0] // c):
        cols = [_dot(ws_ref[head0 + h], vn[r * c:(r + 1) * c, h * A_DH:(h + 1) * A_DH])
                for h in range(vn.shape[1] // A_DH)]
        rows.append(jnp.concatenate(cols, axis=-1) + bias)
    return jnp.concatenate(rows, axis=0)


def _spatial_gate_paired(vn, wpair_ref, bias):
    c = wpair_ref.shape[1]
    n_chunks = vn.shape[0] // c
    zeros = jnp.zeros((c, A_DH), vn.dtype)
    outs = []
    for j in range(wpair_ref.shape[0]):
        top, bottom = [], []
        for r in range(n_chunks):
            top += [vn[r * c:(r + 1) * c, (2 * j) * A_DH:(2 * j + 1) * A_DH], zeros]
            bottom += [zeros, vn[r * c:(r + 1) * c, (2 * j + 1) * A_DH:(2 * j + 2) * A_DH]]
        rhs = jnp.concatenate([jnp.concatenate(top, axis=1), jnp.concatenate(bottom, axis=1)], axis=0)
        outs.append(_dot(wpair_ref[j], rhs))
    rows = [jnp.concatenate([o[:, r * 2 * A_DH:(r + 1) * 2 * A_DH] for o in outs], axis=1) + bias
            for r in range(n_chunks)]
    return jnp.concatenate(rows, axis=0)


def _retention_front(q, k, v, states, cdec):
    n = len(states)
    seq = q.shape[0] // n
    qb, kb, vb = _bf(q), _bf(k), _bf(v)
    kw = _bf(k * _wide(cdec))
    scores = _dot_nt(qb, kb)
    cross = [_dot(qb[i * seq:(i + 1) * seq], _bf(states[i])) for i in range(n)]
    return scores, cross, kw, vb


def _retention_back(front, states, dec, rdec, sdec):
    scores, cross, kw, vb = front
    n = len(states)
    seq = vb.shape[0] // n
    updates = [_dot_tn(kw[i * seq:(i + 1) * seq], vb[i * seq:(i + 1) * seq]) for i in range(n)]
    intra = _dot(_bf(scores * dec), vb)
    cross = (cross[0] if n == 1 else jnp.concatenate(cross, axis=0)) * _wide(rdec)
    new_states = [sdec * states[i] + updates[i] for i in range(n)]
    return intra + cross, new_states


def _rope_rows(rope_hi_ref, rope_lo_ref, block):
    ch = rope_hi_ref[0, pl.ds(block, 1), :]
    sh = rope_hi_ref[1, pl.ds(block, 1), :]
    cl = rope_lo_ref[0]
    sl = rope_lo_ref[1]
    return ch * cl - sh * sl, sh * cl + ch * sl


def _prompt_layer_kernel(x_ref, rope_hi_ref, rope_lo_ref, gn_ref, win_ref, gsgu_ref, ws_ref, bs_ref,
                         dec_ref, rdec_ref, cdec_ref, sdec_ref, wout_ref, gfin_ref,
                         y_ref, s_ref, mix_ref, *, final_norm):
    tile = x_ref.shape[0]
    block0 = pl.program_id(0) * (tile // ROPE_SPLIT)

    @pl.when(pl.program_id(0) == 0)
    def _():
        s_ref[...] = jnp.zeros_like(s_ref)

    x_all = x_ref[...]
    h_all = _bf(_rms(x_all) * gn_ref[...])
    v_all = _dot(h_all, win_ref[:, OFF_V:OFF_V + D_A])
    u_all = _dot(h_all, win_ref[:, OFF_U:OFF_U + D_A])

    def start_block(blk):
        r = slice(blk * RET_BLOCK, (blk + 1) * RET_BLOCK)
        return r, x_all[r], h_all[r], u_all[r], v_all[r]

    n_blocks = tile // RET_BLOCK
    rows_a = D_A + (B_HEADS - 2) * B_DV
    rows_b = rows_a + B_DV
    started = start_block(0)
    for blk in range(n_blocks):
        r, x, h, u, v = started

        vn = _bf(_head_norm(v, gsgu_ref[...]))
        mixed = _spatial_gate_paired(vn, ws_ref, bs_ref[...])
        za = _dot(h, win_ref[:, OFF_ZA:OFF_ZA + D_A])
        mix_ref[r, 0:D_A] = _bf(u * mixed * _silu(za))

        parts = [_rope_rows(rope_hi_ref, rope_lo_ref, block0 + blk * (RET_BLOCK // ROPE_SPLIT) + j)
                 for j in range(RET_BLOCK // ROPE_SPLIT)]
        cos = jnp.concatenate([p[0] for p in parts], axis=0)
        sin = jnp.concatenate([p[1] for p in parts], axis=0)

        def project(hb):
            c0 = hb * B_DK
            k = _rotary(_dot(h, win_ref[:, OFF_K + c0:OFF_K + c0 + B_DK]), cos, sin)
            q = _rotary(_dot(h, win_ref[:, OFF_Q + c0:OFF_Q + c0 + B_DK]), cos, sin)
            vb = _dot(h, win_ref[:, OFF_VB + c0:OFF_VB + c0 + B_DV])
            zb = _dot(h, win_ref[:, OFF_ZB + c0:OFF_ZB + c0 + B_DV])
            return q, k, vb, zb

        proj = project(0)
        for hb in range(B_HEADS):
            q, k, vb, zb = proj
            state = [s_ref[hb]]
            front = _retention_front(q, k, vb, state, cdec_ref[hb])
            if hb + 1 < B_HEADS:
                proj = project(hb + 1)
            else:
                y = x + _dot(mix_ref[r, 0:rows_a], wout_ref[0:rows_a, :])
            o, (s_new,) = _retention_back(front, state, dec_ref[hb], rdec_ref[hb], sdec_ref[hb])
            s_ref[hb] = s_new
            mix_ref[r, D_A + hb * B_DV:D_A + (hb + 1) * B_DV] = _bf(_rms(o) * _silu(zb))

        y = y + _dot(mix_ref[r, rows_a:rows_b], wout_ref[rows_a:rows_b, :])
        if blk + 1 < n_blocks:
            started = start_block(blk + 1)
        y = y + _dot(mix_ref[r, rows_b:D_MIX], wout_ref[rows_b:D_MIX, :])
        if final_norm:
            y = _rms(y) * gfin_ref[...]
        y_ref[r, :] = y


class _TileStream:
    def __init__(self, hbm, stage, sems, sem_row, tiles_per_layer, tile_view):
        self.hbm, self.stage, self.sems, self.sem_row = hbm, stage, sems, sem_row
        self.tiles_per_layer, self.tile_view = tiles_per_layer, tile_view
        self.n_tiles = hbm.shape[0] * tiles_per_layer
        self.n_slots = stage.shape[0]
        self.lookahead = self.n_slots - 1

    def _copy(self, n):
        slot = n % self.n_slots
        src = self.tile_view(self.hbm, n // self.tiles_per_layer, n % self.tiles_per_layer)
        return pltpu.make_async_copy(src, self.stage.at[slot], self.sems.at[self.sem_row, slot])

    def prime(self):
        for n in range(self.lookahead):
            self._copy(n).start()

    def take(self, n):
        @pl.when(n + self.lookahead < self.n_tiles)
        def _():
            self._copy(n + self.lookahead).start()

        self._copy(n).wait()
        return self.stage.at[n % self.n_slots]


def _tile_start(t):
    return t * W_TILE if isinstance(t, int) else pl.multiple_of(t * W_TILE, W_TILE)


def _win_tile(hbm, layer, t):
    return hbm.at[layer, :, pl.ds(_tile_start(t), W_TILE)]


def _wout_tile(hbm, layer, t):
    return hbm.at[layer, pl.ds(_tile_start(t), W_TILE), :]


def _state_group(hbm, layer, g):
    return hbm.at[layer, pl.ds(g * SAMPLE_GROUP, SAMPLE_GROUP)]


def _sample_kernel(x_hbm, s_in_hbm, cos_ref, sin_ref, gn_ref, win_hbm, gsgu_ref, ws_ref, bs_ref,
                   dec_ref, rdec_ref, cdec_ref, sdec_ref, wout_hbm, gfin_ref,
                   y_ref, s_out_ref, vn_ref, win_bf_ref, wout_bf_ref,
                   h_ref, p_ref, mix_ref, win_stage, wout_stage, state_stage, sems, x_sem, *, seq):
    layer = pl.program_id(0)
    step = pl.program_id(1)
    group = SAMPLE_GROUP
    n_groups = y_ref.shape[0] // (group * seq)
    tiles_per_block = D_A // W_TILE
    win_stream = _TileStream(win_hbm, win_stage, sems, 0, N_IN_TILES, _win_tile)
    wout_stream = _TileStream(wout_hbm, wout_stage, sems, 1, N_OUT_TILES, _wout_tile)
    state_stream = _TileStream(s_in_hbm, state_stage, sems, 2, n_groups, _state_group)

    @pl.when((layer == 0) & (step == 0))
    def _():
        win_stream.prime()
        wout_stream.prime()
        state_stream.prime()
        x_copy = pltpu.make_async_copy(x_hbm, y_ref, x_sem)
        x_copy.start()
        x_copy.wait()

    @pl.when(step == 0)
    def _():
        h_ref[...] = _bf(_rms(y_ref[...]) * gn_ref[...])

    @pl.when(step < N_IN_TILES)
    def _():
        w = _bf(win_stream.take(layer * N_IN_TILES + step)[...])
        win_bf_ref[...] = w
        p_ref[step] = _dot(h_ref[...], w)

    for t in range(tiles_per_block):
        @pl.when(step == 2 * tiles_per_block + t)
        def _(t=t):
            cols = slice(t * W_TILE, (t + 1) * W_TILE)
            u = p_ref[t]
            v = p_ref[tiles_per_block + t]
            za = p_ref[2 * tiles_per_block + t]
            vn = _head_norm(v, gsgu_ref[:, cols])
            vn_ref[...] = vn
            mixed = _spatial_gate(_bf(vn), ws_ref, t * (W_TILE // A_DH), bs_ref[:, cols])
            mix_ref[:, cols] = _bf(u * mixed * _silu(za))

    @pl.when((step >= N_IN_TILES) & (step < N_IN_TILES + n_groups))
    def _():
        r0 = pl.multiple_of((step - N_IN_TILES) * (group * seq), group * seq)
        rows = pl.ds(r0, group * seq)
        cos = cos_ref[...]
        sin = sin_ref[...]
        s_in = state_stream.take(layer * n_groups + step - N_IN_TILES)
        heads_per_tile = W_TILE // B_DK
        for hb in range(B_HEADS):
            t, c0 = hb // heads_per_tile, (hb % heads_per_tile) * B_DK
            blocks = [p_ref[(3 + i) * tiles_per_block + t, rows, c0:c0 + B_DK] for i in range(4)]
            q = _rotary(blocks[0], cos, sin)
            k = _rotary(blocks[1], cos, sin)
            states = [s_in[i, hb] for i in range(group)]
            front = _retention_front(q, k, blocks[2], states, cdec_ref[hb])
            o, new_states = _retention_back(front, states, dec_ref[hb], rdec_ref[hb], sdec_ref[hb])
            for i in range(group):
                s_out_ref[i, hb] = new_states[i]
            mix_ref[rows, D_A + hb * B_DV:D_A + (hb + 1) * B_DV] = _bf(_rms(o) * _silu(blocks[3]))

    @pl.when(step >= N_IN_TILES + n_groups)
    def _():
        w = _bf(wout_stream.take(layer * N_OUT_TILES + step - (N_IN_TILES + n_groups))[...])
        wout_bf_ref[...] = w
        for t in range(N_OUT_TILES):
            @pl.when(step == N_IN_TILES + n_groups + t)
            def _(t=t):
                y_ref[...] += _dot(mix_ref[:, t * W_TILE:(t + 1) * W_TILE], w)

    @pl.when((layer == pl.num_programs(0) - 1) & (step == pl.num_programs(1) - 1))
    def _():
        y_ref[...] = _rms(y_ref[...]) * gfin_ref[...]


def _const_spec(shape):
    zeros = (0,) * len(shape)
    return pl.BlockSpec(shape, lambda *_: zeros, pipeline_mode=pl.Buffered(1))


def _layer_spec(shape, layer):
    zeros = (0,) * len(shape)
    return pl.BlockSpec((None,) + shape, lambda i: (layer,) + zeros, pipeline_mode=pl.Buffered(1))


def _decay_tables(length, n_seq=1):
    lg = jnp.log(1.0 - 2.0 ** (-5.0 - jnp.arange(B_HEADS, dtype=jnp.float32)))
    idx = jnp.arange(length, dtype=jnp.float32)
    diff = idx[:, None] - idx[None, :]
    causal = diff >= 0
    dec = jnp.where(causal[None], jnp.exp(lg[:, None, None] * jnp.where(causal, diff, 0.0)[None]), 0.0)
    rdec = jnp.exp(lg[:, None] * (idx[None, :] + 1.0))
    cdec = jnp.exp(lg[:, None] * (length - 1.0 - idx[None, :]))
    sdec = jnp.exp(lg * length)
    dec = dec * (B_DK ** -0.5)
    cdec = cdec * (B_DK ** -0.5)
    dec = jnp.einsum("ab,hij->haibj", jnp.eye(n_seq, dtype=dec.dtype), dec)
    dec = dec.reshape(B_HEADS, n_seq * length, n_seq * length)
    rdec = jnp.broadcast_to(jnp.tile(rdec, (1, n_seq))[:, :, None], (B_HEADS, n_seq * length, LANES))
    cdec = jnp.broadcast_to(jnp.tile(cdec, (1, n_seq))[:, :, None], (B_HEADS, n_seq * length, LANES))
    sdec = jnp.broadcast_to(sdec[:, None, None], (B_HEADS, 1, B_DV))
    return dec, rdec, cdec, sdec


def _rope_tables(start, length):
    inv = ROPE_BASE ** (-jnp.arange(HALF, dtype=jnp.float32) / HALF)
    ang = (start + jnp.arange(length)).astype(jnp.float32)[:, None] * inv[None, :]
    return jnp.cos(ang), jnp.sin(ang)


def _rope_split_tables(length):
    assert length % ROPE_SPLIT == 0
    inv = ROPE_BASE ** (-jnp.arange(HALF, dtype=jnp.float32) / HALF)
    ang_hi = (ROPE_SPLIT * jnp.arange(length // ROPE_SPLIT)).astype(jnp.float32)[:, None] * inv[None, :]
    ang_lo = jnp.arange(ROPE_SPLIT).astype(jnp.float32)[:, None] * inv[None, :]
    return (jnp.stack([jnp.cos(ang_hi), jnp.sin(ang_hi)], axis=0),
            jnp.stack([jnp.cos(ang_lo), jnp.sin(ang_lo)], axis=0))


def _gate_weights(w_s, b_s, c, n_seq=1):
    i = jnp.arange(c)
    mask = (i[None, :] // CHUNK) <= (i[:, None] // CHUNK)
    w = jnp.where(mask, w_s[..., :c, :c], 0.0)
    w = jnp.einsum("ab,...ij->...aibj", jnp.eye(n_seq, dtype=w.dtype), w)
    w = _bf(w.reshape(w.shape[:-4] + (n_seq * c, n_seq * c)))
    bias = jnp.swapaxes(b_s[..., :c], -1, -2)
    bias = jnp.broadcast_to(bias[..., None], bias.shape + (A_DH,)).reshape(bias.shape[:-1] + (D_A,))
    bias = jnp.tile(bias, (1,) * (bias.ndim - 2) + (n_seq, 1))
    return w, bias


def _prompt_layer(layer, x, rope_hi, rope_lo, gn, win, gsgu, ws, bs, decs, wout, gfin, final_norm, in_place):
    seq = x.shape[0]
    tile = PROMPT_TILE
    dec, rdec, cdec, sdec = decs
    row = lambda i: (i, 0)
    kern = functools.partial(_prompt_layer_kernel, final_norm=final_norm)
    return pl.pallas_call(
        kern,
        grid=(seq // tile,),
        in_specs=[
            pl.BlockSpec((tile, D_MODEL), row),
            _const_spec(rope_hi.shape),
            _const_spec(rope_lo.shape),
            _const_spec((1, D_MODEL)),
            _layer_spec((D_MODEL, D_PROJ), layer),
            _const_spec((1, D_A)),
            _const_spec((A_HEADS // 2, A_CHUNK, 2 * A_CHUNK)),
            _const_spec((A_CHUNK, D_A)),
            _const_spec((B_HEADS, RET_BLOCK, RET_BLOCK)),
            _const_spec((B_HEADS, RET_BLOCK, LANES)),
            _const_spec((B_HEADS, RET_BLOCK, LANES)),
            _const_spec((B_HEADS, 1, B_DV)),
            _layer_spec((D_MIX, D_MODEL), layer),
            _const_spec((1, D_MODEL)),
        ],
        out_specs=[
            pl.BlockSpec((tile, D_MODEL), row),
            pl.BlockSpec((B_HEADS, B_DK, B_DV), lambda i: (0, 0, 0)),
        ],
        out_shape=[
            jax.ShapeDtypeStruct((seq, D_MODEL), jnp.float32),
            jax.ShapeDtypeStruct((B_HEADS, B_DK, B_DV), jnp.float32),
        ],
        scratch_shapes=[pltpu.VMEM((tile, D_MIX), jnp.bfloat16)],
        input_output_aliases={0: 0} if in_place else {},
        compiler_params=pltpu.CompilerParams(
            dimension_semantics=("arbitrary",), vmem_limit_bytes=VMEM_LIMIT_BYTES),
        name="prompt_layer",
    )(x, rope_hi, rope_lo, gn, win, gsgu, ws, bs, dec, rdec, cdec, sdec, wout, gfin)


def _sample_trunk(x, state, cos, sin, gn, w_in, gsgu, ws, bs, decs, w_out, gfin, seq):
    rows = x.shape[0]
    nb = rows // seq
    group = SAMPLE_GROUP
    n_groups = nb // group
    n_steps = N_IN_TILES + n_groups + N_OUT_TILES
    dec, rdec, cdec, sdec = decs
    gate_rows = ws.shape[-1]
    tiles_per_block = D_A // W_TILE

    def per_layer(shape):
        zeros = (0,) * len(shape)
        return pl.BlockSpec((None,) + shape, lambda l, s: (l,) + zeros)

    def state_map(l, s):
        return (l, jnp.clip(s - N_IN_TILES, 0, n_groups - 1), 0, 0, 0)

    state_spec = pl.BlockSpec((None, group, B_HEADS, B_DK, B_DV), state_map)
    kern = functools.partial(_sample_kernel, seq=seq)
    return pl.pallas_call(
        kern,
        grid=(DEPTH, n_steps),
        in_specs=[
            pl.BlockSpec(memory_space=pl.ANY),
            pl.BlockSpec(memory_space=pl.ANY),
            _const_spec((group * seq, HALF)),
            _const_spec((group * seq, HALF)),
            per_layer((1, D_MODEL)),
            pl.BlockSpec(memory_space=pl.ANY),
            per_layer((1, D_A)),
            per_layer((A_HEADS, gate_rows, gate_rows)),
            per_layer((gate_rows, D_A)),
            _const_spec((B_HEADS, group * seq, group * seq)),
            _const_spec((B_HEADS, group * seq, LANES)),
            _const_spec((B_HEADS, group * seq, LANES)),
            _const_spec((B_HEADS, 1, B_DV)),
            pl.BlockSpec(memory_space=pl.ANY),
            _const_spec((1, D_MODEL)),
        ],
        out_specs=[
            pl.BlockSpec((rows, D_MODEL), lambda l, s: (0, 0)),
            state_spec,
            pl.BlockSpec((None, rows, W_TILE),
                         lambda l, s: (l, 0, jnp.clip(s - 2 * tiles_per_block, 0, tiles_per_block - 1))),
            pl.BlockSpec((None, D_MODEL, W_TILE), lambda l, s: (l, 0, jnp.minimum(s, N_IN_TILES - 1))),
            pl.BlockSpec((None, W_TILE, D_MODEL),
                         lambda l, s: (l, jnp.clip(s - N_IN_TILES - n_groups, 0, N_OUT_TILES - 1), 0)),
        ],
        out_shape=[
            jax.ShapeDtypeStruct((rows, D_MODEL), jnp.float32),
            jax.ShapeDtypeStruct(state.shape, jnp.float32),
            jax.ShapeDtypeStruct((DEPTH, rows, D_A), jnp.float32),
            jax.ShapeDtypeStruct(w_in.shape, jnp.bfloat16),
            jax.ShapeDtypeStruct(w_out.shape, jnp.bfloat16),
        ],
        scratch_shapes=[
            pltpu.VMEM((rows, D_MODEL), jnp.bfloat16),
            pltpu.VMEM((N_IN_TILES, rows, W_TILE), jnp.float32),
            pltpu.VMEM((rows, D_MIX), jnp.bfloat16),
            pltpu.VMEM((W_IN_STREAM_SLOTS, D_MODEL, W_TILE), jnp.float32),
            pltpu.VMEM((W_OUT_STREAM_SLOTS, W_TILE, D_MODEL), jnp.float32),
            pltpu.VMEM((STATE_STREAM_SLOTS, group, B_HEADS, B_DK, B_DV), jnp.float32),
            pltpu.SemaphoreType.DMA((3, max(W_IN_STREAM_SLOTS, W_OUT_STREAM_SLOTS, STATE_STREAM_SLOTS))),
            pltpu.SemaphoreType.DMA(()),
        ],
        compiler_params=pltpu.CompilerParams(
            dimension_semantics=("arbitrary", "arbitrary"), vmem_limit_bytes=VMEM_LIMIT_BYTES),
        name="sample_trunk",
    )(x, state, cos, sin, gn, w_in, gsgu, ws, bs, dec, rdec, cdec, sdec, w_out, gfin)


def kernel(x_prompt, x_sample, state_ret, g_norm, w_in, g_sgu, w_s, b_s, w_out, g_final):
    batch, seq, _ = x_prompt.shape
    dec_batch, dec_seq, _ = x_sample.shape
    assert batch == 1 and seq % PROMPT_TILE == 0
    assert PROMPT_TILE % RET_BLOCK == 0 and RET_BLOCK % A_CHUNK == 0 and RET_BLOCK % ROPE_SPLIT == 0
    assert dec_seq <= CHUNK and dec_batch % SAMPLE_GROUP == 0
    assert (dec_batch * dec_seq) % (SAMPLE_GATE_GROUP * dec_seq) == 0
    assert D_A % W_TILE == 0 and W_TILE % B_DK == 0 and W_TILE % A_DH == 0

    gfin = g_final.reshape(1, D_MODEL)
    gn = g_norm.reshape(DEPTH, 1, D_MODEL)
    gsgu = g_sgu.reshape(DEPTH, 1, D_A)

    cos_s, sin_s = _rope_tables(PAST_LEN, dec_seq)
    cos_s, sin_s = jnp.tile(cos_s, (SAMPLE_GROUP, 1)), jnp.tile(sin_s, (SAMPLE_GROUP, 1))
    ws_s, bs_s = _gate_weights(w_s, b_s, dec_seq, SAMPLE_GATE_GROUP)
    hs, new_ret_sample, vn, win_bf, wout_bf = _sample_trunk(
        x_sample.reshape(dec_batch * dec_seq, D_MODEL), state_ret, cos_s, sin_s, gn, w_in, gsgu, ws_s, bs_s,
        _decay_tables(dec_seq, SAMPLE_GROUP), w_out, gfin, dec_seq)

    rope_hi, rope_lo = _rope_split_tables(seq)
    decs_p = _decay_tables(RET_BLOCK)
    ws_p, bs_p = _gate_weights(w_s, b_s, A_CHUNK)
    ws_p = ws_p.reshape(DEPTH, A_HEADS // 2, 2, A_CHUNK, A_CHUNK).transpose(0, 1, 3, 2, 4)
    ws_p = ws_p.reshape(DEPTH, A_HEADS // 2, A_CHUNK, 2 * A_CHUNK)
    hp = x_prompt.reshape(seq, D_MODEL)
    ret_p = []
    for l in range(DEPTH):
        hp, sp = _prompt_layer(l, hp, rope_hi, rope_lo, gn[l], win_bf, gsgu[l], ws_p[l], bs_p[l], decs_p, wout_bf,
                               gfin, l == DEPTH - 1, l > 0)
        ret_p.append(sp.reshape(batch, B_HEADS, B_DK, B_DV))

    y_prompt = hp.reshape(batch, seq, D_MODEL)
    y_sample = hs.reshape(dec_batch, dec_seq, D_MODEL)
    new_chunk_v = vn.reshape(DEPTH, dec_batch, dec_seq, A_HEADS, A_DH)
    return (y_prompt, y_sample, jnp.stack(ret_p, axis=0), new_ret_sample, new_chunk_v)
```
